```python
import math
import jax, jax.numpy as jnp
from jax import lax
import numpy as np

D_MODEL = 1024
BATCH = 4
SEQ = 4096
DEPTH = 2

GRID_W = 64
CTX_LEN = 256
N_MOD = 9
EPS = 1e-6
D_FF = 2816
FFN_RES = 0.5

HEAD_DIM = 64
ROPE_BASE = 10000.0

LRU_WIDTH = 256
LRU_BLOCKS = 4
LRU_BLOCK = LRU_WIDTH // LRU_BLOCKS
CONV_WIDTH = 4
CONV_LEFT = 2
LRU_C = 8.0

SWA_HEADS = 4
SWA_KV_HEADS = 2
SWA_GROUP = SWA_HEADS // SWA_KV_HEADS
WINDOW = 128
QBLK = 128

DIFF_HEADS = 4
DIFF_V_DIM = 2 * HEAD_DIM

IN_SIZES = (LRU_WIDTH, LRU_WIDTH,
            SWA_HEADS * HEAD_DIM, SWA_KV_HEADS * HEAD_DIM, SWA_KV_HEADS * HEAD_DIM,
            DIFF_HEADS * 2 * HEAD_DIM, DIFF_HEADS * 2 * HEAD_DIM, DIFF_HEADS * DIFF_V_DIM)
IN_WIDTH = sum(IN_SIZES)
IN_SPLITS = tuple(int(s) for s in np.cumsum(IN_SIZES)[:-1])
MIX_WIDTH = LRU_WIDTH + SWA_HEADS * HEAD_DIM + DIFF_HEADS * DIFF_V_DIM
NEG_INF = -1e30

kernel_name = 'hybrid_prefix_dit_block'


def rmsnorm(x, g):
    xf = x.astype(jnp.float32)
    y = xf * lax.rsqrt(jnp.mean(xf * xf, axis=-1, keepdims=True) + EPS)
    return (y * g.astype(jnp.float32)).astype(x.dtype)


def modulated_norm(x, g, mod, k):
    shift = mod[:, 3 * k][:, None]
    scale = mod[:, 3 * k + 1][:, None]
    return rmsnorm(x, g) * (1 + scale) + shift


def gate_of(mod, k):
    return mod[:, 3 * k + 2][:, None]


def swiglu(x, w_gu, w_down):
    g, u = jnp.split(x @ w_gu, 2, axis=-1)
    return (jax.nn.silu(g) * u) @ w_down


def ffn_sublayer(h, g, mod, k, w_gu, w_down):
    return h + FFN_RES * gate_of(mod, k) * swiglu(modulated_norm(h, g, mod, k), w_gu, w_down)


def axial_rope_tables(n, dtype):
    rows = n // GRID_W
    row = jnp.repeat(jnp.arange(rows, dtype=jnp.float32), GRID_W)
    col = jnp.tile(jnp.arange(GRID_W, dtype=jnp.float32), rows)
    n_freq = HEAD_DIM // 4
    inv_freq = ROPE_BASE ** (-jnp.arange(n_freq, dtype=jnp.float32) / n_freq)
    ang = jnp.concatenate([row[:, None] * inv_freq, col[:, None] * inv_freq], axis=-1)
    return jnp.cos(ang).astype(dtype), jnp.sin(ang).astype(dtype)


def apply_rope(x, cos, sin):
    x1, x2 = jnp.split(x, 2, axis=-1)
    c = cos[None, :, None]
    s = sin[None, :, None]
    return jnp.concatenate([x1 * c - x2 * s, x1 * s + x2 * c], axis=-1)


def depthwise_conv(x, w, b):
    n = x.shape[1]
    xp = jnp.pad(x, ((0, 0), (CONV_LEFT, CONV_WIDTH - 1 - CONV_LEFT), (0, 0)))
    y = b
    for k in range(CONV_WIDTH):
        y = y + xp[:, k:k + n] * w[k]
    return y


def rglru_coeffs(u, w_r, b_r, w_i, b_i, lam):
    B, n, W = u.shape
    uf = u.astype(jnp.float32)
    ub = uf.reshape(B, n, LRU_BLOCKS, LRU_BLOCK)
    r = jax.nn.sigmoid(jnp.einsum('bnkc,kcd->bnkd', ub, w_r.astype(jnp.float32)).reshape(B, n, W) + b_r.astype(jnp.float32))
    i = jax.nn.sigmoid(jnp.einsum('bnkc,kcd->bnkd', ub, w_i.astype(jnp.float32)).reshape(B, n, W) + b_i.astype(jnp.float32))
    log_a = -LRU_C * r * jax.nn.softplus(-lam.astype(jnp.float32))
    a = jnp.exp(log_a)
    mult = jnp.sqrt(-jnp.expm1(2.0 * log_a))
    return a, mult * i * uf


def linear_scan(a, b, h0, reverse):
    if h0 is not None:
        edge = -1 if reverse else 0
        b = b.at[:, edge].add(a[:, edge] * h0)

    def combine(left, right):
        a_l, b_l = left
        a_r, b_r = right
        return a_l * a_r, a_r * b_l + b_r

    _, h = lax.associative_scan(combine, (a, b), reverse=reverse, axis=1)
    return h


def sink_softmax(logits, sink):
    sink_b = jnp.broadcast_to(sink, logits.shape[:-1] + (1,))
    return jax.nn.softmax(jnp.concatenate([logits, sink_b], axis=-1), axis=-1)


def window_attention_latent(q, k, v, kc, vc, sink):
    B, S, _, d = q.shape
    nb = S // QBLK
    scale = d ** -0.5
    qb = q.reshape(B, nb, QBLK, SWA_KV_HEADS, SWA_GROUP, d)

    def band(t):
        tp = jnp.pad(t, ((0, 0), (QBLK, QBLK), (0, 0), (0, 0))).reshape(B, nb + 2, QBLK, SWA_KV_HEADS, d)
        return jnp.concatenate([tp[:, :-2], tp[:, 1:-1], tp[:, 2:]], axis=2)

    kband, vband = band(k), band(v)
    s_loc = jnp.einsum('bnqhgd,bnkhd->bnhgqk', qb, kband).astype(jnp.float32) * scale
    blk = jnp.arange(nb)[:, None, None]
    q_abs = blk * QBLK + jnp.arange(QBLK)[None, :, None]
    k_abs = (blk - 1) * QBLK + jnp.arange(3 * QBLK)[None, None, :]
    valid = (k_abs >= 0) & (k_abs < S) & (jnp.abs(q_abs - k_abs) <= WINDOW)
    s_loc = jnp.where(valid[None, :, None, None], s_loc, NEG_INF)
    s_ctx = jnp.einsum('bnqhgd,bchd->bnhgqc', qb, kc).astype(jnp.float32) * scale
    sink_r = sink.astype(jnp.float32).reshape(SWA_KV_HEADS, SWA_GROUP)[None, None, :, :, None, None]
    p = sink_softmax(jnp.concatenate([s_loc, s_ctx], axis=-1), sink_r)
    n_loc = 3 * QBLK
    p_loc = p[..., :n_loc].astype(v.dtype)
    p_ctx = p[..., n_loc:n_loc + kc.shape[1]].astype(v.dtype)
    o = jnp.einsum('bnhgqk,bnkhd->bnqhgd', p_loc, vband) + jnp.einsum('bnhgqc,bchd->bnqhgd', p_ctx, vc)
    return o.reshape(B, S, SWA_HEADS * d)


def window_attention_context(qc, kc, vc, sink):
    B, L, _, d = qc.shape
    qg = qc.reshape(B, L, SWA_KV_HEADS, SWA_GROUP, d)
    s = jnp.einsum('bqhgd,bkhd->bhgqk', qg, kc).astype(jnp.float32) * d ** -0.5
    sink_r = sink.astype(jnp.float32).reshape(SWA_KV_HEADS, SWA_GROUP)[None, :, :, None, None]
    p = sink_softmax(s, sink_r)[..., :-1].astype(vc.dtype)
    o = jnp.einsum('bhgqk,bkhd->bqhgd', p, vc)
    return o.reshape(B, L, SWA_HEADS * d)


def diff_attend(q, k, v, lam):
    d = q.shape[-1]
    s = jnp.einsum('bqhmd,bkhmd->bhmqk', q, k).astype(jnp.float32) * d ** -0.5
    p = jax.nn.softmax(s, axis=-1)
    w = (p[:, :, 0] - lam * p[:, :, 1]).astype(v.dtype)
    return jnp.einsum('bhqk,bkhe->bqhe', w, v)


def diff_attention_latent(q, k, v, kc, vc, lam):
    B, S, H, _, d = q.shape
    nb = S // QBLK
    k_all = jnp.concatenate([k, kc], axis=1)
    v_all = jnp.concatenate([v, vc], axis=1)
    qb = jnp.moveaxis(q.reshape(B, nb, QBLK, H, 2, d), 1, 0)
    o = lax.map(lambda qblk: diff_attend(qblk, k_all, v_all, lam), qb)
    return jnp.moveaxis(o, 0, 1).reshape(B, S, H, 2 * d)


def token_mixer(n, nc, w_in, w_out, conv_w, conv_b, lru_w_r, lru_b_r, lru_w_i, lru_b_i, lru_lambda,
                swa_sink, diff_lambda, diff_subln_g, lambda_init, with_ctx_out):
    B, S, _ = n.shape
    Lc = nc.shape[1]
    lx, lg, sq, sk, sv, dq, dk, dv = jnp.split(n @ w_in, IN_SPLITS, axis=-1)
    lxc, lgc, sqc, skc, svc, dqc, dkc, dvc = jnp.split(nc @ w_in, IN_SPLITS, axis=-1)
    cos, sin = axial_rope_tables(S, n.dtype)

    u = depthwise_conv(lx, conv_w, conv_b)
    uc = depthwise_conv(lxc, conv_w, conv_b)
    h_lat = 0.0
    h_ctx = 0.0
    for dirn, reverse in ((0, False), (1, True)):
        prm = (lru_w_r[dirn], lru_b_r[dirn], lru_w_i[dirn], lru_b_i[dirn], lru_lambda[dirn])
        a_c, b_c = rglru_coeffs(uc, *prm)
        hc = linear_scan(a_c, b_c, None, reverse)
        h0 = hc[:, 0] if reverse else hc[:, -1]
        a_l, b_l = rglru_coeffs(u, *prm)
        h_lat = h_lat + linear_scan(a_l, b_l, h0, reverse)
        if with_ctx_out:
            h_ctx = h_ctx + hc
    y_lru = h_lat.astype(n.dtype) * jax.nn.gelu(lg)

    q_s = apply_rope(sq.reshape(B, S, SWA_HEADS, HEAD_DIM), cos, sin)
    k_s = apply_rope(sk.reshape(B, S, SWA_KV_HEADS, HEAD_DIM), cos, sin)
    v_s = sv.reshape(B, S, SWA_KV_HEADS, HEAD_DIM)
    kc_s = skc.reshape(B, Lc, SWA_KV_HEADS, HEAD_DIM)
    vc_s = svc.reshape(B, Lc, SWA_KV_HEADS, HEAD_DIM)
    y_swa = window_attention_latent(q_s, k_s, v_s, kc_s, vc_s, swa_sink)

    lam_f = diff_lambda.astype(jnp.float32)
    lam = jnp.exp(jnp.sum(lam_f[0] * lam_f[1])) - jnp.exp(jnp.sum(lam_f[2] * lam_f[3])) + lambda_init
    q_d = apply_rope(dq.reshape(B, S, DIFF_HEADS * 2, HEAD_DIM), cos, sin).reshape(B, S, DIFF_HEADS, 2, HEAD_DIM)
    k_d = apply_rope(dk.reshape(B, S, DIFF_HEADS * 2, HEAD_DIM), cos, sin).reshape(B, S, DIFF_HEADS, 2, HEAD_DIM)
    v_d = dv.reshape(B, S, DIFF_HEADS, DIFF_V_DIM)
    kc_d = dkc.reshape(B, Lc, DIFF_HEADS, 2, HEAD_DIM)
    vc_d = dvc.reshape(B, Lc, DIFF_HEADS, DIFF_V_DIM)
    o_d = diff_attention_latent(q_d, k_d, v_d, kc_d, vc_d, lam)
    y_diff = (rmsnorm(o_d, diff_subln_g) * (1 - lambda_init)).reshape(B, S, DIFF_HEADS * DIFF_V_DIM)

    y = jnp.concatenate([y_lru, y_swa, y_diff], axis=-1) @ w_out
    if not with_ctx_out:
        return y, None

    yc_lru = h_ctx.astype(nc.dtype) * jax.nn.gelu(lgc)
    yc_swa = window_attention_context(sqc.reshape(B, Lc, SWA_HEADS, HEAD_DIM), kc_s, vc_s, swa_sink)
    oc_d = diff_attend(dqc.reshape(B, Lc, DIFF_HEADS, 2, HEAD_DIM), kc_d, vc_d, lam)
    yc_diff = (rmsnorm(oc_d, diff_subln_g) * (1 - lambda_init)).reshape(B, Lc, DIFF_HEADS * DIFF_V_DIM)
    yc = jnp.concatenate([yc_lru, yc_swa, yc_diff], axis=-1) @ w_out
    return y, yc


def setup_inputs(seed: int = 0) -> dict:
    key = jax.random.key(seed)
    ks = jax.random.split(key, 24)
    f32 = jnp.float32
    D = D_MODEL

    def nrm(k, shape, scale):
        return jax.random.normal(k, shape, f32) * scale

    u = jax.random.uniform(ks[19], (DEPTH, 2, LRU_WIDTH), f32, 0.9, 0.999)
    a0 = u ** (1.0 / LRU_C)
    return {
        'x': nrm(ks[0], (BATCH, SEQ, D), 1.0),
        'c': nrm(ks[1], (BATCH, D), 1.0),
        'ctx': nrm(ks[2], (BATCH, CTX_LEN, D), 1.0),
        'c_ctx': nrm(ks[3], (D,), 1.0),
        'w_ada': nrm(ks[4], (DEPTH, D, N_MOD * D), 0.5 * D ** -0.5),
        'b_ada': nrm(ks[5], (DEPTH, N_MOD * D), 0.02),
        'norm_g': 1.0 + nrm(ks[6], (DEPTH, 3, D), 0.02),
        'ffn1_w_gu': nrm(ks[7], (DEPTH, D, 2 * D_FF), D ** -0.5),
        'ffn1_w_down': nrm(ks[8], (DEPTH, D_FF, D), D_FF ** -0.5),
        'ffn2_w_gu': nrm(ks[9], (DEPTH, D, 2 * D_FF), D ** -0.5),
        'ffn2_w_down': nrm(ks[10], (DEPTH, D_FF, D), D_FF ** -0.5),
        'w_in': nrm(ks[11], (DEPTH, D, IN_WIDTH), D ** -0.5),
        'w_out': nrm(ks[12], (DEPTH, MIX_WIDTH, D), MIX_WIDTH ** -0.5),
        'conv_w': nrm(ks[13], (DEPTH, CONV_WIDTH, LRU_WIDTH), CONV_WIDTH ** -0.5),
        'conv_b': nrm(ks[14], (DEPTH, LRU_WIDTH), 0.02),
        'lru_w_r': nrm(ks[15], (DEPTH, 2, LRU_BLOCKS, LRU_BLOCK, LRU_BLOCK), LRU_BLOCK ** -0.5),
        'lru_b_r': nrm(ks[16], (DEPTH, 2, LRU_WIDTH), 0.02),
        'lru_w_i': nrm(ks[17], (DEPTH, 2, LRU_BLOCKS, LRU_BLOCK, LRU_BLOCK), LRU_BLOCK ** -0.5),
        'lru_b_i': nrm(ks[18], (DEPTH, 2, LRU_WIDTH), 0.02),
        'lru_lambda': jnp.log(a0) - jnp.log1p(-a0),
        'swa_sink': nrm(ks[20], (DEPTH, SWA_HEADS), 0.5),
        'diff_lambda': nrm(ks[21], (DEPTH, 4, HEAD_DIM), 0.1),
        'diff_subln_g': 1.0 + nrm(ks[22], (DEPTH, DIFF_V_DIM), 0.02),
        'final_g': 1.0 + nrm(ks[23], (D,), 0.02),
    }


def reference(x, c, ctx, c_ctx, w_ada, b_ada, norm_g, ffn1_w_gu, ffn1_w_down, ffn2_w_gu, ffn2_w_down,
              w_in, w_out, conv_w, conv_b, lru_w_r, lru_b_r, lru_w_i, lru_b_i, lru_lambda,
              swa_sink, diff_lambda, diff_subln_g, final_g):
    B = x.shape[0]
    h, hc = x, ctx
    s_c = jax.nn.silu(c)
    s_cc = jax.nn.silu(c_ctx)
    for l in range(DEPTH):
        last = l == DEPTH - 1
        mod = (s_c @ w_ada[l] + b_ada[l]).reshape(B, N_MOD, D_MODEL)
        mod_c = (s_cc @ w_ada[l] + b_ada[l]).reshape(1, N_MOD, D_MODEL)
        lambda_init = 0.8 - 0.6 * math.exp(-0.3 * l)
        h = ffn_sublayer(h, norm_g[l, 0], mod, 0, ffn1_w_gu[l], ffn1_w_down[l])
        hc = ffn_sublayer(hc, norm_g[l, 0], mod_c, 0, ffn1_w_gu[l], ffn1_w_down[l])
        y, yc = token_mixer(modulated_norm(h, norm_g[l, 1], mod, 1), modulated_norm(hc, norm_g[l, 1], mod_c, 1),
                            w_in[l], w_out[l], conv_w[l], conv_b[l], lru_w_r[l], lru_b_r[l], lru_w_i[l], lru_b_i[l],
                            lru_lambda[l], swa_sink[l], diff_lambda[l], diff_subln_g[l], lambda_init, not last)
        h = h + gate_of(mod, 1) * y
        h = ffn_sublayer(h, norm_g[l, 2], mod, 2, ffn2_w_gu[l], ffn2_w_down[l])
        if not last:
            hc = hc + gate_of(mod_c, 1) * yc
            hc = ffn_sublayer(hc, norm_g[l, 2], mod_c, 2, ffn2_w_gu[l], ffn2_w_down[l])
    return rmsnorm(h, final_g)
```

```python
import functools
import math

import numpy as np
import jax
import jax.numpy as jnp
from jax import lax
from jax.experimental import pallas as pl
from jax.experimental.pallas import tpu as pltpu

F32 = jnp.float32
BF16 = jnp.bfloat16

GRID_W = 64
N_MOD = 9
EPS = 1e-6
FFN_RES = 0.5
HEAD_DIM = 64
ROPE_BASE = 10000.0
LRU_WIDTH = 256
LRU_BLOCKS = 4
CONV_WIDTH = 4
CONV_LEFT = 2
LRU_C = 8.0
SWA_HEADS = 4
SWA_KV_HEADS = 2
WINDOW = 128
DIFF_HEADS = 4
NEG_INF = -1e30

LANES = 128
SUBLANES = 8
VMEM_LIMIT_BYTES = 56 * 1024 * 1024

TOKEN_TILE = 512
FFN_CHUNKS = 2
ADA_COLS = 1024
ATT_TQ = 256
LRU_CHUNK = 256

QKV_WIDTH = 2048
COL_SQ, COL_SK, COL_SV, COL_DQ, COL_DK, COL_DV = 0, 256, 384, 512, 1024, 1536
SWA_HEAD_ORDER = (0, 2, 1, 3)


def _cparams(semantics):
    return pltpu.CompilerParams(dimension_semantics=semantics, vmem_limit_bytes=VMEM_LIMIT_BYTES)


def _resident(shape):
    nd = len(shape)
    return pl.BlockSpec(shape, lambda *_: (0,) * nd, pipeline_mode=pl.Buffered(1))


def _dot(a, b):
    return jnp.dot(a, b, preferred_element_type=F32)


def _dot_nt(a, b):
    return lax.dot_general(a, b, (((1,), (1,)), ((), ())), preferred_element_type=F32)


def _rms(x, g):
    return x * lax.rsqrt(jnp.mean(x * x, axis=-1, keepdims=True) + EPS) * g


def _modnorm(h, g, mod_ref, k):
    shift = mod_ref[0, 3 * k:3 * k + 1, :]
    scale = mod_ref[0, 3 * k + 1:3 * k + 2, :]
    return _rms(h, g) * (1.0 + scale) + shift


def _expm1_of_square(a, t):
    u = a * a
    near_one = u == 1.0
    stable = (u - 1.0) * t / jnp.where(near_one, 1.0, jnp.log(u))
    return jnp.where(u < 0.5, u - 1.0, jnp.where(near_one, t, stable))


def _ada_kernel(c_ref, w_ref, b_ref, o_ref):
    c = c_ref[...]
    s = (c * jax.nn.sigmoid(c)).astype(BF16)
    o_ref[0] = _dot(s, w_ref[0].astype(BF16)) + b_ref[0]


def _ada(cond, w_ada, b_ada):
    L, D, N = w_ada.shape
    R = cond.shape[0]
    return pl.pallas_call(
        _ada_kernel,
        grid=(L, N // ADA_COLS),
        in_specs=[
            pl.BlockSpec((R, D), lambda l, n: (0, 0)),
            pl.BlockSpec((1, D, ADA_COLS), lambda l, n: (l, 0, n)),
            pl.BlockSpec((1, 1, ADA_COLS), lambda l, n: (l, 0, n)),
        ],
        out_specs=pl.BlockSpec((1, R, ADA_COLS), lambda l, n: (l, 0, n)),
        out_shape=jax.ShapeDtypeStruct((L, R, N), F32),
        compiler_params=_cparams(("arbitrary", "arbitrary")),
        name="ada",
    )(cond, w_ada, b_ada.reshape(L, 1, N))


def _mod_index(n_lat_tiles, tiles_per_batch, n_batch):
    def idx(i):
        return (jnp.where(i < n_lat_tiles, i // tiles_per_batch, n_batch), 0, 0)
    return idx


def _ffn_kernel(h_ref, mod_ref, g_ref, wgu_ref, wd_ref, fg_ref, o_ref, *, k, d_ff, final_norm):
    h = h_ref[...]
    xn = _modnorm(h, g_ref[...], mod_ref, k).astype(BF16)
    tf = d_ff // FFN_CHUNKS
    acc = None
    for c in range(FFN_CHUNKS):
        lo = c * tf
        g = _dot(xn, wgu_ref[:, lo:lo + tf])
        u = _dot(xn, wgu_ref[:, d_ff + lo:d_ff + lo + tf])
        a = (g * jax.nn.sigmoid(g) * u).astype(BF16)
        part = _dot(a, wd_ref[lo:lo + tf, :])
        acc = part if acc is None else acc + part
    gate = mod_ref[0, 3 * k + 2:3 * k + 3, :]
    out = h + (FFN_RES * gate) * acc
    if final_norm:
        out = _rms(out, fg_ref[...])
    o_ref[...] = out


def _ffn(h, mod, g, w_gu, w_down, final_g, *, k, n_tiles, geom, final_norm=False):
    T, D = h.shape
    d_ff = w_down.shape[0]
    tm = TOKEN_TILE
    return pl.pallas_call(
        functools.partial(_ffn_kernel, k=k, d_ff=d_ff, final_norm=final_norm),
        grid=(n_tiles,),
        in_specs=[
            pl.BlockSpec((tm, D), lambda i: (i, 0)),
            pl.BlockSpec((1, N_MOD, D), _mod_index(*geom)),
            _resident((1, D)),
            _resident((D, 2 * d_ff)),
            _resident((d_ff, D)),
            _resident((1, D)),
        ],
        out_specs=pl.BlockSpec((tm, D), lambda i: (i, 0)),
        out_shape=jax.ShapeDtypeStruct((n_tiles * tm, D), F32),
        compiler_params=_cparams(("arbitrary",)),
        name="ffn",
    )(h, mod, g, w_gu, w_down, final_g)


def _rope(x, cos, sin):
    lane = lax.broadcasted_iota(jnp.int32, x.shape, 1)
    first = (lane & (HEAD_DIM - 1)) < (HEAD_DIM // 2)
    partner = jnp.where(first, pltpu.roll(x, LANES - HEAD_DIM // 2, 1), pltpu.roll(x, HEAD_DIM // 2, 1))
    return x * cos + partner * sin


def _inproj_kernel(h_ref, mod_ref, g_ref, w_ref, cos_ref, sin_ref, lru_ref, qkv_ref):
    xn = _modnorm(h_ref[...], g_ref[...], mod_ref, 1).astype(BF16)
    y = _dot(xn, w_ref[...])
    lw = 2 * LRU_WIDTH
    lru_ref[...] = y[:, :lw]
    cos = cos_ref[...]
    sin = sin_ref[...]
    q_scale = HEAD_DIM ** -0.5
    for c0 in range(0, QKV_WIDTH, LANES):
        blk = y[:, lw + c0:lw + c0 + LANES]
        is_q = (COL_SQ <= c0 < COL_SK) or (COL_DQ <= c0 < COL_DK)
        is_k = (COL_SK <= c0 < COL_SV) or (COL_DK <= c0 < COL_DV)
        if is_q or is_k:
            blk = _rope(blk, cos, sin)
        if is_q:
            blk = blk * q_scale
        qkv_ref[:, c0:c0 + LANES] = blk.astype(BF16)


def _inproj(h, mod, g, w_in, cos_t, sin_t, *, n_tiles, geom):
    T, D = h.shape
    tm = TOKEN_TILE
    n_lat_tiles, tiles_per_batch, _ = geom

    def tab_idx(i):
        return (jnp.where(i < n_lat_tiles, i % tiles_per_batch, tiles_per_batch), 0)

    return pl.pallas_call(
        _inproj_kernel,
        grid=(n_tiles,),
        in_specs=[
            pl.BlockSpec((tm, D), lambda i: (i, 0)),
            pl.BlockSpec((1, N_MOD, D), _mod_index(*geom)),
            _resident((1, D)),
            _resident(w_in.shape),
            pl.BlockSpec((tm, LANES), tab_idx),
            pl.BlockSpec((tm, LANES), tab_idx),
        ],
        out_specs=[
            pl.BlockSpec((tm, 2 * LRU_WIDTH), lambda i: (i, 0)),
            pl.BlockSpec((tm, QKV_WIDTH), lambda i: (i, 0)),
        ],
        out_shape=[
            jax.ShapeDtypeStruct((T, 2 * LRU_WIDTH), F32),
            jax.ShapeDtypeStruct((T, QKV_WIDTH), BF16),
        ],
        compiler_params=_cparams(("arbitrary",)),
        name="inproj",
    )(h, mod, g, w_in, cos_t, sin_t)


def _outproj_kernel(h_ref, mod_ref, ylat_ref, yctx_ref, yswa_ref, ydiff_ref, w_ref, o_ref, *, n_lat_tiles, has_ctx):
    ylru = ylat_ref[...]
    if has_ctx:
        ylru = jnp.where(pl.program_id(0) < n_lat_tiles, ylru, yctx_ref[...])
    y = jnp.concatenate([ylru, yswa_ref[...], ydiff_ref[...]], axis=1)
    gate = mod_ref[0, 5:6, :]
    o_ref[...] = h_ref[...] + gate * _dot(y, w_ref[...])


def _outproj(h, mod, y_lat, y_ctx, y_swa, y_diff, w_out, *, n_tiles, geom):
    T, D = h.shape
    tm = TOKEN_TILE
    n_lat_tiles = geom[0]
    has_ctx = n_tiles > n_lat_tiles
    return pl.pallas_call(
        functools.partial(_outproj_kernel, n_lat_tiles=n_lat_tiles, has_ctx=has_ctx),
        grid=(n_tiles,),
        in_specs=[
            pl.BlockSpec((tm, D), lambda i: (i, 0)),
            pl.BlockSpec((1, N_MOD, D), _mod_index(*geom)),
            pl.BlockSpec((tm, LRU_WIDTH), lambda i: (jnp.minimum(i, n_lat_tiles - 1), 0)),
            pl.BlockSpec((tm, LRU_WIDTH), lambda i: (jnp.maximum(i - n_lat_tiles, 0), 0)),
            pl.BlockSpec((tm, SWA_HEADS * HEAD_DIM), lambda i: (i, 0)),
            pl.BlockSpec((tm, DIFF_HEADS * 2 * HEAD_DIM), lambda i: (i, 0)),
            _resident(w_out.shape),
        ],
        out_specs=pl.BlockSpec((tm, D), lambda i: (i, 0)),
        out_shape=jax.ShapeDtypeStruct((n_tiles * tm, D), F32),
        compiler_params=_cparams(("arbitrary",)),
        name="outproj",
    )(h, mod, y_lat, y_ctx, y_swa, y_diff, w_out)


def _lru_kernel(lat_ref, ctx_ref, cw_ref, cb_ref, wr_ref, br_ref, wi_ref, bi_ref, lam_ref,
                ylat_ref, yctx_ref, xpad, u_scr, hf_scr, *, S, Lc):
    W = LRU_WIDTH
    TC = LRU_CHUNK
    PAD = SUBLANES
    row = lax.broadcasted_iota(jnp.int32, (TC, W), 0)

    def conv_chunk(start):
        a = xpad[pl.ds(start, TC + 2 * PAD), :]
        u = cb_ref[...] + jnp.zeros((TC, W), F32)
        for k in range(CONV_WIDTH):
            sh = (CONV_LEFT - k) % (TC + 2 * PAD)
            r = a if sh == 0 else pltpu.roll(a, sh, 0)
            u = u + r[PAD:PAD + TC] * cw_ref[k:k + 1, :]
        return u

    def scan_chunk(u, d, carry, reverse):
        r = jax.nn.sigmoid(_dot(u, wr_ref[d]) + br_ref[d])
        i = jax.nn.sigmoid(_dot(u, wi_ref[d]) + bi_ref[d])
        log_a = (-LRU_C) * r * jax.nn.softplus(-lam_ref[d])
        A = jnp.exp(log_a)
        Bv = jnp.sqrt(-_expm1_of_square(A, 2.0 * log_a)) * i * u
        s = 1
        while s < TC:
            if reverse:
                keep = row < TC - s
                sh = TC - s
            else:
                keep = row >= s
                sh = s
            a_sh = jnp.where(keep, pltpu.roll(A, sh, 0), 1.0)
            b_sh = jnp.where(keep, pltpu.roll(Bv, sh, 0), 0.0)
            Bv = A * b_sh + Bv
            A = A * a_sh
            s *= 2
        h = A * carry + Bv
        return h, (h[0:1] if reverse else h[TC - 1:TC])

    def gelu(x):
        return jax.nn.gelu(x)

    zero_pad = jnp.zeros((PAD, W), F32)
    zero_state = jnp.zeros((1, W), F32)

    xpad[0:PAD, :] = zero_pad
    xpad[PAD:PAD + Lc, :] = ctx_ref[:, 0:W]
    xpad[PAD + Lc:2 * PAD + Lc, :] = zero_pad
    uc = conv_chunk(0)
    hcf, carry_f = scan_chunk(uc, 0, zero_state, False)
    hcb, carry_b = scan_chunk(uc, 1, zero_state, True)
    yctx_ref[...] = ((hcf + hcb) * gelu(ctx_ref[:, W:2 * W])).astype(BF16)

    xpad[PAD:PAD + S, :] = lat_ref[:, 0:W]
    xpad[PAD + S:2 * PAD + S, :] = zero_pad
    nc = S // TC

    def conv_body(c, _):
        st = pl.multiple_of(c * TC, TC)
        u_scr[pl.ds(st, TC), :] = conv_chunk(st)
        return 0

    lax.fori_loop(0, nc, conv_body, 0)

    def fwd_body(c, carry):
        st = pl.multiple_of(c * TC, TC)
        h, carry = scan_chunk(u_scr[pl.ds(st, TC), :], 0, carry, False)
        hf_scr[pl.ds(st, TC), :] = h
        return carry

    lax.fori_loop(0, nc, fwd_body, carry_f)

    def bwd_body(c, carry):
        st = pl.multiple_of((nc - 1 - c) * TC, TC)
        h, carry = scan_chunk(u_scr[pl.ds(st, TC), :], 1, carry, True)
        y = (hf_scr[pl.ds(st, TC), :] + h) * gelu(lat_ref[pl.ds(st, TC), W:2 * W])
        ylat_ref[pl.ds(st, TC), :] = y.astype(BF16)
        return carry

    lax.fori_loop(0, nc, bwd_body, carry_b)


def _lru(lru_in, conv_w, conv_b, w_r, b_r, w_i, b_i, lam, *, B, S, Lc):
    W = LRU_WIDTH
    t_lat = B * S
    ctx_blk0 = t_lat // Lc
    return pl.pallas_call(
        functools.partial(_lru_kernel, S=S, Lc=Lc),
        grid=(B,),
        in_specs=[
            pl.BlockSpec((S, 2 * W), lambda b: (b, 0)),
            pl.BlockSpec((Lc, 2 * W), lambda b: (ctx_blk0 + b, 0)),
            _resident(conv_w.shape),
            _resident(conv_b.shape),
            _resident(w_r.shape),
            _resident(b_r.shape),
            _resident(w_i.shape),
            _resident(b_i.shape),
            _resident(lam.shape),
        ],
        out_specs=[
            pl.BlockSpec((S, W), lambda b: (b, 0)),
            pl.BlockSpec((Lc, W), lambda b: (b, 0)),
        ],
        out_shape=[
            jax.ShapeDtypeStruct((t_lat, W), BF16),
            jax.ShapeDtypeStruct((B * Lc, W), BF16),
        ],
        scratch_shapes=[
            pltpu.VMEM((S + 2 * SUBLANES, W), F32),
            pltpu.VMEM((S, W), F32),
            pltpu.VMEM((S, W), F32),
        ],
        compiler_params=_cparams(("arbitrary",)),
        name="lru",
    )(lru_in, lru_in, conv_w, conv_b, w_r, b_r, w_i, b_i, lam)


def _swa_kernel(sink_ref, q_ref, k_ref, v_ref, kc_ref, vc_ref, o_ref, *, nq, S):
    TQ = ATT_TQ
    KB = TQ + 2 * WINDOW
    j = pl.program_id(1)
    q = q_ref[...]
    kc = kc_ref[...]
    vc = vc_ref[...]
    lane = lax.broadcasted_iota(jnp.int32, (TQ, LANES), 1)
    low = lane < HEAD_DIM

    def run(band):
        groups = []
        for grp in range(2):
            qg = q[:, grp * LANES:(grp + 1) * LANES]
            outs = []
            for half in range(2):
                head = SWA_HEAD_ORDER[2 * grp + half]
                sink = sink_ref[0, head]
                qh = jnp.where(low if half == 0 else ~low, qg, jnp.zeros_like(qg))
                s_c = _dot_nt(qh, kc)
                m = jnp.maximum(jnp.max(s_c, axis=-1, keepdims=True), sink)
                if band is not None:
                    kb, vb, valid = band
                    s_l = jnp.where(valid, _dot_nt(qh, kb), NEG_INF)
                    m = jnp.maximum(m, jnp.max(s_l, axis=-1, keepdims=True))
                p_c = jnp.exp(s_c - m)
                den = jnp.sum(p_c, axis=-1, keepdims=True) + jnp.exp(sink - m)
                o = _dot(p_c.astype(BF16), vc)
                if band is not None:
                    p_l = jnp.exp(s_l - m)
                    den = den + jnp.sum(p_l, axis=-1, keepdims=True)
                    o = o + _dot(p_l.astype(BF16), vb)
                outs.append(o / den)
            groups.append(jnp.where(low, outs[0], outs[1]))
        o_ref[...] = jnp.concatenate(groups, axis=1).astype(BF16)

    @pl.when(j < nq)
    def _():
        q0 = j * TQ
        start = pl.multiple_of(jnp.clip(q0 - WINDOW, 0, S - KB), WINDOW)
        kb = k_ref[pl.ds(start, KB), :]
        vb = v_ref[pl.ds(start, KB), :]
        q_abs = q0 + lax.broadcasted_iota(jnp.int32, (TQ, KB), 0)
        k_abs = start + lax.broadcasted_iota(jnp.int32, (TQ, KB), 1)
        valid = jnp.abs(q_abs - k_abs) <= WINDOW
        run((kb, vb, valid))

    @pl.when(j == nq)
    def _():
        run(None)


def _swa(qkv, sink, *, B, S, Lc):
    T = qkv.shape[0]
    TQ = ATT_TQ
    nq = S // TQ
    ctx_blk0 = B * S // Lc
    qw = SWA_HEADS * HEAD_DIM

    def q_idx(b, j):
        return (jnp.where(j < nq, b * nq + j, ctx_blk0 + b), COL_SQ // qw)

    return pl.pallas_call(
        functools.partial(_swa_kernel, nq=nq, S=S),
        grid=(B, nq + 1),
        in_specs=[
            pl.BlockSpec(memory_space=pltpu.SMEM),
            pl.BlockSpec((TQ, qw), q_idx),
            pl.BlockSpec((S, LANES), lambda b, j: (b, COL_SK // LANES)),
            pl.BlockSpec((S, LANES), lambda b, j: (b, COL_SV // LANES)),
            pl.BlockSpec((Lc, LANES), lambda b, j: (ctx_blk0 + b, COL_SK // LANES)),
            pl.BlockSpec((Lc, LANES), lambda b, j: (ctx_blk0 + b, COL_SV // LANES)),
        ],
        out_specs=pl.BlockSpec((TQ, qw), lambda b, j: (jnp.where(j < nq, b * nq + j, ctx_blk0 + b), 0)),
        out_shape=jax.ShapeDtypeStruct((T, qw), BF16),
        compiler_params=_cparams(("arbitrary", "arbitrary")),
        name="swa",
    )(sink, qkv, qkv, qkv, qkv, qkv)


def _diff_kernel(dl_ref, g_ref, q_ref, k_ref, v_ref, kc_ref, vc_ref, o_ref, *, nq, lambda_init):
    TQ = ATT_TQ
    j = pl.program_id(2)
    dl = dl_ref[...]
    lam = (jnp.exp(jnp.sum(dl[0:1] * dl[1:2], axis=-1, keepdims=True))
           - jnp.exp(jnp.sum(dl[2:3] * dl[3:4], axis=-1, keepdims=True)) + lambda_init)
    q = q_ref[...]
    lane = lax.broadcasted_iota(jnp.int32, (TQ, LANES), 1)
    low = lane < HEAD_DIM

    def run(with_latent):
        outs = []
        for half in range(2):
            qh = jnp.where(low if half == 0 else ~low, q, jnp.zeros_like(q))
            s_c = _dot_nt(qh, kc_ref[...])
            m = jnp.max(s_c, axis=-1, keepdims=True)
            if with_latent:
                s_l = _dot_nt(qh, k_ref[...])
                m = jnp.maximum(m, jnp.max(s_l, axis=-1, keepdims=True))
            e_c = jnp.exp(s_c - m)
            den = jnp.sum(e_c, axis=-1, keepdims=True)
            o = _dot(e_c.astype(BF16), vc_ref[...])
            if with_latent:
                e_l = jnp.exp(s_l - m)
                den = den + jnp.sum(e_l, axis=-1, keepdims=True)
                o = o + _dot(e_l.astype(BF16), v_ref[...])
            outs.append(o / den)
        o = outs[0] - lam * outs[1]
        o_ref[...] = (_rms(o, g_ref[...]) * (1.0 - lambda_init)).astype(BF16)

    @pl.when(j < nq)
    def _():
        run(True)

    @pl.when(j == nq)
    def _():
        run(False)


def _diff(qkv, diff_lambda, subln_g, *, B, S, Lc, lambda_init):
    T = qkv.shape[0]
    TQ = ATT_TQ
    nq = S // TQ
    ctx_blk0 = B * S // Lc
    H = DIFF_HEADS

    def q_row(b, j):
        return jnp.where(j < nq, b * nq + j, ctx_blk0 + b)

    return pl.pallas_call(
        functools.partial(_diff_kernel, nq=nq, lambda_init=lambda_init),
        grid=(B, H, nq + 1),
        in_specs=[
            _resident(diff_lambda.shape),
            _resident(subln_g.shape),
            pl.BlockSpec((TQ, LANES), lambda b, h, j: (q_row(b, j), COL_DQ // LANES + h)),
            pl.BlockSpec((S, LANES), lambda b, h, j: (b, COL_DK // LANES + h)),
            pl.BlockSpec((S, LANES), lambda b, h, j: (b, COL_DV // LANES + h)),
            pl.BlockSpec((Lc, LANES), lambda b, h, j: (ctx_blk0 + b, COL_DK // LANES + h)),
            pl.BlockSpec((Lc, LANES), lambda b, h, j: (ctx_blk0 + b, COL_DV // LANES + h)),
        ],
        out_specs=pl.BlockSpec((TQ, LANES), lambda b, h, j: (q_row(b, j), h)),
        out_shape=jax.ShapeDtypeStruct((T, H * LANES), BF16),
        compiler_params=_cparams(("arbitrary", "arbitrary", "arbitrary")),
        name="diff",
    )(diff_lambda, subln_g, qkv, qkv, qkv, qkv, qkv)


def _rope_tables(S, tm):
    rows = S // GRID_W
    row = jnp.repeat(jnp.arange(rows, dtype=F32), GRID_W)
    col = jnp.tile(jnp.arange(GRID_W, dtype=F32), rows)
    n_freq = HEAD_DIM // 4
    inv_freq = ROPE_BASE ** (-jnp.arange(n_freq, dtype=F32) / n_freq)
    ang = jnp.concatenate([row[:, None] * inv_freq, col[:, None] * inv_freq], axis=-1)
    cos, sin = jnp.cos(ang), jnp.sin(ang)
    reps = LANES // (HEAD_DIM // 2)
    sign = np.tile(np.concatenate([-np.ones(HEAD_DIM // 2), np.ones(HEAD_DIM // 2)]), LANES // HEAD_DIM)
    cos_t = jnp.concatenate([jnp.tile(cos, (1, reps)), jnp.ones((tm, LANES), F32)], axis=0)
    sin_t = jnp.concatenate([jnp.tile(sin, (1, reps)) * sign.astype(np.float32), jnp.zeros((tm, LANES), F32)], axis=0)
    return cos_t, sin_t


def _block_diag(w):
    nd, K, c, _ = w.shape
    eye = jnp.eye(K, dtype=w.dtype)
    return jnp.einsum('dkij,kl->dkilj', w, eye).reshape(nd, K * c, K * c)


def kernel(x, c, ctx, c_ctx, w_ada, b_ada, norm_g, ffn1_w_gu, ffn1_w_down, ffn2_w_gu, ffn2_w_down,
           w_in, w_out, conv_w, conv_b, lru_w_r, lru_b_r, lru_w_i, lru_b_i, lru_lambda,
           swa_sink, diff_lambda, diff_subln_g, final_g):
    B, S, D = x.shape
    Lc = ctx.shape[1]
    depth = w_ada.shape[0]
    tm = TOKEN_TILE
    assert S % tm == 0 and (B * Lc) % tm == 0 and Lc == ATT_TQ and S % GRID_W == 0 and S % LRU_CHUNK == 0
    assert Lc == LRU_CHUNK and S >= ATT_TQ + 2 * WINDOW
    n_lat_tiles = B * S // tm
    n_tiles = n_lat_tiles + B * Lc // tm
    geom = (n_lat_tiles, S // tm, B)

    h = jnp.concatenate([x.reshape(B * S, D), ctx.reshape(B * Lc, D)], axis=0)

    n_cond = B + 1
    pad = (-n_cond) % SUBLANES
    cond = jnp.concatenate([c, c_ctx[None], jnp.zeros((pad, D), F32)], axis=0)
    mod_all = _ada(cond, w_ada, b_ada).reshape(depth, n_cond + pad, N_MOD, D)

    cos_t, sin_t = _rope_tables(S, tm)
    q_perm = np.arange(w_in.shape[-1])
    sq0 = 2 * LRU_WIDTH
    q_perm[sq0:sq0 + SWA_HEADS * HEAD_DIM] = sq0 + np.concatenate(
        [np.arange(hd * HEAD_DIM, (hd + 1) * HEAD_DIM) for hd in SWA_HEAD_ORDER])
    o_perm = np.arange(w_out.shape[1])
    o_perm[LRU_WIDTH:LRU_WIDTH + SWA_HEADS * HEAD_DIM] = LRU_WIDTH + np.concatenate(
        [np.arange(hd * HEAD_DIM, (hd + 1) * HEAD_DIM) for hd in SWA_HEAD_ORDER])

    for l in range(depth):
        last = l == depth - 1
        mod = mod_all[l]
        lambda_init = 0.8 - 0.6 * math.exp(-0.3 * l)
        g = norm_g[l]
        h = _ffn(h, mod, g[0:1], ffn1_w_gu[l].astype(BF16), ffn1_w_down[l].astype(BF16), final_g[None],
                 k=0, n_tiles=n_tiles, geom=geom)
        lru_in, qkv = _inproj(h, mod, g[1:2], w_in[l][:, q_perm].astype(BF16), cos_t, sin_t,
                              n_tiles=n_tiles, geom=geom)
        y_lat, y_ctx = _lru(lru_in, conv_w[l], conv_b[l][None], _block_diag(lru_w_r[l]), lru_b_r[l][:, None],
                            _block_diag(lru_w_i[l]), lru_b_i[l][:, None], lru_lambda[l][:, None], B=B, S=S, Lc=Lc)
        y_swa = _swa(qkv, swa_sink[l][None], B=B, S=S, Lc=Lc)
        y_diff = _diff(qkv, diff_lambda[l], diff_subln_g[l][None], B=B, S=S, Lc=Lc, lambda_init=lambda_init)
        n_out = n_lat_tiles if last else n_tiles
        h = _outproj(h, mod, y_lat, y_ctx, y_swa, y_diff, w_out[l][o_perm, :].astype(BF16),
                     n_tiles=n_out, geom=geom)
        h = _ffn(h, mod, g[2:3], ffn2_w_gu[l].astype(BF16), ffn2_w_down[l].astype(BF16), final_g[None],
                 k=2, n_tiles=n_out, geom=geom, final_norm=last)
    return h[:B * S].reshape(B, S, D)
```

```python
import functools
import math

import numpy as np
import jax
import jax.numpy as jnp
from jax import lax
from jax.experimental import pallas as pl
from jax.experimental.pallas import tpu as pltpu

F32 = jnp.float32
BF16 = jnp.bfloat16

GRID_W = 64
N_MOD = 9
EPS = 1e-6
FFN_RES = 0.5
HEAD_DIM = 64
ROPE_BASE = 10000.0
LRU_WIDTH = 256
LRU_BLOCKS = 4
CONV_WIDTH = 4
CONV_LEFT = 2
LRU_C = 8.0
SWA_HEADS = 4
SWA_KV_HEADS = 2
WINDOW = 128
DIFF_HEADS = 4
NEG_INF = -1e30

LANES = 128
SUBLANES = 8
VMEM_LIMIT_BYTES = 56 * 1024 * 1024

TOKEN_TILE = 512
FFN_CHUNKS = 2
ADA_COLS = 1024
ATT_TQ = 256
LRU_CHUNK = 256

QKV_WIDTH = 1536
DV_WIDTH = 512
COL_SQ, COL_SK, COL_SV, COL_DQ, COL_DK = 0, 256, 384, 512, 1024
LOG2E = math.log2(math.e)
DIFF_ROWS = 64
ONES_ROWS = 16
SWA_HEAD_ORDER = (0, 2, 1, 3)


def _cparams(semantics):
    return pltpu.CompilerParams(dimension_semantics=semantics, vmem_limit_bytes=VMEM_LIMIT_BYTES)


def _resident(shape):
    nd = len(shape)
    return pl.BlockSpec(shape, lambda *_: (0,) * nd, pipeline_mode=pl.Buffered(1))


def _dot(a, b):
    return jnp.dot(a, b, preferred_element_type=F32)


def _dot_nt(a, b):
    return lax.dot_general(a, b, (((1,), (1,)), ((), ())), preferred_element_type=F32)


def _rms(x, g):
    return x * lax.rsqrt(jnp.mean(x * x, axis=-1, keepdims=True) + EPS) * g


def _modnorm(h, g, mod_ref, k):
    shift = mod_ref[0, 3 * k:3 * k + 1, :]
    scale = mod_ref[0, 3 * k + 1:3 * k + 2, :]
    return _rms(h, g) * (1.0 + scale) + shift


def _expm1_of_square(a, t):
    u = a * a
    near_one = u == 1.0
    stable = (u - 1.0) * t / jnp.where(near_one, 1.0, jnp.log(u))
    return jnp.where(u < 0.5, u - 1.0, jnp.where(near_one, t, stable))


def _ada_kernel(c_ref, w_ref, b_ref, o_ref):
    c = c_ref[...]
    s = (c * jax.nn.sigmoid(c)).astype(BF16)
    o_ref[0] = _dot(s, w_ref[0].astype(BF16)) + b_ref[0]


def _ada(cond, w_ada, b_ada):
    L, D, N = w_ada.shape
    R = cond.shape[0]
    return pl.pallas_call(
        _ada_kernel,
        grid=(L, N // ADA_COLS),
        in_specs=[
            pl.BlockSpec((R, D), lambda l, n: (0, 0)),
            pl.BlockSpec((1, D, ADA_COLS), lambda l, n: (l, 0, n)),
            pl.BlockSpec((1, 1, ADA_COLS), lambda l, n: (l, 0, n)),
        ],
        out_specs=pl.BlockSpec((1, R, ADA_COLS), lambda l, n: (l, 0, n)),
        out_shape=jax.ShapeDtypeStruct((L, R, N), F32),
        compiler_params=_cparams(("arbitrary", "arbitrary")),
        name="ada",
    )(cond, w_ada, b_ada.reshape(L, 1, N))


def _mod_index(n_lat_tiles, tiles_per_batch, n_batch):
    def idx(i):
        return (jnp.where(i < n_lat_tiles, i // tiles_per_batch, n_batch), 0, 0)
    return idx


def _ffn_kernel(h_ref, mod_ref, g_ref, wgu_ref, wd_ref, fg_ref, o_ref, *, k, d_ff, final_norm):
    h = h_ref[...]
    xn = _modnorm(h, g_ref[...], mod_ref, k).astype(BF16)
    tf = d_ff // FFN_CHUNKS
    acc = None
    for c in range(FFN_CHUNKS):
        lo = c * tf
        g = _dot(xn, wgu_ref[:, lo:lo + tf])
        u = _dot(xn, wgu_ref[:, d_ff + lo:d_ff + lo + tf])
        a = (g * jax.nn.sigmoid(g) * u).astype(BF16)
        part = _dot(a, wd_ref[lo:lo + tf, :])
        acc = part if acc is None else acc + part
    gate = mod_ref[0, 3 * k + 2:3 * k + 3, :]
    out = h + (FFN_RES * gate) * acc
    if final_norm:
        out = _rms(out, fg_ref[...])
    o_ref[...] = out


def _ffn(h, mod, g, w_gu, w_down, final_g, *, k, n_tiles, geom, final_norm=False):
    T, D = h.shape
    d_ff = w_down.shape[0]
    tm = TOKEN_TILE
    return pl.pallas_call(
        functools.partial(_ffn_kernel, k=k, d_ff=d_ff, final_norm=final_norm),
        grid=(n_tiles,),
        in_specs=[
            pl.BlockSpec((tm, D), lambda i: (i, 0)),
            pl.BlockSpec((1, N_MOD, D), _mod_index(*geom)),
            _resident((1, D)),
            _resident((D, 2 * d_ff)),
            _resident((d_ff, D)),
            _resident((1, D)),
        ],
        out_specs=pl.BlockSpec((tm, D), lambda i: (i, 0)),
        out_shape=jax.ShapeDtypeStruct((n_tiles * tm, D), F32),
        compiler_params=_cparams(("arbitrary",)),
        name="ffn",
    )(h, mod, g, w_gu, w_down, final_g)


def _rope(x, cos, sin):
    lane = lax.broadcasted_iota(jnp.int32, x.shape, 1)
    first = (lane & (HEAD_DIM - 1)) < (HEAD_DIM // 2)
    partner = jnp.where(first, pltpu.roll(x, LANES - HEAD_DIM // 2, 1), pltpu.roll(x, HEAD_DIM // 2, 1))
    return x * cos + partner * sin


def _inproj_kernel(h_ref, mod_ref, g_ref, w_ref, cos_ref, sin_ref, lru_ref, qkv_ref, dvt_ref):
    xn = _modnorm(h_ref[...], g_ref[...], mod_ref, 1).astype(BF16)
    y = _dot(xn, w_ref[...])
    lw = 2 * LRU_WIDTH
    lru_ref[...] = y[:, :lw]
    cos = cos_ref[...]
    sin = sin_ref[...]
    q_scale = HEAD_DIM ** -0.5
    for c0 in range(0, QKV_WIDTH, LANES):
        blk = y[:, lw + c0:lw + c0 + LANES]
        swa_q = COL_SQ <= c0 < COL_SK
        diff_q = COL_DQ <= c0 < COL_DK
        is_k = (COL_SK <= c0 < COL_SV) or (COL_DK <= c0 < QKV_WIDTH)
        if swa_q or diff_q or is_k:
            blk = _rope(blk, cos, sin)
        if swa_q:
            blk = blk * q_scale
        if diff_q:
            blk = blk * (q_scale * LOG2E)
        qkv_ref[:, c0:c0 + LANES] = blk.astype(BF16)
    dvt_ref[...] = y[:, lw + QKV_WIDTH:lw + QKV_WIDTH + DV_WIDTH].T.astype(BF16)


def _inproj(h, mod, g, w_in, cos_t, sin_t, *, n_tiles, geom):
    T, D = h.shape
    tm = TOKEN_TILE
    n_lat_tiles, tiles_per_batch, _ = geom

    def tab_idx(i):
        return (jnp.where(i < n_lat_tiles, i % tiles_per_batch, tiles_per_batch), 0)

    return pl.pallas_call(
        _inproj_kernel,
        grid=(n_tiles,),
        in_specs=[
            pl.BlockSpec((tm, D), lambda i: (i, 0)),
            pl.BlockSpec((1, N_MOD, D), _mod_index(*geom)),
            _resident((1, D)),
            _resident(w_in.shape),
            pl.BlockSpec((tm, LANES), tab_idx),
            pl.BlockSpec((tm, LANES), tab_idx),
        ],
        out_specs=[
            pl.BlockSpec((tm, 2 * LRU_WIDTH), lambda i: (i, 0)),
            pl.BlockSpec((tm, QKV_WIDTH), lambda i: (i, 0)),
            pl.BlockSpec((DV_WIDTH, tm), lambda i: (0, i)),
        ],
        out_shape=[
            jax.ShapeDtypeStruct((T, 2 * LRU_WIDTH), F32),
            jax.ShapeDtypeStruct((T, QKV_WIDTH), BF16),
            jax.ShapeDtypeStruct((DV_WIDTH, T), BF16),
        ],
        compiler_params=_cparams(("arbitrary",)),
        name="inproj",
    )(h, mod, g, w_in, cos_t, sin_t)


def _outproj_kernel(h_ref, mod_ref, lru_lat, lru_ctx, yswa_ref, diff_lat, diff_ctx, w_ref, o_ref,
                    *, n_lat_tiles, has_ctx):
    def emit(lru_ref, diff_ref):
        y = jnp.concatenate([lru_ref[...], yswa_ref[...], diff_ref[...]], axis=1)
        gate = mod_ref[0, 5:6, :]
        o_ref[...] = h_ref[...] + gate * _dot(y, w_ref[...])

    if not has_ctx:
        emit(lru_lat, diff_lat)
        return

    @pl.when(pl.program_id(0) < n_lat_tiles)
    def _():
        emit(lru_lat, diff_lat)

    @pl.when(pl.program_id(0) >= n_lat_tiles)
    def _():
        emit(lru_ctx, diff_ctx)


def _outproj(h, mod, lru_lat, lru_ctx, y_swa, diff_lat, diff_ctx, w_out, *, n_tiles, geom):
    T, D = h.shape
    tm = TOKEN_TILE
    n_lat_tiles = geom[0]
    has_ctx = n_tiles > n_lat_tiles
    dw = DIFF_HEADS * 2 * HEAD_DIM

    def lat_idx(i):
        return (jnp.minimum(i, n_lat_tiles - 1), 0)

    def ctx_idx(i):
        return (jnp.maximum(i - n_lat_tiles, 0), 0)

    return pl.pallas_call(
        functools.partial(_outproj_kernel, n_lat_tiles=n_lat_tiles, has_ctx=has_ctx),
        grid=(n_tiles,),
        in_specs=[
            pl.BlockSpec((tm, D), lambda i: (i, 0)),
            pl.BlockSpec((1, N_MOD, D), _mod_index(*geom)),
            pl.BlockSpec((tm, LRU_WIDTH), lat_idx),
            pl.BlockSpec((tm, LRU_WIDTH), ctx_idx),
            pl.BlockSpec((tm, SWA_HEADS * HEAD_DIM), lambda i: (i, 0)),
            pl.BlockSpec((tm, dw), lat_idx),
            pl.BlockSpec((tm, dw), ctx_idx),
            _resident(w_out.shape),
        ],
        out_specs=pl.BlockSpec((tm, D), lambda i: (i, 0)),
        out_shape=jax.ShapeDtypeStruct((n_tiles * tm, D), F32),
        compiler_params=_cparams(("arbitrary",)),
        name="outproj",
    )(h, mod, lru_lat, lru_ctx, y_swa, diff_lat, diff_ctx, w_out)


def _lru_kernel(lat_ref, ctx_ref, cw_ref, cb_ref, wr_ref, br_ref, wi_ref, bi_ref, lam_ref,
                ylat_ref, yctx_ref, xpad, u_scr, hf_scr, *, S, Lc):
    W = LRU_WIDTH
    TC = LRU_CHUNK
    PAD = SUBLANES
    row = lax.broadcasted_iota(jnp.int32, (TC, W), 0)

    def conv_chunk(start):
        a = xpad[pl.ds(start, TC + 2 * PAD), :]
        u = cb_ref[...] + jnp.zeros((TC, W), F32)
        for k in range(CONV_WIDTH):
            sh = (CONV_LEFT - k) % (TC + 2 * PAD)
            r = a if sh == 0 else pltpu.roll(a, sh, 0)
            u = u + r[PAD:PAD + TC] * cw_ref[k:k + 1, :]
        return u

    def scan_chunk(u, d, carry, reverse):
        r = jax.nn.sigmoid(_dot(u, wr_ref[d]) + br_ref[d])
        i = jax.nn.sigmoid(_dot(u, wi_ref[d]) + bi_ref[d])
        log_a = (-LRU_C) * r * jax.nn.softplus(-lam_ref[d])
        A = jnp.exp(log_a)
        Bv = jnp.sqrt(-_expm1_of_square(A, 2.0 * log_a)) * i * u
        s = 1
        while s < TC:
            if reverse:
                keep = row < TC - s
                sh = TC - s
            else:
                keep = row >= s
                sh = s
            a_sh = jnp.where(keep, pltpu.roll(A, sh, 0), 1.0)
            b_sh = jnp.where(keep, pltpu.roll(Bv, sh, 0), 0.0)
            Bv = A * b_sh + Bv
            A = A * a_sh
            s *= 2
        h = A * carry + Bv
        return h, (h[0:1] if reverse else h[TC - 1:TC])

    def gelu(x):
        return jax.nn.gelu(x)

    zero_pad = jnp.zeros((PAD, W), F32)
    zero_state = jnp.zeros((1, W), F32)

    xpad[0:PAD, :] = zero_pad
    xpad[PAD:PAD + Lc, :] = ctx_ref[:, 0:W]
    xpad[PAD + Lc:2 * PAD + Lc, :] = zero_pad
    uc = conv_chunk(0)
    hcf, carry_f = scan_chunk(uc, 0, zero_state, False)
    hcb, carry_b = scan_chunk(uc, 1, zero_state, True)
    yctx_ref[...] = ((hcf + hcb) * gelu(ctx_ref[:, W:2 * W])).astype(BF16)

    xpad[PAD:PAD + S, :] = lat_ref[:, 0:W]
    xpad[PAD + S:2 * PAD + S, :] = zero_pad
    nc = S // TC

    def conv_body(c, _):
        st = pl.multiple_of(c * TC, TC)
        u_scr[pl.ds(st, TC), :] = conv_chunk(st)
        return 0

    lax.fori_loop(0, nc, conv_body, 0)

    def fwd_body(c, carry):
        st = pl.multiple_of(c * TC, TC)
        h, carry = scan_chunk(u_scr[pl.ds(st, TC), :], 0, carry, False)
        hf_scr[pl.ds(st, TC), :] = h
        return carry

    lax.fori_loop(0, nc, fwd_body, carry_f)

    def bwd_body(c, carry):
        st = pl.multiple_of((nc - 1 - c) * TC, TC)
        h, carry = scan_chunk(u_scr[pl.ds(st, TC), :], 1, carry, True)
        y = (hf_scr[pl.ds(st, TC), :] + h) * gelu(lat_ref[pl.ds(st, TC), W:2 * W])
        ylat_ref[pl.ds(st, TC), :] = y.astype(BF16)
        return carry

    lax.fori_loop(0, nc, bwd_body, carry_b)


def _lru(lru_in, conv_w, conv_b, w_r, b_r, w_i, b_i, lam, *, B, S, Lc):
    W = LRU_WIDTH
    t_lat = B * S
    ctx_blk0 = t_lat // Lc
    return pl.pallas_call(
        functools.partial(_lru_kernel, S=S, Lc=Lc),
        grid=(B,),
        in_specs=[
            pl.BlockSpec((S, 2 * W), lambda b: (b, 0)),
            pl.BlockSpec((Lc, 2 * W), lambda b: (ctx_blk0 + b, 0)),
            _resident(conv_w.shape),
            _resident(conv_b.shape),
            _resident(w_r.shape),
            _resident(b_r.shape),
            _resident(w_i.shape),
            _resident(b_i.shape),
            _resident(lam.shape),
        ],
        out_specs=[
            pl.BlockSpec((S, W), lambda b: (b, 0)),
            pl.BlockSpec((Lc, W), lambda b: (b, 0)),
        ],
        out_shape=[
            jax.ShapeDtypeStruct((t_lat, W), BF16),
            jax.ShapeDtypeStruct((B * Lc, W), BF16),
        ],
        scratch_shapes=[
            pltpu.VMEM((S + 2 * SUBLANES, W), F32),
            pltpu.VMEM((S, W), F32),
            pltpu.VMEM((S, W), F32),
        ],
        compiler_params=_cparams(("arbitrary",)),
        name="lru",
    )(lru_in, lru_in, conv_w, conv_b, w_r, b_r, w_i, b_i, lam)


def _swa_kernel(sink_ref, q_ref, k_ref, v_ref, kc_ref, vc_ref, o_ref, *, nq, S):
    TQ = ATT_TQ
    KB = TQ + 2 * WINDOW
    j = pl.program_id(1)
    q = q_ref[...]
    kc = kc_ref[...]
    vc = vc_ref[...]
    lane = lax.broadcasted_iota(jnp.int32, (TQ, LANES), 1)
    low = lane < HEAD_DIM

    def run(band):
        groups = []
        for grp in range(2):
            qg = q[:, grp * LANES:(grp + 1) * LANES]
            outs = []
            for half in range(2):
                head = SWA_HEAD_ORDER[2 * grp + half]
                sink = sink_ref[0, head]
                qh = jnp.where(low if half == 0 else ~low, qg, jnp.zeros_like(qg))
                s_c = _dot_nt(qh, kc)
                m = jnp.maximum(jnp.max(s_c, axis=-1, keepdims=True), sink)
                if band is not None:
                    kb, vb, valid = band
                    s_l = jnp.where(valid, _dot_nt(qh, kb), NEG_INF)
                    m = jnp.maximum(m, jnp.max(s_l, axis=-1, keepdims=True))
                p_c = jnp.exp(s_c - m)
                den = jnp.sum(p_c, axis=-1, keepdims=True) + jnp.exp(sink - m)
                o = _dot(p_c.astype(BF16), vc)
                if band is not None:
                    p_l = jnp.exp(s_l - m)
                    den = den + jnp.sum(p_l, axis=-1, keepdims=True)
                    o = o + _dot(p_l.astype(BF16), vb)
                outs.append(o / den)
            groups.append(jnp.where(low, outs[0], outs[1]))
        o_ref[...] = jnp.concatenate(groups, axis=1).astype(BF16)

    @pl.when(j < nq)
    def _():
        q0 = j * TQ
        start = pl.multiple_of(jnp.clip(q0 - WINDOW, 0, S - KB), WINDOW)
        kb = k_ref[pl.ds(start, KB), :]
        vb = v_ref[pl.ds(start, KB), :]
        q_abs = q0 + lax.broadcasted_iota(jnp.int32, (TQ, KB), 0)
        k_abs = start + lax.broadcasted_iota(jnp.int32, (TQ, KB), 1)
        valid = jnp.abs(q_abs - k_abs) <= WINDOW
        run((kb, vb, valid))

    @pl.when(j == nq)
    def _():
        run(None)


def _swa(qkv, sink, *, B, S, Lc):
    T = qkv.shape[0]
    TQ = ATT_TQ
    nq = S // TQ
    ctx_blk0 = B * S // Lc
    qw = SWA_HEADS * HEAD_DIM

    def q_idx(b, j):
        return (jnp.where(j < nq, b * nq + j, ctx_blk0 + b), COL_SQ // qw)

    return pl.pallas_call(
        functools.partial(_swa_kernel, nq=nq, S=S),
        grid=(B, nq + 1),
        in_specs=[
            pl.BlockSpec(memory_space=pltpu.SMEM),
            pl.BlockSpec((TQ, qw), q_idx),
            pl.BlockSpec((S, LANES), lambda b, j: (b, COL_SK // LANES)),
            pl.BlockSpec((S, LANES), lambda b, j: (b, COL_SV // LANES)),
            pl.BlockSpec((Lc, LANES), lambda b, j: (ctx_blk0 + b, COL_SK // LANES)),
            pl.BlockSpec((Lc, LANES), lambda b, j: (ctx_blk0 + b, COL_SV // LANES)),
        ],
        out_specs=pl.BlockSpec((TQ, qw), lambda b, j: (jnp.where(j < nq, b * nq + j, ctx_blk0 + b), 0)),
        out_shape=jax.ShapeDtypeStruct((T, qw), BF16),
        compiler_params=_cparams(("arbitrary", "arbitrary")),
        name="swa",
    )(sink, qkv, qkv, qkv, qkv, qkv)


def _diff_kernel(dl_ref, g_ref, q_ref, k_ref, vt_ref, qc_ref, kc_ref, vtc_ref, o_ref, oc_ref,
                 vt_scr, s_scr, *, S, Lc, lambda_init):
    TQ = ATT_TQ
    NK = S + Lc
    dv = 2 * HEAD_DIM
    vt_scr[0:dv, 0:S] = vt_ref[...]
    vt_scr[0:dv, S:NK] = vtc_ref[...]
    vt_scr[dv:dv + ONES_ROWS, :] = jnp.ones((ONES_ROWS, NK), BF16)
    dl = dl_ref[...]
    lam = (jnp.exp(jnp.sum(dl[0:1] * dl[1:2], axis=-1, keepdims=True))
           - jnp.exp(jnp.sum(dl[2:3] * dl[3:4], axis=-1, keepdims=True)) + lambda_init)
    low = lax.broadcasted_iota(jnp.int32, (TQ, LANES), 1) < HEAD_DIM

    def attend(q, lo):
        zero = jnp.zeros_like(q)
        q2 = jnp.concatenate([jnp.where(low, q, zero), jnp.where(low, zero, q)], axis=0)
        if lo < S:
            s_scr[0:S, :] = _dot_nt(k_ref[...], q2)
        s_scr[S:NK, :] = _dot_nt(kc_ref[...], q2)
        m = s_scr[lo:lo + DIFF_ROWS, :]
        for r in range(lo + DIFF_ROWS, NK, DIFF_ROWS):
            m = jnp.maximum(m, s_scr[r:r + DIFF_ROWS, :])
        m = jnp.max(m, axis=0, keepdims=True)
        p = jnp.exp2(s_scr[lo:NK, :] - m).astype(BF16)
        acc = _dot(vt_scr[:, lo:NK], p)
        o_n = acc[0:dv] * (1.0 / acc[dv:dv + 1])
        o_t = o_n[:, 0:TQ] - lam * o_n[:, TQ:2 * TQ]
        ms = jnp.mean(o_t * o_t, axis=0, keepdims=True)
        y_t = o_t * (lax.rsqrt(ms + EPS) * (1.0 - lambda_init))
        return (y_t.T * g_ref[...]).astype(BF16)

    def body(i, _):
        st = pl.multiple_of(i * TQ, TQ)
        o_ref[pl.ds(st, TQ), :] = attend(q_ref[pl.ds(st, TQ), :], 0)
        return 0

    lax.fori_loop(0, S // TQ, body, 0)
    oc_ref[...] = attend(qc_ref[...], S)


def _diff(qkv, dvt, diff_lambda, subln_g, *, B, S, Lc, lambda_init):
    H = DIFF_HEADS
    dv = 2 * HEAD_DIM
    NK = S + Lc
    ctx_blk0 = B * S // Lc
    return pl.pallas_call(
        functools.partial(_diff_kernel, S=S, Lc=Lc, lambda_init=lambda_init),
        grid=(B, H),
        in_specs=[
            _resident(diff_lambda.shape),
            _resident(subln_g.shape),
            pl.BlockSpec((S, LANES), lambda b, h: (b, COL_DQ // LANES + h)),
            pl.BlockSpec((S, LANES), lambda b, h: (b, COL_DK // LANES + h)),
            pl.BlockSpec((dv, S), lambda b, h: (h, b)),
            pl.BlockSpec((Lc, LANES), lambda b, h: (ctx_blk0 + b, COL_DQ // LANES + h)),
            pl.BlockSpec((Lc, LANES), lambda b, h: (ctx_blk0 + b, COL_DK // LANES + h)),
            pl.BlockSpec((dv, Lc), lambda b, h: (h, ctx_blk0 + b)),
        ],
        out_specs=[
            pl.BlockSpec((S, dv), lambda b, h: (b, h)),
            pl.BlockSpec((Lc, dv), lambda b, h: (b, h)),
        ],
        out_shape=[
            jax.ShapeDtypeStruct((B * S, H * dv), BF16),
            jax.ShapeDtypeStruct((B * Lc, H * dv), BF16),
        ],
        scratch_shapes=[
            pltpu.VMEM((dv + ONES_ROWS, NK), BF16),
            pltpu.VMEM((NK, 2 * ATT_TQ), F32),
        ],
        compiler_params=_cparams(("arbitrary", "arbitrary")),
        name="diff",
    )(diff_lambda, subln_g, qkv, qkv, dvt, qkv, qkv, dvt)


def _rope_tables(S, tm):
    rows = S // GRID_W
    row = jnp.repeat(jnp.arange(rows, dtype=F32), GRID_W)
    col = jnp.tile(jnp.arange(GRID_W, dtype=F32), rows)
    n_freq = HEAD_DIM // 4
    inv_freq = ROPE_BASE ** (-jnp.arange(n_freq, dtype=F32) / n_freq)
    ang = jnp.concatenate([row[:, None] * inv_freq, col[:, None] * inv_freq], axis=-1)
    cos, sin = jnp.cos(ang), jnp.sin(ang)
    reps = LANES // (HEAD_DIM // 2)
    sign = np.tile(np.concatenate([-np.ones(HEAD_DIM // 2), np.ones(HEAD_DIM // 2)]), LANES // HEAD_DIM)
    cos_t = jnp.concatenate([jnp.tile(cos, (1, reps)), jnp.ones((tm, LANES), F32)], axis=0)
    sin_t = jnp.concatenate([jnp.tile(sin, (1, reps)) * sign.astype(np.float32), jnp.zeros((tm, LANES), F32)], axis=0)
    return cos_t, sin_t


def _block_diag(w):
    nd, K, c, _ = w.shape
    eye = jnp.eye(K, dtype=w.dtype)
    return jnp.einsum('dkij,kl->dkilj', w, eye).reshape(nd, K * c, K * c)


def kernel(x, c, ctx, c_ctx, w_ada, b_ada, norm_g, ffn1_w_gu, ffn1_w_down, ffn2_w_gu, ffn2_w_down,
           w_in, w_out, conv_w, conv_b, lru_w_r, lru_b_r, lru_w_i, lru_b_i, lru_lambda,
           swa_sink, diff_lambda, diff_subln_g, final_g):
    B, S, D = x.shape
    Lc = ctx.shape[1]
    depth = w_ada.shape[0]
    tm = TOKEN_TILE
    assert S % tm == 0 and (B * Lc) % tm == 0 and Lc == ATT_TQ and S % GRID_W == 0 and S % LRU_CHUNK == 0
    assert Lc == LRU_CHUNK and S >= ATT_TQ + 2 * WINDOW
    n_lat_tiles = B * S // tm
    n_tiles = n_lat_tiles + B * Lc // tm
    geom = (n_lat_tiles, S // tm, B)

    h = jnp.concatenate([x.reshape(B * S, D), ctx.reshape(B * Lc, D)], axis=0)

    n_cond = B + 1
    pad = (-n_cond) % SUBLANES
    cond = jnp.concatenate([c, c_ctx[None], jnp.zeros((pad, D), F32)], axis=0)
    mod_all = _ada(cond, w_ada, b_ada).reshape(depth, n_cond + pad, N_MOD, D)

    cos_t, sin_t = _rope_tables(S, tm)
    q_perm = np.arange(w_in.shape[-1])
    sq0 = 2 * LRU_WIDTH
    q_perm[sq0:sq0 + SWA_HEADS * HEAD_DIM] = sq0 + np.concatenate(
        [np.arange(hd * HEAD_DIM, (hd + 1) * HEAD_DIM) for hd in SWA_HEAD_ORDER])
    o_perm = np.arange(w_out.shape[1])
    o_perm[LRU_WIDTH:LRU_WIDTH + SWA_HEADS * HEAD_DIM] = LRU_WIDTH + np.concatenate(
        [np.arange(hd * HEAD_DIM, (hd + 1) * HEAD_DIM) for hd in SWA_HEAD_ORDER])

    for l in range(depth):
        last = l == depth - 1
        mod = mod_all[l]
        lambda_init = 0.8 - 0.6 * math.exp(-0.3 * l)
        g = norm_g[l]
        h = _ffn(h, mod, g[0:1], ffn1_w_gu[l].astype(BF16), ffn1_w_down[l].astype(BF16), final_g[None],
                 k=0, n_tiles=n_tiles, geom=geom)
        lru_in, qkv, dvt = _inproj(h, mod, g[1:2], w_in[l][:, q_perm].astype(BF16), cos_t, sin_t,
                                   n_tiles=n_tiles, geom=geom)
        y_lat, y_ctx = _lru(lru_in, conv_w[l], conv_b[l][None], _block_diag(lru_w_r[l]), lru_b_r[l][:, None],
                            _block_diag(lru_w_i[l]), lru_b_i[l][:, None], lru_lambda[l][:, None], B=B, S=S, Lc=Lc)
        y_swa = _swa(qkv, swa_sink[l][None], B=B, S=S, Lc=Lc)
        d_lat, d_ctx = _diff(qkv, dvt, diff_lambda[l], diff_subln_g[l][None], B=B, S=S, Lc=Lc,
                             lambda_init=lambda_init)
        n_out = n_lat_tiles if last else n_tiles
        h = _outproj(h, mod, y_lat, y_ctx, y_swa, d_lat, d_ctx, w_out[l][o_perm, :].astype(BF16),
                     n_tiles=n_out, geom=geom)
        h = _ffn(h, mod, g[2:3], ffn2_w_gu[l].astype(BF16), ffn2_w_down[l].astype(BF16), final_g[None],
                 k=2, n_tiles=n_out, geom=geom, final_norm=last)
    return h[:B * S].reshape(B, S, D)
```

```python
import functools
import math

import numpy as np
import jax
import jax.numpy as jnp
from jax import lax
from jax.experimental import pallas as pl
from jax.experimental.pallas import tpu as pltpu

F32 = jnp.float32
BF16 = jnp.bfloat16

GRID_W = 64
N_MOD = 9
EPS = 1e-6
FFN_RES = 0.5
HEAD_DIM = 64
ROPE_BASE = 10000.0
LRU_WIDTH = 256
LRU_BLOCKS = 4
CONV_WIDTH = 4
CONV_LEFT = 2
LRU_C = 8.0
SWA_HEADS = 4
SWA_KV_HEADS = 2
WINDOW = 128
DIFF_HEADS = 4
NEG_INF = -1e30

LANES = 128
SUBLANES = 8
VMEM_LIMIT_BYTES = 56 * 1024 * 1024

TOKEN_TILE = 512
FFN_CHUNKS = 1
ADA_COLS = 1024
ATT_TQ = 256
LRU_CHUNK = 256

QK_WIDTH = 1408
COL_SQ, COL_SK, COL_DQ, COL_DK = 0, 256, 384, 896
VT_ROWS = 640
ROW_SV, ROW_DV = 0, 128
LOG2E = math.log2(math.e)
DIFF_ROWS = 16
DIFF_CHUNK = 1024
ONES_ROWS = 16
SWA_HEAD_ORDER = (0, 2, 1, 3)


def _cparams(semantics):
    return pltpu.CompilerParams(dimension_semantics=semantics, vmem_limit_bytes=VMEM_LIMIT_BYTES)


def _resident(shape):
    nd = len(shape)
    return pl.BlockSpec(shape, lambda *_: (0,) * nd, pipeline_mode=pl.Buffered(1))


def _dot(a, b):
    return jnp.dot(a, b, preferred_element_type=F32)


def _dot_nt(a, b):
    return lax.dot_general(a, b, (((1,), (1,)), ((), ())), preferred_element_type=F32)


def _rms(x, g):
    return x * lax.rsqrt(jnp.mean(x * x, axis=-1, keepdims=True) + EPS) * g


def _modnorm(h, g, mod_ref, k):
    shift = mod_ref[0, 3 * k:3 * k + 1, :]
    scale = mod_ref[0, 3 * k + 1:3 * k + 2, :]
    return _rms(h, g) * (1.0 + scale) + shift


def _expm1_of_square(a, t):
    u = a * a
    near_one = u == 1.0
    stable = (u - 1.0) * t / jnp.where(near_one, 1.0, jnp.log(u))
    return jnp.where(u < 0.5, u - 1.0, jnp.where(near_one, t, stable))


def _ada_kernel(c_ref, w_ref, b_ref, o_ref):
    c = c_ref[...]
    s = (c * jax.nn.sigmoid(c)).astype(BF16)
    o_ref[0] = _dot(s, w_ref[0].astype(BF16)) + b_ref[0]


def _ada(cond, w_ada, b_ada):
    L, D, N = w_ada.shape
    R = cond.shape[0]
    return pl.pallas_call(
        _ada_kernel,
        grid=(L, N // ADA_COLS),
        in_specs=[
            pl.BlockSpec((R, D), lambda l, n: (0, 0)),
            pl.BlockSpec((1, D, ADA_COLS), lambda l, n: (l, 0, n)),
            pl.BlockSpec((1, 1, ADA_COLS), lambda l, n: (l, 0, n)),
        ],
        out_specs=pl.BlockSpec((1, R, ADA_COLS), lambda l, n: (l, 0, n)),
        out_shape=jax.ShapeDtypeStruct((L, R, N), F32),
        compiler_params=_cparams(("arbitrary", "arbitrary")),
        name="ada",
    )(cond, w_ada, b_ada.reshape(L, 1, N))


def _mod_index(n_lat_tiles, tiles_per_batch, n_batch):
    def idx(i):
        return (jnp.where(i < n_lat_tiles, i // tiles_per_batch, n_batch), 0, 0)
    return idx


def _ffn_kernel(*refs, k, d_ff, final_norm, n_lat_tiles, split_input):
    if split_input:
        hl_ref, hc_ref, mod_ref, g_ref, wgu_ref, wd_ref, fg_ref, o_ref = refs
        h = jnp.where(pl.program_id(0) < n_lat_tiles, hl_ref[...], hc_ref[...])
    else:
        h_ref, mod_ref, g_ref, wgu_ref, wd_ref, fg_ref, o_ref = refs
        h = h_ref[...]
    xn = _modnorm(h, g_ref[...], mod_ref, k).astype(BF16)
    tf = d_ff // FFN_CHUNKS
    acc = None
    for c in range(FFN_CHUNKS):
        lo = c * tf
        g = _dot(xn, wgu_ref[:, lo:lo + tf])
        u = _dot(xn, wgu_ref[:, d_ff + lo:d_ff + lo + tf])
        a = (g * jax.nn.sigmoid(g) * u).astype(BF16)
        part = _dot(a, wd_ref[lo:lo + tf, :])
        acc = part if acc is None else acc + part
    gate = mod_ref[0, 3 * k + 2:3 * k + 3, :]
    out = h + (FFN_RES * gate) * acc
    if final_norm:
        out = _rms(out, fg_ref[...])
    o_ref[...] = out


def _ffn(h, mod, g, w_gu, w_down, final_g, *, k, n_tiles, geom, final_norm=False):
    split = isinstance(h, tuple)
    hs = h if split else (h,)
    D = hs[0].shape[1]
    d_ff = w_down.shape[0]
    tm = TOKEN_TILE
    n_lat_tiles = geom[0]
    if split:
        h_specs = [pl.BlockSpec((tm, D), lambda i: (jnp.minimum(i, n_lat_tiles - 1), 0)),
                   pl.BlockSpec((tm, D), lambda i: (jnp.maximum(i - n_lat_tiles, 0), 0))]
    else:
        h_specs = [pl.BlockSpec((tm, D), lambda i: (i, 0))]
    return pl.pallas_call(
        functools.partial(_ffn_kernel, k=k, d_ff=d_ff, final_norm=final_norm, n_lat_tiles=n_lat_tiles,
                          split_input=split),
        grid=(n_tiles,),
        in_specs=h_specs + [
            pl.BlockSpec((1, N_MOD, D), _mod_index(*geom)),
            _resident((1, D)),
            _resident((D, 2 * d_ff)),
            _resident((d_ff, D)),
            _resident((1, D)),
        ],
        out_specs=pl.BlockSpec((tm, D), lambda i: (i, 0)),
        out_shape=jax.ShapeDtypeStruct((n_tiles * tm, D), F32),
        compiler_params=_cparams(("arbitrary",)),
        name="ffn",
    )(*hs, mod, g, w_gu, w_down, final_g)


def _rope(x, cos, sin):
    lane = lax.broadcasted_iota(jnp.int32, x.shape, 1)
    first = (lane & (HEAD_DIM - 1)) < (HEAD_DIM // 2)
    partner = jnp.where(first, pltpu.roll(x, LANES - HEAD_DIM // 2, 1), pltpu.roll(x, HEAD_DIM // 2, 1))
    return x * cos + partner * sin


def _inproj_kernel(h_ref, mod_ref, g_ref, w_ref, cos_ref, sin_ref, lru_ref, qk_ref, vt_ref):
    xn = _modnorm(h_ref[...], g_ref[...], mod_ref, 1).astype(BF16)
    y = _dot(xn, w_ref[...])
    lw = 2 * LRU_WIDTH
    lru_ref[...] = y[:, :lw]
    cos = cos_ref[...]
    sin = sin_ref[...]
    q_scale = HEAD_DIM ** -0.5 * LOG2E
    for c0 in range(0, QK_WIDTH, LANES):
        blk = _rope(y[:, lw + c0:lw + c0 + LANES], cos, sin)
        if (COL_SQ <= c0 < COL_SK) or (COL_DQ <= c0 < COL_DK):
            blk = blk * q_scale
        qk_ref[:, c0:c0 + LANES] = blk.astype(BF16)
    vt_ref[...] = y[:, lw + QK_WIDTH:lw + QK_WIDTH + VT_ROWS].T.astype(BF16)


def _inproj(h, mod, g, w_in, cos_t, sin_t, *, n_tiles, geom):
    T, D = h.shape
    tm = TOKEN_TILE
    n_lat_tiles, tiles_per_batch, _ = geom

    def tab_idx(i):
        return (jnp.where(i < n_lat_tiles, i % tiles_per_batch, tiles_per_batch), 0)

    return pl.pallas_call(
        _inproj_kernel,
        grid=(n_tiles,),
        in_specs=[
            pl.BlockSpec((tm, D), lambda i: (i, 0)),
            pl.BlockSpec((1, N_MOD, D), _mod_index(*geom)),
            _resident((1, D)),
            _resident(w_in.shape),
            pl.BlockSpec((tm, LANES), tab_idx),
            pl.BlockSpec((tm, LANES), tab_idx),
        ],
        out_specs=[
            pl.BlockSpec((tm, 2 * LRU_WIDTH), lambda i: (i, 0)),
            pl.BlockSpec((tm, QK_WIDTH), lambda i: (i, 0)),
            pl.BlockSpec((VT_ROWS, tm), lambda i: (0, i)),
        ],
        out_shape=[
            jax.ShapeDtypeStruct((T, 2 * LRU_WIDTH), F32),
            jax.ShapeDtypeStruct((T, QK_WIDTH), BF16),
            jax.ShapeDtypeStruct((VT_ROWS, T), BF16),
        ],
        compiler_params=_cparams(("arbitrary",)),
        name="inproj",
    )(h, mod, g, w_in, cos_t, sin_t)


def _outproj_kernel(h_ref, mod_ref, lru_lat, lru_ctx, yswa_ref, diff_lat, diff_ctx, w_ref, o_ref,
                    *, n_lat_tiles, has_ctx):
    def emit(lru_ref, diff_ref):
        y = jnp.concatenate([lru_ref[...], yswa_ref[...], diff_ref[...]], axis=1)
        gate = mod_ref[0, 5:6, :]
        o_ref[...] = h_ref[...] + gate * _dot(y, w_ref[...])

    if not has_ctx:
        emit(lru_lat, diff_lat)
        return

    @pl.when(pl.program_id(0) < n_lat_tiles)
    def _():
        emit(lru_lat, diff_lat)

    @pl.when(pl.program_id(0) >= n_lat_tiles)
    def _():
        emit(lru_ctx, diff_ctx)


def _outproj(h, mod, lru_lat, lru_ctx, y_swa, diff_lat, diff_ctx, w_out, *, n_tiles, geom):
    T, D = h.shape
    tm = TOKEN_TILE
    n_lat_tiles = geom[0]
    has_ctx = n_tiles > n_lat_tiles
    dw = DIFF_HEADS * 2 * HEAD_DIM

    def lat_idx(i):
        return (jnp.minimum(i, n_lat_tiles - 1), 0)

    def ctx_idx(i):
        return (jnp.maximum(i - n_lat_tiles, 0), 0)

    return pl.pallas_call(
        functools.partial(_outproj_kernel, n_lat_tiles=n_lat_tiles, has_ctx=has_ctx),
        grid=(n_tiles,),
        in_specs=[
            pl.BlockSpec((tm, D), lambda i: (i, 0)),
            pl.BlockSpec((1, N_MOD, D), _mod_index(*geom)),
            pl.BlockSpec((tm, LRU_WIDTH), lat_idx),
            pl.BlockSpec((tm, LRU_WIDTH), ctx_idx),
            pl.BlockSpec((tm, SWA_HEADS * HEAD_DIM), lambda i: (i, 0)),
            pl.BlockSpec((tm, dw), lat_idx),
            pl.BlockSpec((tm, dw), ctx_idx),
            _resident(w_out.shape),
        ],
        out_specs=pl.BlockSpec((tm, D), lambda i: (i, 0)),
        out_shape=jax.ShapeDtypeStruct((n_tiles * tm, D), F32),
        compiler_params=_cparams(("arbitrary",)),
        name="outproj",
    )(h, mod, lru_lat, lru_ctx, y_swa, diff_lat, diff_ctx, w_out)


def _lru_kernel(lat_ref, ctx_ref, cw_ref, cb_ref, wr_ref, br_ref, wi_ref, bi_ref, lam_ref,
                ylat_ref, yctx_ref, xpad, u_scr, hf_scr, ge_scr, *g_scr, S, Lc):
    W = LRU_WIDTH
    TC = LRU_CHUNK
    PAD = SUBLANES
    row = lax.broadcasted_iota(jnp.int32, (TC, W), 0)
    group = lax.broadcasted_iota(jnp.int32, (TC // SUBLANES, W), 0)

    def conv_chunk(start):
        a = xpad[pl.ds(start, TC + 2 * PAD), :]
        u = cb_ref[...] + jnp.zeros((TC, W), F32)
        for k in range(CONV_WIDTH):
            sh = (CONV_LEFT - k) % (TC + 2 * PAD)
            r = a if sh == 0 else pltpu.roll(a, sh, 0)
            u = u + r[PAD:PAD + TC] * cw_ref[k:k + 1, :]
        return u

    def scan_chunk(u, d, carry, reverse):
        r = jax.nn.sigmoid(_dot(u, wr_ref[d]) + br_ref[d])
        i = jax.nn.sigmoid(_dot(u, wi_ref[d]) + bi_ref[d])
        log_a = (-LRU_C) * r * jax.nn.softplus(-lam_ref[d])
        A = jnp.exp(log_a)
        Bv = jnp.sqrt(-_expm1_of_square(A, 2.0 * log_a)) * i * u

        def combine(A, Bv, pos, axis):
            n = A.shape[axis]
            s = 1
            while s < n:
                if reverse:
                    keep = pos < n - s
                    sh = n - s
                else:
                    keep = pos >= s
                    sh = s
                a_sh = jnp.where(keep, pltpu.roll(A, sh, axis), 1.0)
                b_sh = jnp.where(keep, pltpu.roll(Bv, sh, axis), 0.0)
                Bv = A * b_sh + Bv
                A = A * a_sh
                s *= 2
            return A, Bv

        G = SUBLANES
        ng = TC // G
        A, Bv = combine(A.reshape(ng, G, W), Bv.reshape(ng, G, W), lax.broadcasted_iota(jnp.int32, (ng, G, W), 1), 1)
        A = A.reshape(TC, W)
        Bv = Bv.reshape(TC, W)
        edge = 0 if reverse else G - 1

        def boundary_rows(planes, x):
            out = []
            for c, scr in enumerate(planes):
                scr[...] = x[:, c * LANES:(c + 1) * LANES]
                out.append(scr[pl.ds(edge, ng, stride=G), :])
            return jnp.concatenate(out, axis=1)

        n_pl = W // LANES
        Ag, Bg = combine(boundary_rows(g_scr[:n_pl], A), boundary_rows(g_scr[n_pl:], Bv), group, 0)
        ge_scr[...] = Ag * carry + Bg
        parts = []
        for g in range(ng):
            nb_g = g + 1 if reverse else g - 1
            h_in = carry if (nb_g < 0 or nb_g >= ng) else ge_scr[nb_g:nb_g + 1, :]
            parts.append(A[g * G:(g + 1) * G] * h_in + Bv[g * G:(g + 1) * G])
        h = jnp.concatenate(parts, axis=0)
        return h, (h[0:1] if reverse else h[TC - 1:TC])

    def gelu(x):
        return jax.nn.gelu(x)

    zero_pad = jnp.zeros((PAD, W), F32)
    zero_state = jnp.zeros((1, W), F32)

    xpad[0:PAD, :] = zero_pad
    xpad[PAD:PAD + Lc, :] = ctx_ref[:, 0:W]
    xpad[PAD + Lc:2 * PAD + Lc, :] = zero_pad
    uc = conv_chunk(0)
    hcf, carry_f = scan_chunk(uc, 0, zero_state, False)
    hcb, carry_b = scan_chunk(uc, 1, zero_state, True)
    yctx_ref[...] = ((hcf + hcb) * gelu(ctx_ref[:, W:2 * W])).astype(BF16)

    xpad[PAD:PAD + S, :] = lat_ref[:, 0:W]
    xpad[PAD + S:2 * PAD + S, :] = zero_pad
    nc = S // TC

    def conv_body(c, _):
        st = pl.multiple_of(c * TC, TC)
        u_scr[pl.ds(st, TC), :] = conv_chunk(st)
        return 0

    lax.fori_loop(0, nc, conv_body, 0)

    def fwd_body(c, carry):
        st = pl.multiple_of(c * TC, TC)
        h, carry = scan_chunk(u_scr[pl.ds(st, TC), :], 0, carry, False)
        hf_scr[pl.ds(st, TC), :] = h
        return carry

    lax.fori_loop(0, nc, fwd_body, carry_f)

    def bwd_body(c, carry):
        st = pl.multiple_of((nc - 1 - c) * TC, TC)
        h, carry = scan_chunk(u_scr[pl.ds(st, TC), :], 1, carry, True)
        y = (hf_scr[pl.ds(st, TC), :] + h) * gelu(lat_ref[pl.ds(st, TC), W:2 * W])
        ylat_ref[pl.ds(st, TC), :] = y.astype(BF16)
        return carry

    lax.fori_loop(0, nc, bwd_body, carry_b)


def _lru(lru_in, conv_w, conv_b, w_r, b_r, w_i, b_i, lam, *, B, S, Lc):
    W = LRU_WIDTH
    t_lat = B * S
    ctx_blk0 = t_lat // Lc
    return pl.pallas_call(
        functools.partial(_lru_kernel, S=S, Lc=Lc),
        grid=(B,),
        in_specs=[
            pl.BlockSpec((S, 2 * W), lambda b: (b, 0)),
            pl.BlockSpec((Lc, 2 * W), lambda b: (ctx_blk0 + b, 0)),
            _resident(conv_w.shape),
            _resident(conv_b.shape),
            _resident(w_r.shape),
            _resident(b_r.shape),
            _resident(w_i.shape),
            _resident(b_i.shape),
            _resident(lam.shape),
        ],
        out_specs=[
            pl.BlockSpec((S, W), lambda b: (b, 0)),
            pl.BlockSpec((Lc, W), lambda b: (b, 0)),
        ],
        out_shape=[
            jax.ShapeDtypeStruct((t_lat, W), BF16),
            jax.ShapeDtypeStruct((B * Lc, W), BF16),
        ],
        scratch_shapes=[
            pltpu.VMEM((S + 2 * SUBLANES, W), F32),
            pltpu.VMEM((S, W), F32),
            pltpu.VMEM((S, W), F32),
            pltpu.VMEM((LRU_CHUNK // SUBLANES, W), F32),
        ] + [pltpu.VMEM((LRU_CHUNK, LANES), F32)] * (2 * (W // LANES)) + [
        ],
        compiler_params=_cparams(("arbitrary",)),
        name="lru",
    )(lru_in, lru_in, conv_w, conv_b, w_r, b_r, w_i, b_i, lam)


def _swa_kernel(sink_ref, q_ref, k_ref, vt0_ref, vt1_ref, vt2_ref, vt3_ref, kc_ref, vtc_ref, o_ref, *, nq, S):
    TQ = ATT_TQ
    KB = TQ + 2 * WINDOW
    hd = HEAD_DIM
    j = pl.program_id(1)
    q = q_ref[...]
    low = lax.broadcasted_iota(jnp.int32, (TQ, LANES), 1) < hd
    zero = jnp.zeros((TQ, LANES), BF16)
    rows = []
    for grp in range(2):
        qg = q[:, grp * LANES:(grp + 1) * LANES]
        rows += [jnp.where(low, qg, zero), jnp.where(low, zero, qg)]
    q4 = jnp.concatenate(rows, axis=0)
    col = lax.broadcasted_iota(jnp.int32, (1, SWA_HEADS * TQ), 1)
    sink = jnp.full((1, SWA_HEADS * TQ), sink_ref[0, SWA_HEAD_ORDER[-1]] * LOG2E, F32)
    for c in range(SWA_HEADS - 2, -1, -1):
        sink = jnp.where(col < (c + 1) * TQ, sink_ref[0, SWA_HEAD_ORDER[c]] * LOG2E, sink)

    def run(band):
        s = _dot_nt(kc_ref[...], q4)
        vt = vtc_ref[...]
        if band is not None:
            kb, vtb, valid = band
            s_b = _dot_nt(kb, q4)
            s_b = jnp.concatenate(
                [jnp.where(valid, s_b[:, c * TQ:(c + 1) * TQ], NEG_INF) for c in range(SWA_HEADS)], axis=1)
            s = jnp.concatenate([s_b, s], axis=0)
            vt = jnp.concatenate([vtb, vt], axis=1)
        m = jnp.maximum(jnp.max(s, axis=0, keepdims=True), sink)
        p = jnp.exp2(s - m).astype(BF16)
        vt_ext = jnp.concatenate([vt, jnp.ones((ONES_ROWS, vt.shape[1]), BF16)], axis=0)
        acc = _dot(vt_ext, p)
        den = acc[2 * hd:2 * hd + 1] + jnp.exp2(sink - m)
        o_n = acc[0:2 * hd] * (1.0 / den)
        y_t = jnp.concatenate(
            [o_n[(c % 2) * hd:(c % 2 + 1) * hd, c * TQ:(c + 1) * TQ] for c in range(SWA_HEADS)], axis=0)
        o_ref[...] = y_t.T.astype(BF16)

    @pl.when(j < nq)
    def _():
        start = _band_start(j, S) * LANES
        kb = k_ref[pl.ds(pl.multiple_of(start, LANES), KB), :]
        vtb = jnp.concatenate([vt0_ref[...], vt1_ref[...], vt2_ref[...], vt3_ref[...]], axis=1)
        k_abs = start + lax.broadcasted_iota(jnp.int32, (KB, TQ), 0)
        q_abs = j * TQ + lax.broadcasted_iota(jnp.int32, (KB, TQ), 1)
        run((kb, vtb, jnp.abs(q_abs - k_abs) <= WINDOW))

    @pl.when(j == nq)
    def _():
        run(None)


def _band_start(j, S):
    return jnp.clip(2 * j - 1, 0, S // LANES - (ATT_TQ + 2 * WINDOW) // LANES)


def _swa(qk, vt, sink, *, B, S, Lc):
    T = qk.shape[0]
    TQ = ATT_TQ
    nq = S // TQ
    ctx_blk0 = B * S // Lc
    qw = SWA_HEADS * HEAD_DIM
    sv_blk = ROW_SV // LANES

    def q_idx(b, j):
        return (jnp.where(j < nq, b * nq + j, ctx_blk0 + b), COL_SQ // qw)

    def vt_chunk(d):
        return pl.BlockSpec((LANES, LANES), lambda b, j: (sv_blk, b * (S // LANES) + _band_start(j, S) + d))

    return pl.pallas_call(
        functools.partial(_swa_kernel, nq=nq, S=S),
        grid=(B, nq + 1),
        in_specs=[
            pl.BlockSpec(memory_space=pltpu.SMEM),
            pl.BlockSpec((TQ, qw), q_idx),
            pl.BlockSpec((S, LANES), lambda b, j: (b, COL_SK // LANES)),
            vt_chunk(0), vt_chunk(1), vt_chunk(2), vt_chunk(3),
            pl.BlockSpec((Lc, LANES), lambda b, j: (ctx_blk0 + b, COL_SK // LANES)),
            pl.BlockSpec((LANES, Lc), lambda b, j: (sv_blk, ctx_blk0 + b)),
        ],
        out_specs=pl.BlockSpec((TQ, qw), lambda b, j: (jnp.where(j < nq, b * nq + j, ctx_blk0 + b), 0)),
        out_shape=jax.ShapeDtypeStruct((T, qw), BF16),
        compiler_params=_cparams(("arbitrary", "arbitrary")),
        name="swa",
    )(sink, qk, qk, vt, vt, vt, vt, qk, vt)


def _diff_kernel(dl_ref, g_ref, q_ref, k_ref, vt_ref, qc_ref, kc_ref, vtc_ref, o_ref, oc_ref,
                 vt_scr, acc_a, acc_b, *s_scr, S, Lc, lambda_init):
    TQ = ATT_TQ
    NK = S + Lc
    dv = 2 * HEAD_DIM
    vt_scr[0:dv, 0:S] = vt_ref[...]
    vt_scr[0:dv, S:NK] = vtc_ref[...]
    vt_scr[dv:dv + ONES_ROWS, :] = jnp.ones((ONES_ROWS, NK), BF16)
    dl = dl_ref[...]
    lam = (jnp.exp(jnp.sum(dl[0:1] * dl[1:2], axis=-1, keepdims=True))
           - jnp.exp(jnp.sum(dl[2:3] * dl[3:4], axis=-1, keepdims=True)) + lambda_init)
    low = lax.broadcasted_iota(jnp.int32, (TQ, LANES), 1) < HEAD_DIM
    n_lat = S // DIFF_CHUNK
    chunks = [(n, n * DIFF_CHUNK, DIFF_CHUNK, k_ref, n * DIFF_CHUNK) for n in range(n_lat)]
    chunks.append((n_lat, S, Lc, kc_ref, 0))
    set_a, set_b = s_scr[:n_lat + 1], s_scr[n_lat + 1:]

    def both_halves(q):
        zero = jnp.zeros_like(q)
        return jnp.concatenate([jnp.where(low, q, zero), jnp.where(low, zero, q)], axis=0)

    def scores(chunk, q2, s_dst, m_acc):
        n, _, rows, kref, off = chunk
        blk = _dot_nt(kref[off:off + rows, :], q2)
        s_dst[n][...] = blk
        for r in range(0, rows, DIFF_ROWS):
            part = blk[r:r + DIFF_ROWS]
            m_acc = part if m_acc is None else jnp.maximum(m_acc, part)
        return m_acc

    def values(chunk, s_src, m, acc, first):
        n, r0, rows, _, _ = chunk
        p = jnp.exp2(s_src[n][...] - m).astype(BF16)
        part = _dot(vt_scr[:, r0:r0 + rows], p)
        if first:
            acc[...] = part
        else:
            acc[...] += part

    def finish(acc):
        o_n = acc[0:dv, :] * (1.0 / acc[dv:dv + 1, :])
        o_t = o_n[:, 0:TQ] - lam * o_n[:, TQ:2 * TQ]
        ms = jnp.mean(o_t * o_t, axis=0, keepdims=True)
        y_t = o_t * (lax.rsqrt(ms + EPS) * (1.0 - lambda_init))
        return (y_t.T * g_ref[...]).astype(BF16)

    def colmax(m_acc):
        return jnp.max(m_acc, axis=0, keepdims=True)

    def stage(s_cur, m_cur, acc, s_nxt, q_nxt):
        q2 = both_halves(q_nxt)
        m_acc = None
        for chunk in chunks:
            m_acc = scores(chunk, q2, s_nxt, m_acc)
            values(chunk, s_cur, m_cur, acc, chunk[0] == 0)
        return finish(acc), colmax(m_acc)

    def q_block(i):
        return q_ref[pl.ds(pl.multiple_of(i * TQ, TQ), TQ), :]

    nb = S // TQ
    q2 = both_halves(q_block(0))
    m_acc = None
    for chunk in chunks:
        m_acc = scores(chunk, q2, set_a, m_acc)

    def body(i, m_a):
        b0 = 2 * i
        y, m_b = stage(set_a, m_a, acc_a, set_b, q_block(b0 + 1))
        o_ref[pl.ds(pl.multiple_of(b0 * TQ, TQ), TQ), :] = y
        y, m_a = stage(set_b, m_b, acc_b, set_a, q_block(jnp.minimum(b0 + 2, nb - 1)))
        o_ref[pl.ds(pl.multiple_of((b0 + 1) * TQ, TQ), TQ), :] = y
        return m_a

    lax.fori_loop(0, nb // 2, body, colmax(m_acc))

    m_c = colmax(scores(chunks[-1], both_halves(qc_ref[...]), set_a, None))
    values(chunks[-1], set_a, m_c, acc_a, True)
    oc_ref[...] = finish(acc_a)


def _diff(qk, vt, diff_lambda, subln_g, *, B, S, Lc, lambda_init):
    H = DIFF_HEADS
    dv = 2 * HEAD_DIM
    NK = S + Lc
    ctx_blk0 = B * S // Lc
    return pl.pallas_call(
        functools.partial(_diff_kernel, S=S, Lc=Lc, lambda_init=lambda_init),
        grid=(B, H),
        in_specs=[
            _resident(diff_lambda.shape),
            _resident(subln_g.shape),
            pl.BlockSpec((S, LANES), lambda b, h: (b, COL_DQ // LANES + h)),
            pl.BlockSpec((S, LANES), lambda b, h: (b, COL_DK // LANES + h)),
            pl.BlockSpec((dv, S), lambda b, h: (ROW_DV // dv + h, b)),
            pl.BlockSpec((Lc, LANES), lambda b, h: (ctx_blk0 + b, COL_DQ // LANES + h)),
            pl.BlockSpec((Lc, LANES), lambda b, h: (ctx_blk0 + b, COL_DK // LANES + h)),
            pl.BlockSpec((dv, Lc), lambda b, h: (ROW_DV // dv + h, ctx_blk0 + b)),
        ],
        out_specs=[
            pl.BlockSpec((S, dv), lambda b, h: (b, h)),
            pl.BlockSpec((Lc, dv), lambda b, h: (b, h)),
        ],
        out_shape=[
            jax.ShapeDtypeStruct((B * S, H * dv), BF16),
            jax.ShapeDtypeStruct((B * Lc, H * dv), BF16),
        ],
        scratch_shapes=[
            pltpu.VMEM((dv + ONES_ROWS, NK), BF16),
            pltpu.VMEM((dv + ONES_ROWS, 2 * ATT_TQ), F32),
            pltpu.VMEM((dv + ONES_ROWS, 2 * ATT_TQ), F32),
        ] + 2 * ([pltpu.VMEM((DIFF_CHUNK, 2 * ATT_TQ), F32)] * (S // DIFF_CHUNK) + [pltpu.VMEM((Lc, 2 * ATT_TQ), F32)]),
        compiler_params=_cparams(("arbitrary", "arbitrary")),
        name="diff",
    )(diff_lambda, subln_g, qk, qk, vt, qk, qk, vt)


def _rope_tables(S, tm):
    rows = S // GRID_W
    row = jnp.repeat(jnp.arange(rows, dtype=F32), GRID_W)
    col = jnp.tile(jnp.arange(GRID_W, dtype=F32), rows)
    n_freq = HEAD_DIM // 4
    inv_freq = ROPE_BASE ** (-jnp.arange(n_freq, dtype=F32) / n_freq)
    ang = jnp.concatenate([row[:, None] * inv_freq, col[:, None] * inv_freq], axis=-1)
    cos, sin = jnp.cos(ang), jnp.sin(ang)
    reps = LANES // (HEAD_DIM // 2)
    sign = np.tile(np.concatenate([-np.ones(HEAD_DIM // 2), np.ones(HEAD_DIM // 2)]), LANES // HEAD_DIM)
    cos_t = jnp.concatenate([jnp.tile(cos, (1, reps)), jnp.ones((tm, LANES), F32)], axis=0)
    sin_t = jnp.concatenate([jnp.tile(sin, (1, reps)) * sign.astype(np.float32), jnp.zeros((tm, LANES), F32)], axis=0)
    return cos_t, sin_t


def _block_diag(w):
    nd, K, c, _ = w.shape
    eye = jnp.eye(K, dtype=w.dtype)
    return jnp.einsum('dkij,kl->dkilj', w, eye).reshape(nd, K * c, K * c)


def kernel(x, c, ctx, c_ctx, w_ada, b_ada, norm_g, ffn1_w_gu, ffn1_w_down, ffn2_w_gu, ffn2_w_down,
           w_in, w_out, conv_w, conv_b, lru_w_r, lru_b_r, lru_w_i, lru_b_i, lru_lambda,
           swa_sink, diff_lambda, diff_subln_g, final_g):
    B, S, D = x.shape
    Lc = ctx.shape[1]
    depth = w_ada.shape[0]
    tm = TOKEN_TILE
    assert S % tm == 0 and (B * Lc) % tm == 0 and Lc == ATT_TQ and S % GRID_W == 0 and S % LRU_CHUNK == 0
    assert Lc == LRU_CHUNK and S >= ATT_TQ + 2 * WINDOW
    assert S % DIFF_CHUNK == 0 and (S // ATT_TQ) % 2 == 0
    n_lat_tiles = B * S // tm
    n_tiles = n_lat_tiles + B * Lc // tm
    geom = (n_lat_tiles, S // tm, B)

    h = (x.reshape(B * S, D), ctx.reshape(B * Lc, D))

    n_cond = B + 1
    pad = (-n_cond) % SUBLANES
    cond = jnp.concatenate([c, c_ctx[None], jnp.zeros((pad, D), F32)], axis=0)
    mod_all = _ada(cond, w_ada, b_ada).reshape(depth, n_cond + pad, N_MOD, D)

    cos_t, sin_t = _rope_tables(S, tm)
    sizes = (LRU_WIDTH, LRU_WIDTH, SWA_HEADS * HEAD_DIM, SWA_KV_HEADS * HEAD_DIM, SWA_KV_HEADS * HEAD_DIM,
             DIFF_HEADS * 2 * HEAD_DIM, DIFF_HEADS * 2 * HEAD_DIM, DIFF_HEADS * 2 * HEAD_DIM)
    offs = np.concatenate([[0], np.cumsum(sizes)])
    seg = [np.arange(offs[n], offs[n + 1]) for n in range(len(sizes))]
    seg[2] = offs[2] + np.concatenate([np.arange(hd * HEAD_DIM, (hd + 1) * HEAD_DIM) for hd in SWA_HEAD_ORDER])
    q_perm = np.concatenate([seg[0], seg[1], seg[2], seg[3], seg[5], seg[6], seg[4], seg[7]])
    o_perm = np.arange(w_out.shape[1])
    o_perm[LRU_WIDTH:LRU_WIDTH + SWA_HEADS * HEAD_DIM] = LRU_WIDTH + np.concatenate(
        [np.arange(hd * HEAD_DIM, (hd + 1) * HEAD_DIM) for hd in SWA_HEAD_ORDER])

    for l in range(depth):
        last = l == depth - 1
        mod = mod_all[l]
        lambda_init = 0.8 - 0.6 * math.exp(-0.3 * l)
        g = norm_g[l]
        h = _ffn(h, mod, g[0:1], ffn1_w_gu[l].astype(BF16), ffn1_w_down[l].astype(BF16), final_g[None],
                 k=0, n_tiles=n_tiles, geom=geom)
        lru_in, qk, vt = _inproj(h, mod, g[1:2], w_in[l][:, q_perm].astype(BF16), cos_t, sin_t,
                                 n_tiles=n_tiles, geom=geom)
        y_lat, y_ctx = _lru(lru_in, conv_w[l], conv_b[l][None], _block_diag(lru_w_r[l]), lru_b_r[l][:, None],
                            _block_diag(lru_w_i[l]), lru_b_i[l][:, None], lru_lambda[l][:, None], B=B, S=S, Lc=Lc)
        y_swa = _swa(qk, vt, swa_sink[l][None], B=B, S=S, Lc=Lc)
        d_lat, d_ctx = _diff(qk, vt, diff_lambda[l], diff_subln_g[l][None], B=B, S=S, Lc=Lc,
                             lambda_init=lambda_init)
        n_out = n_lat_tiles if last else n_tiles
        h = _outproj(h, mod, y_lat, y_ctx, y_swa, d_lat, d_ctx, w_out[l][o_perm, :].astype(BF16),
                     n_tiles=n_out, geom=geom)
        h = _ffn(h, mod, g[2:3], ffn2_w_gu[l].astype(BF16), ffn2_w_down[l].astype(BF16), final_g[None],
                 k=2, n_tiles=n_out, geom=geom, final_norm=last)
    return h[:B * S].reshape(B, S, D)
```

```python
import functools
import math

import numpy as np
import jax
import jax.numpy as jnp
from jax import lax
from jax.experimental import pallas as pl
from jax.experimental.pallas import tpu as pltpu

F32 = jnp.float32
BF16 = jnp.bfloat16

GRID_W = 64
N_MOD = 9
EPS = 1e-6
FFN_RES = 0.5
HEAD_DIM = 64
ROPE_BASE = 10000.0
LRU_WIDTH = 256
LRU_BLOCKS = 4
CONV_WIDTH = 4
CONV_LEFT = 2
LRU_C = 8.0
SWA_HEADS = 4
SWA_KV_HEADS = 2
WINDOW = 128
DIFF_HEADS = 4
NEG_INF = -1e30

LANES = 128
SUBLANES = 8
VMEM_LIMIT_BYTES = 56 * 1024 * 1024

TOKEN_TILE = 512
FFN_CHUNKS = 1
ADA_COLS = 1024
ATT_TQ = 256
LRU_CHUNK = 256

QK_WIDTH = 1408
COL_SQ, COL_SK, COL_DQ, COL_DK = 0, 256, 384, 896
VT_ROWS = 640
ROW_SV, ROW_DV = 0, 128
LOG2E = math.log2(math.e)
DIFF_ROWS = 16
DIFF_CHUNK = 1024
ONES_ROWS = 16
SWA_HEAD_ORDER = (0, 2, 1, 3)


def _cparams(semantics):
    return pltpu.CompilerParams(dimension_semantics=semantics, vmem_limit_bytes=VMEM_LIMIT_BYTES)


def _resident(shape):
    nd = len(shape)
    return pl.BlockSpec(shape, lambda *_: (0,) * nd, pipeline_mode=pl.Buffered(1))


def _dot(a, b):
    return jnp.dot(a, b, preferred_element_type=F32)


def _dot_nt(a, b):
    return lax.dot_general(a, b, (((1,), (1,)), ((), ())), preferred_element_type=F32)


def _rms(x, g):
    return x * lax.rsqrt(jnp.mean(x * x, axis=-1, keepdims=True) + EPS) * g


def _modnorm(h, g, mod_ref, k):
    shift = mod_ref[0, 3 * k:3 * k + 1, :]
    scale = mod_ref[0, 3 * k + 1:3 * k + 2, :]
    return _rms(h, g) * (1.0 + scale) + shift


def _expm1_of_square(a, t):
    u = a * a
    near_one = u == 1.0
    stable = (u - 1.0) * t / jnp.where(near_one, 1.0, jnp.log(u))
    return jnp.where(u < 0.5, u - 1.0, jnp.where(near_one, t, stable))


def _ada_kernel(c_ref, w_ref, b_ref, o_ref):
    c = c_ref[...]
    s = (c * jax.nn.sigmoid(c)).astype(BF16)
    o_ref[0] = _dot(s, w_ref[0].astype(BF16)) + b_ref[0]


def _ada(cond, w_ada, b_ada):
    L, D, N = w_ada.shape
    R = cond.shape[0]
    return pl.pallas_call(
        _ada_kernel,
        grid=(L, N // ADA_COLS),
        in_specs=[
            pl.BlockSpec((R, D), lambda l, n: (0, 0)),
            pl.BlockSpec((1, D, ADA_COLS), lambda l, n: (l, 0, n)),
            pl.BlockSpec((1, 1, ADA_COLS), lambda l, n: (l, 0, n)),
        ],
        out_specs=pl.BlockSpec((1, R, ADA_COLS), lambda l, n: (l, 0, n)),
        out_shape=jax.ShapeDtypeStruct((L, R, N), F32),
        compiler_params=_cparams(("arbitrary", "arbitrary")),
        name="ada",
    )(cond, w_ada, b_ada.reshape(L, 1, N))


def _mod_index(n_lat_tiles, tiles_per_batch, n_batch):
    def idx(i):
        return (jnp.where(i < n_lat_tiles, i // tiles_per_batch, n_batch), 0, 0)
    return idx


def _ffn_kernel(*refs, k, d_ff, final_norm, n_lat_tiles, split_input):
    if split_input:
        hl_ref, hc_ref, mod_ref, g_ref, wgu_ref, wd_ref, fg_ref, o_ref = refs
        h = jnp.where(pl.program_id(0) < n_lat_tiles, hl_ref[...], hc_ref[...])
    else:
        h_ref, mod_ref, g_ref, wgu_ref, wd_ref, fg_ref, o_ref = refs
        h = h_ref[...]
    xn = _modnorm(h, g_ref[...], mod_ref, k).astype(BF16)
    tf = d_ff // FFN_CHUNKS
    acc = None
    for c in range(FFN_CHUNKS):
        lo = c * tf
        g = _dot(xn, wgu_ref[:, lo:lo + tf])
        u = _dot(xn, wgu_ref[:, d_ff + lo:d_ff + lo + tf])
        a = (g * jax.nn.sigmoid(g) * u).astype(BF16)
        part = _dot(a, wd_ref[lo:lo + tf, :])
        acc = part if acc is None else acc + part
    gate = mod_ref[0, 3 * k + 2:3 * k + 3, :]
    out = h + (FFN_RES * gate) * acc
    if final_norm:
        out = _rms(out, fg_ref[...])
    o_ref[...] = out


def _ffn(h, mod, g, w_gu, w_down, final_g, *, k, n_tiles, geom, final_norm=False):
    split = isinstance(h, tuple)
    hs = h if split else (h,)
    D = hs[0].shape[1]
    d_ff = w_down.shape[0]
    tm = TOKEN_TILE
    n_lat_tiles = geom[0]
    if split:
        h_specs = [pl.BlockSpec((tm, D), lambda i: (jnp.minimum(i, n_lat_tiles - 1), 0)),
                   pl.BlockSpec((tm, D), lambda i: (jnp.maximum(i - n_lat_tiles, 0), 0))]
    else:
        h_specs = [pl.BlockSpec((tm, D), lambda i: (i, 0))]
    return pl.pallas_call(
        functools.partial(_ffn_kernel, k=k, d_ff=d_ff, final_norm=final_norm, n_lat_tiles=n_lat_tiles,
                          split_input=split),
        grid=(n_tiles,),
        in_specs=h_specs + [
            pl.BlockSpec((1, N_MOD, D), _mod_index(*geom)),
            _resident((1, D)),
            _resident((D, 2 * d_ff)),
            _resident((d_ff, D)),
            _resident((1, D)),
        ],
        out_specs=pl.BlockSpec((tm, D), lambda i: (i, 0)),
        out_shape=jax.ShapeDtypeStruct((n_tiles * tm, D), F32),
        compiler_params=_cparams(("arbitrary",)),
        name="ffn",
    )(*hs, mod, g, w_gu, w_down, final_g)


def _rope(x, cos, sin):
    lane = lax.broadcasted_iota(jnp.int32, x.shape, 1)
    first = (lane & (HEAD_DIM - 1)) < (HEAD_DIM // 2)
    partner = jnp.where(first, pltpu.roll(x, LANES - HEAD_DIM // 2, 1), pltpu.roll(x, HEAD_DIM // 2, 1))
    return x * cos + partner * sin


def _inproj_kernel(h_ref, mod_ref, g_ref, w_ref, cos_ref, sin_ref, lru_ref, qk_ref, vt_ref):
    xn = _modnorm(h_ref[...], g_ref[...], mod_ref, 1).astype(BF16)
    y = _dot(xn, w_ref[...])
    lw = 2 * LRU_WIDTH
    lru_ref[...] = y[:, :lw]
    cos = cos_ref[...]
    sin = sin_ref[...]
    q_scale = HEAD_DIM ** -0.5 * LOG2E
    for c0 in range(0, QK_WIDTH, LANES):
        blk = _rope(y[:, lw + c0:lw + c0 + LANES], cos, sin)
        if (COL_SQ <= c0 < COL_SK) or (COL_DQ <= c0 < COL_DK):
            blk = blk * q_scale
        qk_ref[:, c0:c0 + LANES] = blk.astype(BF16)
    vt_ref[...] = y[:, lw + QK_WIDTH:lw + QK_WIDTH + VT_ROWS].T.astype(BF16)


def _inproj(h, mod, g, w_in, cos_t, sin_t, *, n_tiles, geom):
    T, D = h.shape
    tm = TOKEN_TILE
    n_lat_tiles, tiles_per_batch, _ = geom

    def tab_idx(i):
        return (jnp.where(i < n_lat_tiles, i % tiles_per_batch, tiles_per_batch), 0)

    return pl.pallas_call(
        _inproj_kernel,
        grid=(n_tiles,),
        in_specs=[
            pl.BlockSpec((tm, D), lambda i: (i, 0)),
            pl.BlockSpec((1, N_MOD, D), _mod_index(*geom)),
            _resident((1, D)),
            _resident(w_in.shape),
            pl.BlockSpec((tm, LANES), tab_idx),
            pl.BlockSpec((tm, LANES), tab_idx),
        ],
        out_specs=[
            pl.BlockSpec((tm, 2 * LRU_WIDTH), lambda i: (i, 0)),
            pl.BlockSpec((tm, QK_WIDTH), lambda i: (i, 0)),
            pl.BlockSpec((VT_ROWS, tm), lambda i: (0, i)),
        ],
        out_shape=[
            jax.ShapeDtypeStruct((T, 2 * LRU_WIDTH), F32),
            jax.ShapeDtypeStruct((T, QK_WIDTH), BF16),
            jax.ShapeDtypeStruct((VT_ROWS, T), BF16),
        ],
        compiler_params=_cparams(("arbitrary",)),
        name="inproj",
    )(h, mod, g, w_in, cos_t, sin_t)


def _outproj_kernel(h_ref, mod_ref, lru_lat, lru_ctx, yswa_ref, diff_lat, diff_ctx, w_ref, o_ref,
                    *, n_lat_tiles, has_ctx):
    def emit(lru_ref, diff_ref):
        y = jnp.concatenate([lru_ref[...], yswa_ref[...], diff_ref[...]], axis=1)
        gate = mod_ref[0, 5:6, :]
        o_ref[...] = h_ref[...] + gate * _dot(y, w_ref[...])

    if not has_ctx:
        emit(lru_lat, diff_lat)
        return

    @pl.when(pl.program_id(0) < n_lat_tiles)
    def _():
        emit(lru_lat, diff_lat)

    @pl.when(pl.program_id(0) >= n_lat_tiles)
    def _():
        emit(lru_ctx, diff_ctx)


def _outproj(h, mod, lru_lat, lru_ctx, y_swa, diff_lat, diff_ctx, w_out, *, n_tiles, geom):
    T, D = h.shape
    tm = TOKEN_TILE
    n_lat_tiles = geom[0]
    has_ctx = n_tiles > n_lat_tiles
    dw = DIFF_HEADS * 2 * HEAD_DIM

    def lat_idx(i):
        return (jnp.minimum(i, n_lat_tiles - 1), 0)

    def ctx_idx(i):
        return (jnp.maximum(i - n_lat_tiles, 0), 0)

    return pl.pallas_call(
        functools.partial(_outproj_kernel, n_lat_tiles=n_lat_tiles, has_ctx=has_ctx),
        grid=(n_tiles,),
        in_specs=[
            pl.BlockSpec((tm, D), lambda i: (i, 0)),
            pl.BlockSpec((1, N_MOD, D), _mod_index(*geom)),
            pl.BlockSpec((tm, LRU_WIDTH), lat_idx),
            pl.BlockSpec((tm, LRU_WIDTH), ctx_idx),
            pl.BlockSpec((tm, SWA_HEADS * HEAD_DIM), lambda i: (i, 0)),
            pl.BlockSpec((tm, dw), lat_idx),
            pl.BlockSpec((tm, dw), ctx_idx),
            _resident(w_out.shape),
        ],
        out_specs=pl.BlockSpec((tm, D), lambda i: (i, 0)),
        out_shape=jax.ShapeDtypeStruct((n_tiles * tm, D), F32),
        compiler_params=_cparams(("arbitrary",)),
        name="outproj",
    )(h, mod, lru_lat, lru_ctx, y_swa, diff_lat, diff_ctx, w_out)


def _lru_kernel(lat_ref, ctx_ref, cw_ref, cb_ref, wr_ref, br_ref, wi_ref, bi_ref, lam_ref,
                ylat_ref, yctx_ref, xpad, u_scr, hf_scr, ge_scr, *g_scr, S, Lc):
    W = LRU_WIDTH
    TC = LRU_CHUNK
    PAD = SUBLANES
    row = lax.broadcasted_iota(jnp.int32, (TC, W), 0)
    group = lax.broadcasted_iota(jnp.int32, (TC // SUBLANES, W), 0)

    def conv_chunk(start):
        a = xpad[pl.ds(start, TC + 2 * PAD), :]
        u = cb_ref[...] + jnp.zeros((TC, W), F32)
        for k in range(CONV_WIDTH):
            sh = (CONV_LEFT - k) % (TC + 2 * PAD)
            r = a if sh == 0 else pltpu.roll(a, sh, 0)
            u = u + r[PAD:PAD + TC] * cw_ref[k:k + 1, :]
        return u

    def scan_chunk(u, d, carry, reverse):
        r = jax.nn.sigmoid(_dot(u, wr_ref[d]) + br_ref[d])
        i = jax.nn.sigmoid(_dot(u, wi_ref[d]) + bi_ref[d])
        log_a = (-LRU_C) * r * jax.nn.softplus(-lam_ref[d])
        A = jnp.exp(log_a)
        Bv = jnp.sqrt(-_expm1_of_square(A, 2.0 * log_a)) * i * u

        def combine(A, Bv, pos, axis):
            n = A.shape[axis]
            s = 1
            while s < n:
                if reverse:
                    keep = pos < n - s
                    sh = n - s
                else:
                    keep = pos >= s
                    sh = s
                a_sh = jnp.where(keep, pltpu.roll(A, sh, axis), 1.0)
                b_sh = jnp.where(keep, pltpu.roll(Bv, sh, axis), 0.0)
                Bv = A * b_sh + Bv
                A = A * a_sh
                s *= 2
            return A, Bv

        G = SUBLANES
        ng = TC // G
        A, Bv = combine(A.reshape(ng, G, W), Bv.reshape(ng, G, W), lax.broadcasted_iota(jnp.int32, (ng, G, W), 1), 1)
        A = A.reshape(TC, W)
        Bv = Bv.reshape(TC, W)
        edge = 0 if reverse else G - 1

        def boundary_rows(planes, x):
            out = []
            for c, scr in enumerate(planes):
                scr[...] = x[:, c * LANES:(c + 1) * LANES]
                out.append(scr[pl.ds(edge, ng, stride=G), :])
            return jnp.concatenate(out, axis=1)

        n_pl = W // LANES
        Ag, Bg = combine(boundary_rows(g_scr[:n_pl], A), boundary_rows(g_scr[n_pl:], Bv), group, 0)
        ge_scr[...] = Ag * carry + Bg
        parts = []
        for g in range(ng):
            nb_g = g + 1 if reverse else g - 1
            h_in = carry if (nb_g < 0 or nb_g >= ng) else ge_scr[nb_g:nb_g + 1, :]
            parts.append(A[g * G:(g + 1) * G] * h_in + Bv[g * G:(g + 1) * G])
        h = jnp.concatenate(parts, axis=0)
        return h, (h[0:1] if reverse else h[TC - 1:TC])

    def gelu(x):
        return jax.nn.gelu(x)

    zero_pad = jnp.zeros((PAD, W), F32)
    zero_state = jnp.zeros((1, W), F32)

    xpad[0:PAD, :] = zero_pad
    xpad[PAD:PAD + Lc, :] = ctx_ref[:, 0:W]
    xpad[PAD + Lc:2 * PAD + Lc, :] = zero_pad
    uc = conv_chunk(0)
    hcf, carry_f = scan_chunk(uc, 0, zero_state, False)
    hcb, carry_b = scan_chunk(uc, 1, zero_state, True)
    yctx_ref[...] = ((hcf + hcb) * gelu(ctx_ref[:, W:2 * W])).astype(BF16)

    xpad[PAD:PAD + S, :] = lat_ref[:, 0:W]
    xpad[PAD + S:2 * PAD + S, :] = zero_pad
    nc = S // TC

    def conv_body(c, _):
        st = pl.multiple_of(c * TC, TC)
        u_scr[pl.ds(st, TC), :] = conv_chunk(st)
        return 0

    lax.fori_loop(0, nc, conv_body, 0)

    def fwd_body(c, carry):
        st = pl.multiple_of(c * TC, TC)
        h, carry = scan_chunk(u_scr[pl.ds(st, TC), :], 0, carry, False)
        hf_scr[pl.ds(st, TC), :] = h
        return carry

    lax.fori_loop(0, nc, fwd_body, carry_f)

    def bwd_body(c, carry):
        st = pl.multiple_of((nc - 1 - c) * TC, TC)
        h, carry = scan_chunk(u_scr[pl.ds(st, TC), :], 1, carry, True)
        y = (hf_scr[pl.ds(st, TC), :] + h) * gelu(lat_ref[pl.ds(st, TC), W:2 * W])
        ylat_ref[pl.ds(st, TC), :] = y.astype(BF16)
        return carry

    lax.fori_loop(0, nc, bwd_body, carry_b)


def _lru(lru_in, conv_w, conv_b, w_r, b_r, w_i, b_i, lam, *, B, S, Lc):
    W = LRU_WIDTH
    t_lat = B * S
    ctx_blk0 = t_lat // Lc
    return pl.pallas_call(
        functools.partial(_lru_kernel, S=S, Lc=Lc),
        grid=(B,),
        in_specs=[
            pl.BlockSpec((S, 2 * W), lambda b: (b, 0)),
            pl.BlockSpec((Lc, 2 * W), lambda b: (ctx_blk0 + b, 0)),
            _resident(conv_w.shape),
            _resident(conv_b.shape),
            _resident(w_r.shape),
            _resident(b_r.shape),
            _resident(w_i.shape),
            _resident(b_i.shape),
            _resident(lam.shape),
        ],
        out_specs=[
            pl.BlockSpec((S, W), lambda b: (b, 0)),
            pl.BlockSpec((Lc, W), lambda b: (b, 0)),
        ],
        out_shape=[
            jax.ShapeDtypeStruct((t_lat, W), BF16),
            jax.ShapeDtypeStruct((B * Lc, W), BF16),
        ],
        scratch_shapes=[
            pltpu.VMEM((S + 2 * SUBLANES, W), F32),
            pltpu.VMEM((S, W), F32),
            pltpu.VMEM((S, W), F32),
            pltpu.VMEM((LRU_CHUNK // SUBLANES, W), F32),
        ] + [pltpu.VMEM((LRU_CHUNK, LANES), F32)] * (2 * (W // LANES)) + [
        ],
        compiler_params=_cparams(("arbitrary",)),
        name="lru",
    )(lru_in, lru_in, conv_w, conv_b, w_r, b_r, w_i, b_i, lam)


def _swa_kernel(sink_ref, q_ref, k_ref, vt0_ref, vt1_ref, vt2_ref, vt3_ref, kc_ref, vtc_ref, o_ref, *, nq, S):
    TQ = ATT_TQ
    KB = TQ + 2 * WINDOW
    hd = HEAD_DIM
    j = pl.program_id(1)
    q = q_ref[...]
    low = lax.broadcasted_iota(jnp.int32, (TQ, LANES), 1) < hd
    zero = jnp.zeros((TQ, LANES), BF16)
    rows = []
    for grp in range(2):
        qg = q[:, grp * LANES:(grp + 1) * LANES]
        rows += [jnp.where(low, qg, zero), jnp.where(low, zero, qg)]
    q4 = jnp.concatenate(rows, axis=0)
    col = lax.broadcasted_iota(jnp.int32, (1, SWA_HEADS * TQ), 1)
    sink = jnp.full((1, SWA_HEADS * TQ), sink_ref[0, SWA_HEAD_ORDER[-1]] * LOG2E, F32)
    for c in range(SWA_HEADS - 2, -1, -1):
        sink = jnp.where(col < (c + 1) * TQ, sink_ref[0, SWA_HEAD_ORDER[c]] * LOG2E, sink)

    def run(band):
        s = _dot_nt(kc_ref[...], q4)
        vt = vtc_ref[...]
        if band is not None:
            kb, vtb, valid = band
            s_b = _dot_nt(kb, q4)
            s_b = jnp.concatenate(
                [jnp.where(valid, s_b[:, c * TQ:(c + 1) * TQ], NEG_INF) for c in range(SWA_HEADS)], axis=1)
            s = jnp.concatenate([s_b, s], axis=0)
            vt = jnp.concatenate([vtb, vt], axis=1)
        m = jnp.maximum(jnp.max(s, axis=0, keepdims=True), sink)
        p = jnp.exp2(s - m).astype(BF16)
        vt_ext = jnp.concatenate([vt, jnp.ones((ONES_ROWS, vt.shape[1]), BF16)], axis=0)
        acc = _dot(vt_ext, p)
        den = acc[2 * hd:2 * hd + 1] + jnp.exp2(sink - m)
        o_n = acc[0:2 * hd] * (1.0 / den)
        y_t = jnp.concatenate(
            [o_n[(c % 2) * hd:(c % 2 + 1) * hd, c * TQ:(c + 1) * TQ] for c in range(SWA_HEADS)], axis=0)
        o_ref[...] = y_t.T.astype(BF16)

    @pl.when(j < nq)
    def _():
        start = _band_start(j, S) * LANES
        kb = k_ref[pl.ds(pl.multiple_of(start, LANES), KB), :]
        vtb = jnp.concatenate([vt0_ref[...], vt1_ref[...], vt2_ref[...], vt3_ref[...]], axis=1)
        k_abs = start + lax.broadcasted_iota(jnp.int32, (KB, TQ), 0)
        q_abs = j * TQ + lax.broadcasted_iota(jnp.int32, (KB, TQ), 1)
        run((kb, vtb, jnp.abs(q_abs - k_abs) <= WINDOW))

    @pl.when(j == nq)
    def _():
        run(None)


def _band_start(j, S):
    return jnp.clip(2 * j - 1, 0, S // LANES - (ATT_TQ + 2 * WINDOW) // LANES)


def _swa(qk, vt, sink, *, B, S, Lc):
    T = qk.shape[0]
    TQ = ATT_TQ
    nq = S // TQ
    ctx_blk0 = B * S // Lc
    qw = SWA_HEADS * HEAD_DIM
    sv_blk = ROW_SV // LANES

    def q_idx(b, j):
        return (jnp.where(j < nq, b * nq + j, ctx_blk0 + b), COL_SQ // qw)

    def vt_chunk(d):
        return pl.BlockSpec((LANES, LANES), lambda b, j: (sv_blk, b * (S // LANES) + _band_start(j, S) + d))

    return pl.pallas_call(
        functools.partial(_swa_kernel, nq=nq, S=S),
        grid=(B, nq + 1),
        in_specs=[
            pl.BlockSpec(memory_space=pltpu.SMEM),
            pl.BlockSpec((TQ, qw), q_idx),
            pl.BlockSpec((S, LANES), lambda b, j: (b, COL_SK // LANES)),
            vt_chunk(0), vt_chunk(1), vt_chunk(2), vt_chunk(3),
            pl.BlockSpec((Lc, LANES), lambda b, j: (ctx_blk0 + b, COL_SK // LANES)),
            pl.BlockSpec((LANES, Lc), lambda b, j: (sv_blk, ctx_blk0 + b)),
        ],
        out_specs=pl.BlockSpec((TQ, qw), lambda b, j: (jnp.where(j < nq, b * nq + j, ctx_blk0 + b), 0)),
        out_shape=jax.ShapeDtypeStruct((T, qw), BF16),
        compiler_params=_cparams(("arbitrary", "arbitrary")),
        name="swa",
    )(sink, qk, qk, vt, vt, vt, vt, qk, vt)


def _diff_kernel(dl_ref, g_ref, q_ref, k_ref, vt_ref, qc_ref, kc_ref, vtc_ref, o_ref, oc_ref,
                 vt_scr, s_a, s_b, p_a, p_b, *, S, Lc, lambda_init):
    TQ = ATT_TQ
    NK = S + Lc
    dv = 2 * HEAD_DIM
    vt_scr[0:dv, 0:S] = vt_ref[...]
    vt_scr[0:dv, S:NK] = vtc_ref[...]
    vt_scr[dv:dv + ONES_ROWS, :] = jnp.ones((ONES_ROWS, NK), BF16)
    dl = dl_ref[...]
    lam = (jnp.exp(jnp.sum(dl[0:1] * dl[1:2], axis=-1, keepdims=True))
           - jnp.exp(jnp.sum(dl[2:3] * dl[3:4], axis=-1, keepdims=True)) + lambda_init)
    low = lax.broadcasted_iota(jnp.int32, (TQ, LANES), 1) < HEAD_DIM
    lat_pieces = [(r, r + DIFF_CHUNK, k_ref, r) for r in range(0, S, DIFF_CHUNK)]
    ctx_piece = (S, NK, kc_ref, 0)

    def scores(q, s_dst, lo=0):
        zero = jnp.zeros_like(q)
        q2 = jnp.concatenate([jnp.where(low, q, zero), jnp.where(low, zero, q)], axis=0)
        m_acc = None
        for r0, r1, kref, off in ([] if lo else lat_pieces) + [ctx_piece]:
            blk = _dot_nt(kref[off:off + (r1 - r0), :], q2)
            s_dst[r0:r1, :] = blk
            for r in range(0, r1 - r0, DIFF_ROWS):
                part = blk[r:r + DIFF_ROWS]
                m_acc = part if m_acc is None else jnp.maximum(m_acc, part)
        return jnp.max(m_acc, axis=0, keepdims=True)

    def probs(s_src, m, p_dst, lo=0):
        p_dst[lo:NK, :] = jnp.exp2(s_src[lo:NK, :] - m).astype(BF16)

    def attend(p_src, lo=0):
        acc = _dot(vt_scr[:, lo:NK], p_src[lo:NK, :])
        o_n = acc[0:dv] * (1.0 / acc[dv:dv + 1])
        o_t = o_n[:, 0:TQ] - lam * o_n[:, TQ:2 * TQ]
        ms = jnp.mean(o_t * o_t, axis=0, keepdims=True)
        y_t = o_t * (lax.rsqrt(ms + EPS) * (1.0 - lambda_init))
        return (y_t.T * g_ref[...]).astype(BF16)

    def q_block(i):
        return q_ref[pl.ds(pl.multiple_of(i * TQ, TQ), TQ), :]

    def emit(i, y):
        o_ref[pl.ds(pl.multiple_of(i * TQ, TQ), TQ), :] = y

    nb = S // TQ
    m_a = scores(q_block(0), s_a)
    m_b = scores(q_block(1), s_b)
    probs(s_a, m_a, p_a)

    def pair(j, m_b):
        b = 2 * j
        m_a = scores(q_block(b + 2), s_a)
        probs(s_b, m_b, p_b)
        emit(b, attend(p_a))
        m_b = scores(q_block(b + 3), s_b)
        probs(s_a, m_a, p_a)
        emit(b + 1, attend(p_b))
        return m_b

    m_b = lax.fori_loop(0, nb // 2 - 1, pair, m_b)
    probs(s_b, m_b, p_b)
    emit(nb - 2, attend(p_a))
    emit(nb - 1, attend(p_b))

    m_c = scores(qc_ref[...], s_a, S)
    probs(s_a, m_c, p_a, S)
    oc_ref[...] = attend(p_a, S)


def _diff(qk, vt, diff_lambda, subln_g, *, B, S, Lc, lambda_init):
    H = DIFF_HEADS
    dv = 2 * HEAD_DIM
    NK = S + Lc
    ctx_blk0 = B * S // Lc
    return pl.pallas_call(
        functools.partial(_diff_kernel, S=S, Lc=Lc, lambda_init=lambda_init),
        grid=(B, H),
        in_specs=[
            _resident(diff_lambda.shape),
            _resident(subln_g.shape),
            pl.BlockSpec((S, LANES), lambda b, h: (b, COL_DQ // LANES + h)),
            pl.BlockSpec((S, LANES), lambda b, h: (b, COL_DK // LANES + h)),
            pl.BlockSpec((dv, S), lambda b, h: (ROW_DV // dv + h, b)),
            pl.BlockSpec((Lc, LANES), lambda b, h: (ctx_blk0 + b, COL_DQ // LANES + h)),
            pl.BlockSpec((Lc, LANES), lambda b, h: (ctx_blk0 + b, COL_DK // LANES + h)),
            pl.BlockSpec((dv, Lc), lambda b, h: (ROW_DV // dv + h, ctx_blk0 + b)),
        ],
        out_specs=[
            pl.BlockSpec((S, dv), lambda b, h: (b, h)),
            pl.BlockSpec((Lc, dv), lambda b, h: (b, h)),
        ],
        out_shape=[
            jax.ShapeDtypeStruct((B * S, H * dv), BF16),
            jax.ShapeDtypeStruct((B * Lc, H * dv), BF16),
        ],
        scratch_shapes=[
            pltpu.VMEM((dv + ONES_ROWS, NK), BF16),
            pltpu.VMEM((NK, 2 * ATT_TQ), F32),
            pltpu.VMEM((NK, 2 * ATT_TQ), F32),
            pltpu.VMEM((NK, 2 * ATT_TQ), BF16),
            pltpu.VMEM((NK, 2 * ATT_TQ), BF16),
        ],
        compiler_params=_cparams(("arbitrary", "arbitrary")),
        name="diff",
    )(diff_lambda, subln_g, qk, qk, vt, qk, qk, vt)


def _rope_tables(S, tm):
    rows = S // GRID_W
    row = jnp.repeat(jnp.arange(rows, dtype=F32), GRID_W)
    col = jnp.tile(jnp.arange(GRID_W, dtype=F32), rows)
    n_freq = HEAD_DIM // 4
    inv_freq = ROPE_BASE ** (-jnp.arange(n_freq, dtype=F32) / n_freq)
    ang = jnp.concatenate([row[:, None] * inv_freq, col[:, None] * inv_freq], axis=-1)
    cos, sin = jnp.cos(ang), jnp.sin(ang)
    reps = LANES // (HEAD_DIM // 2)
    sign = np.tile(np.concatenate([-np.ones(HEAD_DIM // 2), np.ones(HEAD_DIM // 2)]), LANES // HEAD_DIM)
    cos_t = jnp.concatenate([jnp.tile(cos, (1, reps)), jnp.ones((tm, LANES), F32)], axis=0)
    sin_t = jnp.concatenate([jnp.tile(sin, (1, reps)) * sign.astype(np.float32), jnp.zeros((tm, LANES), F32)], axis=0)
    return cos_t, sin_t


def _block_diag(w):
    nd, K, c, _ = w.shape
    eye = jnp.eye(K, dtype=w.dtype)
    return jnp.einsum('dkij,kl->dkilj', w, eye).reshape(nd, K * c, K * c)


def kernel(x, c, ctx, c_ctx, w_ada, b_ada, norm_g, ffn1_w_gu, ffn1_w_down, ffn2_w_gu, ffn2_w_down,
           w_in, w_out, conv_w, conv_b, lru_w_r, lru_b_r, lru_w_i, lru_b_i, lru_lambda,
           swa_sink, diff_lambda, diff_subln_g, final_g):
    B, S, D = x.shape
    Lc = ctx.shape[1]
    depth = w_ada.shape[0]
    tm = TOKEN_TILE
    assert S % tm == 0 and (B * Lc) % tm == 0 and Lc == ATT_TQ and S % GRID_W == 0 and S % LRU_CHUNK == 0
    assert Lc == LRU_CHUNK and S >= ATT_TQ + 2 * WINDOW
    assert S % DIFF_CHUNK == 0 and (S // ATT_TQ) % 2 == 0
    n_lat_tiles = B * S // tm
    n_tiles = n_lat_tiles + B * Lc // tm
    geom = (n_lat_tiles, S // tm, B)

    h = (x.reshape(B * S, D), ctx.reshape(B * Lc, D))

    n_cond = B + 1
    pad = (-n_cond) % SUBLANES
    cond = jnp.concatenate([c, c_ctx[None], jnp.zeros((pad, D), F32)], axis=0)
    mod_all = _ada(cond, w_ada, b_ada).reshape(depth, n_cond + pad, N_MOD, D)

    cos_t, sin_t = _rope_tables(S, tm)
    sizes = (LRU_WIDTH, LRU_WIDTH, SWA_HEADS * HEAD_DIM, SWA_KV_HEADS * HEAD_DIM, SWA_KV_HEADS * HEAD_DIM,
             DIFF_HEADS * 2 * HEAD_DIM, DIFF_HEADS * 2 * HEAD_DIM, DIFF_HEADS * 2 * HEAD_DIM)
    offs = np.concatenate([[0], np.cumsum(sizes)])
    seg = [np.arange(offs[n], offs[n + 1]) for n in range(len(sizes))]
    seg[2] = offs[2] + np.concatenate([np.arange(hd * HEAD_DIM, (hd + 1) * HEAD_DIM) for hd in SWA_HEAD_ORDER])
    q_perm = np.concatenate([seg[0], seg[1], seg[2], seg[3], seg[5], seg[6], seg[4], seg[7]])
    o_perm = np.arange(w_out.shape[1])
    o_perm[LRU_WIDTH:LRU_WIDTH + SWA_HEADS * HEAD_DIM] = LRU_WIDTH + np.concatenate(
        [np.arange(hd * HEAD_DIM, (hd + 1) * HEAD_DIM) for hd in SWA_HEAD_ORDER])

    for l in range(depth):
        last = l == depth - 1
        mod = mod_all[l]
        lambda_init = 0.8 - 0.6 * math.exp(-0.3 * l)
        g = norm_g[l]
        h = _ffn(h, mod, g[0:1], ffn1_w_gu[l].astype(BF16), ffn1_w_down[l].astype(BF16), final_g[None],
                 k=0, n_tiles=n_tiles, geom=geom)
        lru_in, qk, vt = _inproj(h, mod, g[1:2], w_in[l][:, q_perm].astype(BF16), cos_t, sin_t,
                                 n_tiles=n_tiles, geom=geom)
        y_lat, y_ctx = _lru(lru_in, conv_w[l], conv_b[l][None], _block_diag(lru_w_r[l]), lru_b_r[l][:, None],
                            _block_diag(lru_w_i[l]), lru_b_i[l][:, None], lru_lambda[l][:, None], B=B, S=S, Lc=Lc)
        y_swa = _swa(qk, vt, swa_sink[l][None], B=B, S=S, Lc=Lc)
        d_lat, d_ctx = _diff(qk, vt, diff_lambda[l], diff_subln_g[l][None], B=B, S=S, Lc=Lc,
                             lambda_init=lambda_init)
        n_out = n_lat_tiles if last else n_tiles
        h = _outproj(h, mod, y_lat, y_ctx, y_swa, d_lat, d_ctx, w_out[l][o_perm, :].astype(BF16),
                     n_tiles=n_out, geom=geom)
        h = _ffn(h, mod, g[2:3], ffn2_w_gu[l].astype(BF16), ffn2_w_down[l].astype(BF16), final_g[None],
                 k=2, n_tiles=n_out, geom=geom, final_norm=last)
    return h[:B * S].reshape(B, S, D)
```

```python
import functools
import math

import numpy as np
import jax
import jax.numpy as jnp
from jax import lax
from jax.experimental import pallas as pl
from jax.experimental.pallas import tpu as pltpu

F32 = jnp.float32
BF16 = jnp.bfloat16

GRID_W = 64
N_MOD = 9
EPS = 1e-6
FFN_RES = 0.5
HEAD_DIM = 64
ROPE_BASE = 10000.0
LRU_WIDTH = 256
LRU_BLOCKS = 4
CONV_WIDTH = 4
CONV_LEFT = 2
LRU_C = 8.0
SWA_HEADS = 4
SWA_KV_HEADS = 2
WINDOW = 128
DIFF_HEADS = 4
NEG_INF = -1e30

LANES = 128
SUBLANES = 8
VMEM_LIMIT_BYTES = 56 * 1024 * 1024

TOKEN_TILE = 512
FFN_CHUNKS = 1
ADA_COLS = 1024
ATT_TQ = 256
LRU_CHUNK = 256

QK_WIDTH = 1408
COL_SQ, COL_SK, COL_DQ, COL_DK = 0, 256, 384, 896
VT_ROWS = 640
ROW_SV, ROW_DV = 0, 128
LOG2E = math.log2(math.e)
DIFF_ROWS = 16
DIFF_CHUNK = 1024
ONES_ROWS = 16
SWA_HEAD_ORDER = (0, 2, 1, 3)


def _cparams(semantics):
    return pltpu.CompilerParams(dimension_semantics=semantics, vmem_limit_bytes=VMEM_LIMIT_BYTES)


def _resident(shape):
    nd = len(shape)
    return pl.BlockSpec(shape, lambda *_: (0,) * nd, pipeline_mode=pl.Buffered(1))


def _layer_resident(stacked, layer):
    nd = stacked.ndim - 1
    return pl.BlockSpec((None,) + stacked.shape[1:], lambda *_: (layer,) + (0,) * nd, pipeline_mode=pl.Buffered(1))


def _dot(a, b):
    return jnp.dot(a, b, preferred_element_type=F32)


def _dot_nt(a, b):
    return lax.dot_general(a, b, (((1,), (1,)), ((), ())), preferred_element_type=F32)


def _rms(x, g):
    return x * lax.rsqrt(jnp.mean(x * x, axis=-1, keepdims=True) + EPS) * g


def _modnorm(h, g, mod_ref, k):
    shift = mod_ref[0, 3 * k:3 * k + 1, :]
    scale = mod_ref[0, 3 * k + 1:3 * k + 2, :]
    return _rms(h, g) * (1.0 + scale) + shift


def _expm1_of_square(a, t):
    u = a * a
    near_one = u == 1.0
    stable = (u - 1.0) * t / jnp.where(near_one, 1.0, jnp.log(u))
    return jnp.where(u < 0.5, u - 1.0, jnp.where(near_one, t, stable))


def _ada_kernel(c_ref, w_ref, b_ref, o_ref):
    c = c_ref[...]
    s = (c * jax.nn.sigmoid(c)).astype(BF16)
    o_ref[0] = _dot(s, w_ref[0].astype(BF16)) + b_ref[0]


def _ada(cond, w_ada, b_ada):
    L, D, N = w_ada.shape
    R = cond.shape[0]
    return pl.pallas_call(
        _ada_kernel,
        grid=(L, N // ADA_COLS),
        in_specs=[
            pl.BlockSpec((R, D), lambda l, n: (0, 0)),
            pl.BlockSpec((1, D, ADA_COLS), lambda l, n: (l, 0, n)),
            pl.BlockSpec((1, 1, ADA_COLS), lambda l, n: (l, 0, n)),
        ],
        out_specs=pl.BlockSpec((1, R, ADA_COLS), lambda l, n: (l, 0, n)),
        out_shape=jax.ShapeDtypeStruct((L, R, N), F32),
        compiler_params=_cparams(("arbitrary", "arbitrary")),
        name="ada",
    )(cond, w_ada, b_ada.reshape(L, 1, N))


def _mod_index(n_lat_tiles, tiles_per_batch, n_batch):
    def idx(i):
        return (jnp.where(i < n_lat_tiles, i // tiles_per_batch, n_batch), 0, 0)
    return idx


def _ffn_kernel(*refs, k, d_ff, final_norm, n_lat_tiles, split_input):
    if split_input:
        hl_ref, hc_ref, mod_ref, g_ref, wgu_ref, wd_ref, fg_ref, o_ref = refs
        h = jnp.where(pl.program_id(0) < n_lat_tiles, hl_ref[...], hc_ref[...])
    else:
        h_ref, mod_ref, g_ref, wgu_ref, wd_ref, fg_ref, o_ref = refs
        h = h_ref[...]
    xn = _modnorm(h, g_ref[...], mod_ref, k).astype(BF16)
    tf = d_ff // FFN_CHUNKS
    acc = None
    for c in range(FFN_CHUNKS):
        lo = c * tf
        g = _dot(xn, wgu_ref[:, lo:lo + tf])
        u = _dot(xn, wgu_ref[:, d_ff + lo:d_ff + lo + tf])
        a = (g * jax.nn.sigmoid(g) * u).astype(BF16)
        part = _dot(a, wd_ref[lo:lo + tf, :])
        acc = part if acc is None else acc + part
    gate = mod_ref[0, 3 * k + 2:3 * k + 3, :]
    out = h + (FFN_RES * gate) * acc
    if final_norm:
        out = _rms(out, fg_ref[...])
    o_ref[...] = out


def _ffn(h, mod, g, w_gu, w_down, final_g, *, layer, k, n_tiles, geom, final_norm=False):
    split = isinstance(h, tuple)
    hs = h if split else (h,)
    D = hs[0].shape[1]
    d_ff = w_down.shape[1]
    tm = TOKEN_TILE
    n_lat_tiles = geom[0]
    if split:
        h_specs = [pl.BlockSpec((tm, D), lambda i: (jnp.minimum(i, n_lat_tiles - 1), 0)),
                   pl.BlockSpec((tm, D), lambda i: (jnp.maximum(i - n_lat_tiles, 0), 0))]
    else:
        h_specs = [pl.BlockSpec((tm, D), lambda i: (i, 0))]
    return pl.pallas_call(
        functools.partial(_ffn_kernel, k=k, d_ff=d_ff, final_norm=final_norm, n_lat_tiles=n_lat_tiles,
                          split_input=split),
        grid=(n_tiles,),
        in_specs=h_specs + [
            pl.BlockSpec((1, N_MOD, D), _mod_index(*geom)),
            _resident((1, D)),
            _layer_resident(w_gu, layer),
            _layer_resident(w_down, layer),
            _resident((1, D)),
        ],
        out_specs=pl.BlockSpec((tm, D), lambda i: (i, 0)),
        out_shape=jax.ShapeDtypeStruct((n_tiles * tm, D), F32),
        compiler_params=_cparams(("arbitrary",)),
        name="ffn",
    )(*hs, mod, g, w_gu, w_down, final_g)


def _rope(x, cos, sin):
    lane = lax.broadcasted_iota(jnp.int32, x.shape, 1)
    first = (lane & (HEAD_DIM - 1)) < (HEAD_DIM // 2)
    partner = jnp.where(first, pltpu.roll(x, LANES - HEAD_DIM // 2, 1), pltpu.roll(x, HEAD_DIM // 2, 1))
    return x * cos + partner * sin


def _inproj_kernel(h_ref, mod_ref, g_ref, w_ref, cos_ref, sin_ref, lru_ref, qk_ref, vt_ref):
    xn = _modnorm(h_ref[...], g_ref[...], mod_ref, 1).astype(BF16)
    y = _dot(xn, w_ref[...])
    lw = 2 * LRU_WIDTH
    lru_ref[...] = y[:, :lw]
    cos = cos_ref[...]
    sin = sin_ref[...]
    q_scale = HEAD_DIM ** -0.5 * LOG2E
    for c0 in range(0, QK_WIDTH, LANES):
        blk = _rope(y[:, lw + c0:lw + c0 + LANES], cos, sin)
        if (COL_SQ <= c0 < COL_SK) or (COL_DQ <= c0 < COL_DK):
            blk = blk * q_scale
        qk_ref[:, c0:c0 + LANES] = blk.astype(BF16)
    vt_ref[...] = y[:, lw + QK_WIDTH:lw + QK_WIDTH + VT_ROWS].T.astype(BF16)


def _inproj(h, mod, g, w_in, cos_t, sin_t, *, layer, n_tiles, geom):
    T, D = h.shape
    tm = TOKEN_TILE
    n_lat_tiles, tiles_per_batch, _ = geom

    def tab_idx(i):
        return (jnp.where(i < n_lat_tiles, i % tiles_per_batch, tiles_per_batch), 0)

    return pl.pallas_call(
        _inproj_kernel,
        grid=(n_tiles,),
        in_specs=[
            pl.BlockSpec((tm, D), lambda i: (i, 0)),
            pl.BlockSpec((1, N_MOD, D), _mod_index(*geom)),
            _resident((1, D)),
            _layer_resident(w_in, layer),
            pl.BlockSpec((tm, LANES), tab_idx),
            pl.BlockSpec((tm, LANES), tab_idx),
        ],
        out_specs=[
            pl.BlockSpec((tm, 2 * LRU_WIDTH), lambda i: (i, 0)),
            pl.BlockSpec((tm, QK_WIDTH), lambda i: (i, 0)),
            pl.BlockSpec((VT_ROWS, tm), lambda i: (0, i)),
        ],
        out_shape=[
            jax.ShapeDtypeStruct((T, 2 * LRU_WIDTH), F32),
            jax.ShapeDtypeStruct((T, QK_WIDTH), BF16),
            jax.ShapeDtypeStruct((VT_ROWS, T), BF16),
        ],
        compiler_params=_cparams(("arbitrary",)),
        name="inproj",
    )(h, mod, g, w_in, cos_t, sin_t)


def _outproj_kernel(h_ref, mod_ref, lru_lat, lru_ctx, swa_lat, swa_ctx, diff_lat, diff_ctx, w_ref, o_ref,
                    *, n_lat_tiles, has_ctx):
    def emit(lru_ref, swa_ref, diff_ref):
        y = jnp.concatenate([lru_ref[...], swa_ref[...], diff_ref[...]], axis=1)
        gate = mod_ref[0, 5:6, :]
        o_ref[...] = h_ref[...] + gate * _dot(y, w_ref[...])

    if not has_ctx:
        emit(lru_lat, swa_lat, diff_lat)
        return

    @pl.when(pl.program_id(0) < n_lat_tiles)
    def _():
        emit(lru_lat, swa_lat, diff_lat)

    @pl.when(pl.program_id(0) >= n_lat_tiles)
    def _():
        emit(lru_ctx, swa_ctx, diff_ctx)


def _outproj(h, mod, lru, swa, diff, w_out, *, layer, n_tiles, geom):
    T, D = h.shape
    tm = TOKEN_TILE
    n_lat_tiles = geom[0]
    has_ctx = n_tiles > n_lat_tiles

    def lat_idx(i):
        return (jnp.minimum(i, n_lat_tiles - 1), 0)

    def ctx_idx(i):
        return (jnp.maximum(i - n_lat_tiles, 0), 0)

    mix_specs = []
    for lat, _ in (lru, swa, diff):
        mix_specs += [pl.BlockSpec((tm, lat.shape[1]), lat_idx), pl.BlockSpec((tm, lat.shape[1]), ctx_idx)]
    return pl.pallas_call(
        functools.partial(_outproj_kernel, n_lat_tiles=n_lat_tiles, has_ctx=has_ctx),
        grid=(n_tiles,),
        in_specs=[
            pl.BlockSpec((tm, D), lambda i: (i, 0)),
            pl.BlockSpec((1, N_MOD, D), _mod_index(*geom)),
        ] + mix_specs + [_layer_resident(w_out, layer)],
        out_specs=pl.BlockSpec((tm, D), lambda i: (i, 0)),
        out_shape=jax.ShapeDtypeStruct((n_tiles * tm, D), F32),
        compiler_params=_cparams(("arbitrary",)),
        name="outproj",
    )(h, mod, *lru, *swa, *diff, w_out)


def _lru_kernel(lat_ref, ctx_ref, cw_ref, cb_ref, wr_ref, br_ref, wi_ref, bi_ref, lam_ref,
                ylat_ref, yctx_ref, xpad, u_scr, hf_scr, ge_scr, *g_scr, S, Lc):
    W = LRU_WIDTH
    TC = LRU_CHUNK
    PAD = SUBLANES
    row = lax.broadcasted_iota(jnp.int32, (TC, W), 0)
    group = lax.broadcasted_iota(jnp.int32, (TC // SUBLANES, W), 0)

    def conv_chunk(start):
        a = xpad[pl.ds(start, TC + 2 * PAD), :]
        u = cb_ref[...] + jnp.zeros((TC, W), F32)
        for k in range(CONV_WIDTH):
            sh = (CONV_LEFT - k) % (TC + 2 * PAD)
            r = a if sh == 0 else pltpu.roll(a, sh, 0)
            u = u + r[PAD:PAD + TC] * cw_ref[k:k + 1, :]
        return u

    def scan_chunk(u, d, carry, reverse):
        r = jax.nn.sigmoid(_dot(u, wr_ref[d]) + br_ref[d])
        i = jax.nn.sigmoid(_dot(u, wi_ref[d]) + bi_ref[d])
        log_a = (-LRU_C) * r * jax.nn.softplus(-lam_ref[d])
        A = jnp.exp(log_a)
        Bv = jnp.sqrt(-_expm1_of_square(A, 2.0 * log_a)) * i * u

        def combine(A, Bv, pos, axis):
            n = A.shape[axis]
            s = 1
            while s < n:
                if reverse:
                    keep = pos < n - s
                    sh = n - s
                else:
                    keep = pos >= s
                    sh = s
                a_sh = jnp.where(keep, pltpu.roll(A, sh, axis), 1.0)
                b_sh = jnp.where(keep, pltpu.roll(Bv, sh, axis), 0.0)
                Bv = A * b_sh + Bv
                A = A * a_sh
                s *= 2
            return A, Bv

        G = SUBLANES
        ng = TC // G
        A, Bv = combine(A.reshape(ng, G, W), Bv.reshape(ng, G, W), lax.broadcasted_iota(jnp.int32, (ng, G, W), 1), 1)
        A = A.reshape(TC, W)
        Bv = Bv.reshape(TC, W)
        edge = 0 if reverse else G - 1

        def boundary_rows(planes, x):
            out = []
            for c, scr in enumerate(planes):
                scr[...] = x[:, c * LANES:(c + 1) * LANES]
                out.append(scr[pl.ds(edge, ng, stride=G), :])
            return jnp.concatenate(out, axis=1)

        n_pl = W // LANES
        Ag, Bg = combine(boundary_rows(g_scr[:n_pl], A), boundary_rows(g_scr[n_pl:], Bv), group, 0)
        ge_scr[...] = Ag * carry + Bg
        parts = []
        for g in range(ng):
            nb_g = g + 1 if reverse else g - 1
            h_in = carry if (nb_g < 0 or nb_g >= ng) else ge_scr[nb_g:nb_g + 1, :]
            parts.append(A[g * G:(g + 1) * G] * h_in + Bv[g * G:(g + 1) * G])
        h = jnp.concatenate(parts, axis=0)
        return h, (h[0:1] if reverse else h[TC - 1:TC])

    def gelu(x):
        return jax.nn.gelu(x)

    zero_pad = jnp.zeros((PAD, W), F32)
    zero_state = jnp.zeros((1, W), F32)

    xpad[0:PAD, :] = zero_pad
    xpad[PAD:PAD + Lc, :] = ctx_ref[:, 0:W]
    xpad[PAD + Lc:2 * PAD + Lc, :] = zero_pad
    uc = conv_chunk(0)
    hcf, carry_f = scan_chunk(uc, 0, zero_state, False)
    hcb, carry_b = scan_chunk(uc, 1, zero_state, True)
    yctx_ref[...] = ((hcf + hcb) * gelu(ctx_ref[:, W:2 * W])).astype(BF16)

    xpad[PAD:PAD + S, :] = lat_ref[:, 0:W]
    xpad[PAD + S:2 * PAD + S, :] = zero_pad
    nc = S // TC

    def conv_body(c, _):
        st = pl.multiple_of(c * TC, TC)
        u_scr[pl.ds(st, TC), :] = conv_chunk(st)
        return 0

    lax.fori_loop(0, nc, conv_body, 0)

    def fwd_body(c, carry):
        st = pl.multiple_of(c * TC, TC)
        h, carry = scan_chunk(u_scr[pl.ds(st, TC), :], 0, carry, False)
        hf_scr[pl.ds(st, TC), :] = h
        return carry

    lax.fori_loop(0, nc, fwd_body, carry_f)

    def bwd_body(c, carry):
        st = pl.multiple_of((nc - 1 - c) * TC, TC)
        h, carry = scan_chunk(u_scr[pl.ds(st, TC), :], 1, carry, True)
        y = (hf_scr[pl.ds(st, TC), :] + h) * gelu(lat_ref[pl.ds(st, TC), W:2 * W])
        ylat_ref[pl.ds(st, TC), :] = y.astype(BF16)
        return carry

    lax.fori_loop(0, nc, bwd_body, carry_b)


def _lru(lru_in, conv_w, conv_b, w_r, b_r, w_i, b_i, lam, *, B, S, Lc):
    W = LRU_WIDTH
    t_lat = B * S
    ctx_blk0 = t_lat // Lc
    return pl.pallas_call(
        functools.partial(_lru_kernel, S=S, Lc=Lc),
        grid=(B,),
        in_specs=[
            pl.BlockSpec((S, 2 * W), lambda b: (b, 0)),
            pl.BlockSpec((Lc, 2 * W), lambda b: (ctx_blk0 + b, 0)),
            _resident(conv_w.shape),
            _resident(conv_b.shape),
            _resident(w_r.shape),
            _resident(b_r.shape),
            _resident(w_i.shape),
            _resident(b_i.shape),
            _resident(lam.shape),
        ],
        out_specs=[
            pl.BlockSpec((S, W), lambda b: (b, 0)),
            pl.BlockSpec((Lc, W), lambda b: (b, 0)),
        ],
        out_shape=[
            jax.ShapeDtypeStruct((t_lat, W), BF16),
            jax.ShapeDtypeStruct((B * Lc, W), BF16),
        ],
        scratch_shapes=[
            pltpu.VMEM((S + 2 * SUBLANES, W), F32),
            pltpu.VMEM((S, W), F32),
            pltpu.VMEM((S, W), F32),
            pltpu.VMEM((LRU_CHUNK // SUBLANES, W), F32),
        ] + [pltpu.VMEM((LRU_CHUNK, LANES), F32)] * (2 * (W // LANES)) + [
        ],
        compiler_params=_cparams(("arbitrary",)),
        name="lru",
    )(lru_in, lru_in, conv_w, conv_b, w_r, b_r, w_i, b_i, lam)


def _swa_kernel(sink_ref, q_ref, k_ref, vt_ref, qc_ref, kc_ref, vtc_ref, o_ref, oc_ref,
                vt3, s_a, s_b, p_a, p_b, *, S, Lc):
    TQ = ATT_TQ
    KB = TQ + 2 * WINDOW
    NKB = KB + Lc
    hd = HEAD_DIM
    nh = SWA_HEADS
    n_lat_ch = S // LANES
    n_ctx_ch = Lc // LANES
    band_ch = KB // LANES
    ones = jnp.ones((ONES_ROWS, LANES), BF16)
    for c in range(n_lat_ch + n_ctx_ch):
        src = vt_ref[:, c * LANES:(c + 1) * LANES] if c < n_lat_ch else \
            vtc_ref[:, (c - n_lat_ch) * LANES:(c - n_lat_ch + 1) * LANES]
        vt3[c, 0:2 * hd, :] = src
        vt3[c, 2 * hd:2 * hd + ONES_ROWS, :] = ones
    low = lax.broadcasted_iota(jnp.int32, (TQ, LANES), 1) < hd
    col = lax.broadcasted_iota(jnp.int32, (1, nh * TQ), 1)
    sink = jnp.full((1, nh * TQ), sink_ref[0, SWA_HEAD_ORDER[-1]] * LOG2E, F32)
    for c in range(nh - 2, -1, -1):
        sink = jnp.where(col < (c + 1) * TQ, sink_ref[0, SWA_HEAD_ORDER[c]] * LOG2E, sink)

    def band_start(t):
        return jnp.clip(2 * t - 1, 0, n_lat_ch - band_ch)

    def scores(q, s_dst, t=None):
        zero = jnp.zeros((TQ, LANES), BF16)
        rows = []
        for grp in range(2):
            qg = q[:, grp * LANES:(grp + 1) * LANES]
            rows += [jnp.where(low, qg, zero), jnp.where(low, zero, qg)]
        q4 = jnp.concatenate(rows, axis=0)
        s_c = _dot_nt(kc_ref[...], q4)
        s_dst[KB:NKB, :] = s_c
        m = jnp.max(s_c, axis=0, keepdims=True)
        if t is not None:
            st = band_start(t) * LANES
            kb = k_ref[pl.ds(pl.multiple_of(st, LANES), KB), :]
            k_abs = st + lax.broadcasted_iota(jnp.int32, (KB, TQ), 0)
            q_abs = t * TQ + lax.broadcasted_iota(jnp.int32, (KB, TQ), 1)
            valid = jnp.abs(q_abs - k_abs) <= WINDOW
            s_l = _dot_nt(kb, q4)
            s_l = jnp.concatenate(
                [jnp.where(valid, s_l[:, c * TQ:(c + 1) * TQ], NEG_INF) for c in range(nh)], axis=1)
            s_dst[0:KB, :] = s_l
            m = jnp.maximum(m, jnp.max(s_l, axis=0, keepdims=True))
        return jnp.maximum(m, sink)

    def probs(s_src, m, p_dst, lo=0):
        p_dst[lo:NKB, :] = jnp.exp2(s_src[lo:NKB, :] - m).astype(BF16)

    def attend(p_src, m, t=None):
        ctx_v = [vt3[n_lat_ch + c] for c in range(n_ctx_ch)]
        if t is None:
            lo, vt = KB, jnp.concatenate(ctx_v, axis=1)
        else:
            st = band_start(t)
            lo, vt = 0, jnp.concatenate([vt3[st + d] for d in range(band_ch)] + ctx_v, axis=1)
        acc = _dot(vt, p_src[lo:NKB, :])
        den = acc[2 * hd:2 * hd + 1] + jnp.exp2(sink - m)
        o_n = acc[0:2 * hd] * (1.0 / den)
        y_t = jnp.concatenate(
            [o_n[(c % 2) * hd:(c % 2 + 1) * hd, c * TQ:(c + 1) * TQ] for c in range(nh)], axis=0)
        return y_t.T.astype(BF16)

    def q_block(i):
        return q_ref[pl.ds(pl.multiple_of(i * TQ, TQ), TQ), :]

    def emit(i, y):
        o_ref[pl.ds(pl.multiple_of(i * TQ, TQ), TQ), :] = y

    nb = S // TQ
    m_a = scores(q_block(0), s_a, 0)
    m_b = scores(q_block(1), s_b, 1)
    probs(s_a, m_a, p_a)

    def pair(j, carry):
        m_b, m_pa = carry
        b = 2 * j
        m_a = scores(q_block(b + 2), s_a, b + 2)
        probs(s_b, m_b, p_b)
        emit(b, attend(p_a, m_pa, b))
        m_b2 = scores(q_block(b + 3), s_b, b + 3)
        probs(s_a, m_a, p_a)
        emit(b + 1, attend(p_b, m_b, b + 1))
        return m_b2, m_a

    m_b, m_pa = lax.fori_loop(0, nb // 2 - 1, pair, (m_b, m_a))
    probs(s_b, m_b, p_b)
    emit(nb - 2, attend(p_a, m_pa, nb - 2))
    emit(nb - 1, attend(p_b, m_b, nb - 1))

    m_c = scores(qc_ref[...], s_a)
    probs(s_a, m_c, p_a, KB)
    oc_ref[...] = attend(p_a, m_c)


def _swa(qk, vt, sink, *, B, S, Lc):
    qw = SWA_HEADS * HEAD_DIM
    ctx_blk0 = B * S // Lc
    sv_blk = ROW_SV // LANES
    NKB = ATT_TQ + 2 * WINDOW + Lc
    return pl.pallas_call(
        functools.partial(_swa_kernel, S=S, Lc=Lc),
        grid=(B,),
        in_specs=[
            pl.BlockSpec(memory_space=pltpu.SMEM),
            pl.BlockSpec((S, qw), lambda b: (b, COL_SQ // qw)),
            pl.BlockSpec((S, LANES), lambda b: (b, COL_SK // LANES)),
            pl.BlockSpec((LANES, S), lambda b: (sv_blk, b)),
            pl.BlockSpec((Lc, qw), lambda b: (ctx_blk0 + b, COL_SQ // qw)),
            pl.BlockSpec((Lc, LANES), lambda b: (ctx_blk0 + b, COL_SK // LANES)),
            pl.BlockSpec((LANES, Lc), lambda b: (sv_blk, ctx_blk0 + b)),
        ],
        out_specs=[
            pl.BlockSpec((S, qw), lambda b: (b, 0)),
            pl.BlockSpec((Lc, qw), lambda b: (b, 0)),
        ],
        out_shape=[
            jax.ShapeDtypeStruct((B * S, qw), BF16),
            jax.ShapeDtypeStruct((B * Lc, qw), BF16),
        ],
        scratch_shapes=[
            pltpu.VMEM(((S + Lc) // LANES, 2 * HEAD_DIM + ONES_ROWS, LANES), BF16),
            pltpu.VMEM((NKB, SWA_HEADS * ATT_TQ), F32),
            pltpu.VMEM((NKB, SWA_HEADS * ATT_TQ), F32),
            pltpu.VMEM((NKB, SWA_HEADS * ATT_TQ), BF16),
            pltpu.VMEM((NKB, SWA_HEADS * ATT_TQ), BF16),
        ],
        compiler_params=_cparams(("arbitrary",)),
        name="swa",
    )(sink, qk, qk, vt, qk, qk, vt)


def _diff_kernel(dl_ref, g_ref, q_ref, k_ref, vt_ref, qc_ref, kc_ref, vtc_ref, o_ref, oc_ref,
                 vt_scr, s_a, s_b, p_a, p_b, *, S, Lc, lambda_init):
    TQ = ATT_TQ
    NK = S + Lc
    dv = 2 * HEAD_DIM
    vt_scr[0:dv, 0:S] = vt_ref[...]
    vt_scr[0:dv, S:NK] = vtc_ref[...]
    vt_scr[dv:dv + ONES_ROWS, :] = jnp.ones((ONES_ROWS, NK), BF16)
    dl = dl_ref[...]
    lam = (jnp.exp(jnp.sum(dl[0:1] * dl[1:2], axis=-1, keepdims=True))
           - jnp.exp(jnp.sum(dl[2:3] * dl[3:4], axis=-1, keepdims=True)) + lambda_init)
    low = lax.broadcasted_iota(jnp.int32, (TQ, LANES), 1) < HEAD_DIM
    lat_pieces = [(r, r + DIFF_CHUNK, k_ref, r) for r in range(0, S, DIFF_CHUNK)]
    ctx_piece = (S, NK, kc_ref, 0)

    def scores(q, s_dst, lo=0):
        zero = jnp.zeros_like(q)
        q2 = jnp.concatenate([jnp.where(low, q, zero), jnp.where(low, zero, q)], axis=0)
        m_acc = None
        for r0, r1, kref, off in ([] if lo else lat_pieces) + [ctx_piece]:
            blk = _dot_nt(kref[off:off + (r1 - r0), :], q2)
            s_dst[r0:r1, :] = blk
            for r in range(0, r1 - r0, DIFF_ROWS):
                part = blk[r:r + DIFF_ROWS]
                m_acc = part if m_acc is None else jnp.maximum(m_acc, part)
        return jnp.max(m_acc, axis=0, keepdims=True)

    def probs(s_src, m, p_dst, lo=0):
        p_dst[lo:NK, :] = jnp.exp2(s_src[lo:NK, :] - m).astype(BF16)

    def attend(p_src, lo=0):
        acc = _dot(vt_scr[:, lo:NK], p_src[lo:NK, :])
        o_n = acc[0:dv] * (1.0 / acc[dv:dv + 1])
        o_t = o_n[:, 0:TQ] - lam * o_n[:, TQ:2 * TQ]
        ms = jnp.mean(o_t * o_t, axis=0, keepdims=True)
        y_t = o_t * (lax.rsqrt(ms + EPS) * (1.0 - lambda_init))
        return (y_t.T * g_ref[...]).astype(BF16)

    def q_block(i):
        return q_ref[pl.ds(pl.multiple_of(i * TQ, TQ), TQ), :]

    def emit(i, y):
        o_ref[pl.ds(pl.multiple_of(i * TQ, TQ), TQ), :] = y

    nb = S // TQ
    m_a = scores(q_block(0), s_a)
    m_b = scores(q_block(1), s_b)
    probs(s_a, m_a, p_a)

    def pair(j, m_b):
        b = 2 * j
        m_a = scores(q_block(b + 2), s_a)
        probs(s_b, m_b, p_b)
        emit(b, attend(p_a))
        m_b = scores(q_block(b + 3), s_b)
        probs(s_a, m_a, p_a)
        emit(b + 1, attend(p_b))
        return m_b

    m_b = lax.fori_loop(0, nb // 2 - 1, pair, m_b)
    probs(s_b, m_b, p_b)
    emit(nb - 2, attend(p_a))
    emit(nb - 1, attend(p_b))

    m_c = scores(qc_ref[...], s_a, S)
    probs(s_a, m_c, p_a, S)
    oc_ref[...] = attend(p_a, S)


def _diff(qk, vt, diff_lambda, subln_g, *, B, S, Lc, lambda_init):
    H = DIFF_HEADS
    dv = 2 * HEAD_DIM
    NK = S + Lc
    ctx_blk0 = B * S // Lc
    return pl.pallas_call(
        functools.partial(_diff_kernel, S=S, Lc=Lc, lambda_init=lambda_init),
        grid=(B, H),
        in_specs=[
            _resident(diff_lambda.shape),
            _resident(subln_g.shape),
            pl.BlockSpec((S, LANES), lambda b, h: (b, COL_DQ // LANES + h)),
            pl.BlockSpec((S, LANES), lambda b, h: (b, COL_DK // LANES + h)),
            pl.BlockSpec((dv, S), lambda b, h: (ROW_DV // dv + h, b)),
            pl.BlockSpec((Lc, LANES), lambda b, h: (ctx_blk0 + b, COL_DQ // LANES + h)),
            pl.BlockSpec((Lc, LANES), lambda b, h: (ctx_blk0 + b, COL_DK // LANES + h)),
            pl.BlockSpec((dv, Lc), lambda b, h: (ROW_DV // dv + h, ctx_blk0 + b)),
        ],
        out_specs=[
            pl.BlockSpec((S, dv), lambda b, h: (b, h)),
            pl.BlockSpec((Lc, dv), lambda b, h: (b, h)),
        ],
        out_shape=[
            jax.ShapeDtypeStruct((B * S, H * dv), BF16),
            jax.ShapeDtypeStruct((B * Lc, H * dv), BF16),
        ],
        scratch_shapes=[
            pltpu.VMEM((dv + ONES_ROWS, NK), BF16),
            pltpu.VMEM((NK, 2 * ATT_TQ), F32),
            pltpu.VMEM((NK, 2 * ATT_TQ), F32),
            pltpu.VMEM((NK, 2 * ATT_TQ), BF16),
            pltpu.VMEM((NK, 2 * ATT_TQ), BF16),
        ],
        compiler_params=_cparams(("arbitrary", "arbitrary")),
        name="diff",
    )(diff_lambda, subln_g, qk, qk, vt, qk, qk, vt)


def _rope_tables(S, tm):
    rows = S // GRID_W
    row = jnp.repeat(jnp.arange(rows, dtype=F32), GRID_W)
    col = jnp.tile(jnp.arange(GRID_W, dtype=F32), rows)
    n_freq = HEAD_DIM // 4
    inv_freq = ROPE_BASE ** (-jnp.arange(n_freq, dtype=F32) / n_freq)
    ang = jnp.concatenate([row[:, None] * inv_freq, col[:, None] * inv_freq], axis=-1)
    cos, sin = jnp.cos(ang), jnp.sin(ang)
    reps = LANES // (HEAD_DIM // 2)
    sign = np.tile(np.concatenate([-np.ones(HEAD_DIM // 2), np.ones(HEAD_DIM // 2)]), LANES // HEAD_DIM)
    cos_t = jnp.concatenate([jnp.tile(cos, (1, reps)), jnp.ones((tm, LANES), F32)], axis=0)
    sin_t = jnp.concatenate([jnp.tile(sin, (1, reps)) * sign.astype(np.float32), jnp.zeros((tm, LANES), F32)], axis=0)
    return cos_t, sin_t


def _block_diag(w):
    nd, K, c, _ = w.shape
    eye = jnp.eye(K, dtype=w.dtype)
    return jnp.einsum('dkij,kl->dkilj', w, eye).reshape(nd, K * c, K * c)


def kernel(x, c, ctx, c_ctx, w_ada, b_ada, norm_g, ffn1_w_gu, ffn1_w_down, ffn2_w_gu, ffn2_w_down,
           w_in, w_out, conv_w, conv_b, lru_w_r, lru_b_r, lru_w_i, lru_b_i, lru_lambda,
           swa_sink, diff_lambda, diff_subln_g, final_g):
    B, S, D = x.shape
    Lc = ctx.shape[1]
    depth = w_ada.shape[0]
    tm = TOKEN_TILE
    assert S % tm == 0 and (B * Lc) % tm == 0 and Lc == ATT_TQ and S % GRID_W == 0 and S % LRU_CHUNK == 0
    assert Lc == LRU_CHUNK and S >= ATT_TQ + 2 * WINDOW
    assert S % DIFF_CHUNK == 0 and (S // ATT_TQ) % 2 == 0
    n_lat_tiles = B * S // tm
    n_tiles = n_lat_tiles + B * Lc // tm
    geom = (n_lat_tiles, S // tm, B)

    h = (x.reshape(B * S, D), ctx.reshape(B * Lc, D))

    n_cond = B + 1
    pad = (-n_cond) % SUBLANES
    cond = jnp.concatenate([c, c_ctx[None], jnp.zeros((pad, D), F32)], axis=0)
    mod_all = _ada(cond, w_ada, b_ada).reshape(depth, n_cond + pad, N_MOD, D)

    cos_t, sin_t = _rope_tables(S, tm)
    sizes = (LRU_WIDTH, LRU_WIDTH, SWA_HEADS * HEAD_DIM, SWA_KV_HEADS * HEAD_DIM, SWA_KV_HEADS * HEAD_DIM,
             DIFF_HEADS * 2 * HEAD_DIM, DIFF_HEADS * 2 * HEAD_DIM, DIFF_HEADS * 2 * HEAD_DIM)
    offs = np.concatenate([[0], np.cumsum(sizes)])
    seg = [np.arange(offs[n], offs[n + 1]) for n in range(len(sizes))]
    seg[2] = offs[2] + np.concatenate([np.arange(hd * HEAD_DIM, (hd + 1) * HEAD_DIM) for hd in SWA_HEAD_ORDER])
    q_perm = np.concatenate([seg[0], seg[1], seg[2], seg[3], seg[5], seg[6], seg[4], seg[7]])
    o_perm = np.arange(w_out.shape[1])
    o_perm[LRU_WIDTH:LRU_WIDTH + SWA_HEADS * HEAD_DIM] = LRU_WIDTH + np.concatenate(
        [np.arange(hd * HEAD_DIM, (hd + 1) * HEAD_DIM) for hd in SWA_HEAD_ORDER])

    w1_gu, w1_down = ffn1_w_gu.astype(BF16), ffn1_w_down.astype(BF16)
    w2_gu, w2_down = ffn2_w_gu.astype(BF16), ffn2_w_down.astype(BF16)
    w_in_b = w_in[:, :, q_perm].astype(BF16)
    w_out_b = w_out[:, o_perm, :].astype(BF16)

    for l in range(depth):
        last = l == depth - 1
        mod = mod_all[l]
        lambda_init = 0.8 - 0.6 * math.exp(-0.3 * l)
        g = norm_g[l]
        h = _ffn(h, mod, g[0:1], w1_gu, w1_down, final_g[None], layer=l, k=0, n_tiles=n_tiles, geom=geom)
        lru_in, qk, vt = _inproj(h, mod, g[1:2], w_in_b, cos_t, sin_t, layer=l, n_tiles=n_tiles, geom=geom)
        y_lru = _lru(lru_in, conv_w[l], conv_b[l][None], _block_diag(lru_w_r[l]), lru_b_r[l][:, None],
                     _block_diag(lru_w_i[l]), lru_b_i[l][:, None], lru_lambda[l][:, None], B=B, S=S, Lc=Lc)
        y_swa = _swa(qk, vt, swa_sink[l][None], B=B, S=S, Lc=Lc)
        y_diff = _diff(qk, vt, diff_lambda[l], diff_subln_g[l][None], B=B, S=S, Lc=Lc, lambda_init=lambda_init)
        n_out = n_lat_tiles if last else n_tiles
        h = _outproj(h, mod, y_lru, y_swa, y_diff, w_out_b, layer=l, n_tiles=n_out, geom=geom)
        h = _ffn(h, mod, g[2:3], w2_gu, w2_down, final_g[None], layer=l, k=2, n_tiles=n_out, geom=geom,
                 final_norm=last)
    return h[:B * S].reshape(B, S, D)
```

```python
import functools
import math

import numpy as np
import jax
import jax.numpy as jnp
from jax import lax
from jax.experimental import pallas as pl
from jax.experimental.pallas import tpu as pltpu

F32 = jnp.float32
BF16 = jnp.bfloat16

GRID_W = 64
N_MOD = 9
EPS = 1e-6
FFN_RES = 0.5
HEAD_DIM = 64
ROPE_BASE = 10000.0
LRU_WIDTH = 256
LRU_BLOCKS = 4
CONV_WIDTH = 4
CONV_LEFT = 2
LRU_C = 8.0
SWA_HEADS = 4
SWA_KV_HEADS = 2
WINDOW = 128
DIFF_HEADS = 4
NEG_INF = -1e30

LANES = 128
SUBLANES = 8
VMEM_LIMIT_BYTES = 56 * 1024 * 1024

TOKEN_TILE = 512
FFN_CHUNKS = 1
ADA_COLS = 1024
ATT_TQ = 256
LRU_CHUNK = 256

QK_WIDTH = 1408
COL_SQ, COL_SK, COL_DQ, COL_DK = 0, 256, 384, 896
VT_ROWS = 640
ROW_SV, ROW_DV = 0, 128
IN_LRU, IN_SQ, IN_SK, IN_SV, IN_DQ, IN_DK, IN_DV = 0, 512, 768, 896, 1024, 1536, 2048
LOG2E = math.log2(math.e)
DIFF_ROWS = 16
DIFF_CHUNK = 1024
ONES_ROWS = 16
SWA_HEAD_ORDER = (0, 2, 1, 3)


def _cparams(semantics):
    return pltpu.CompilerParams(dimension_semantics=semantics, vmem_limit_bytes=VMEM_LIMIT_BYTES)


def _resident(shape):
    nd = len(shape)
    return pl.BlockSpec(shape, lambda *_: (0,) * nd, pipeline_mode=pl.Buffered(1))


def _layer_resident(stacked, layer):
    nd = stacked.ndim - 1
    return pl.BlockSpec((None,) + stacked.shape[1:], lambda *_: (layer,) + (0,) * nd, pipeline_mode=pl.Buffered(1))


def _dot(a, b):
    return jnp.dot(a, b, preferred_element_type=F32)


def _dot_nt(a, b):
    return lax.dot_general(a, b, (((1,), (1,)), ((), ())), preferred_element_type=F32)


def _rms(x, g):
    return x * lax.rsqrt(jnp.mean(x * x, axis=-1, keepdims=True) + EPS) * g


def _modnorm(h, g, mod_ref, k):
    shift = mod_ref[0, 3 * k:3 * k + 1, :]
    scale = mod_ref[0, 3 * k + 1:3 * k + 2, :]
    return _rms(h, g) * (1.0 + scale) + shift


def _expm1_of_square(a, t):
    u = a * a
    near_one = u == 1.0
    stable = (u - 1.0) * t / jnp.where(near_one, 1.0, jnp.log(u))
    return jnp.where(u < 0.5, u - 1.0, jnp.where(near_one, t, stable))


def _ada_kernel(c_ref, w_ref, b_ref, o_ref):
    c = c_ref[...]
    s = (c * jax.nn.sigmoid(c)).astype(BF16)
    o_ref[0] = _dot(s, w_ref[0].astype(BF16)) + b_ref[0]


def _ada(cond, w_ada, b_ada):
    L, D, N = w_ada.shape
    R = cond.shape[0]
    return pl.pallas_call(
        _ada_kernel,
        grid=(L, N // ADA_COLS),
        in_specs=[
            pl.BlockSpec((R, D), lambda l, n: (0, 0)),
            pl.BlockSpec((1, D, ADA_COLS), lambda l, n: (l, 0, n)),
            pl.BlockSpec((1, 1, ADA_COLS), lambda l, n: (l, 0, n)),
        ],
        out_specs=pl.BlockSpec((1, R, ADA_COLS), lambda l, n: (l, 0, n)),
        out_shape=jax.ShapeDtypeStruct((L, R, N), F32),
        compiler_params=_cparams(("arbitrary", "arbitrary")),
        name="ada",
    )(cond, w_ada, b_ada.reshape(L, 1, N))


def _mod_index(n_lat_tiles, tiles_per_batch, n_batch):
    def idx(i):
        return (jnp.where(i < n_lat_tiles, i // tiles_per_batch, n_batch), 0, 0)
    return idx


def _ffn_kernel(*refs, k, d_ff, final_norm, n_lat_tiles, split_input, n_mix, mix_has_ctx):
    mod_ref, g_ref, wgu_ref, wd_ref, fg_ref, o_ref = refs[-6:]
    is_lat = pl.program_id(0) < n_lat_tiles
    if split_input:
        h = jnp.where(is_lat, refs[0][...], refs[1][...])
    else:
        h = refs[0][...]
    if n_mix:
        mix_refs = refs[-7 - 2 * n_mix:-7]
        parts = []
        for lat_ref, ctx_ref in zip(mix_refs[0::2], mix_refs[1::2]):
            parts.append(jnp.where(is_lat, lat_ref[...], ctx_ref[...]) if mix_has_ctx else lat_ref[...])
        h = h + mod_ref[0, 5:6, :] * _dot(jnp.concatenate(parts, axis=1), refs[-7][...])
    xn = _modnorm(h, g_ref[...], mod_ref, k).astype(BF16)
    tf = d_ff // FFN_CHUNKS
    acc = None
    for c in range(FFN_CHUNKS):
        lo = c * tf
        g = _dot(xn, wgu_ref[:, lo:lo + tf])
        u = _dot(xn, wgu_ref[:, d_ff + lo:d_ff + lo + tf])
        a = (g * jax.nn.sigmoid(g) * u).astype(BF16)
        part = _dot(a, wd_ref[lo:lo + tf, :])
        acc = part if acc is None else acc + part
    gate = mod_ref[0, 3 * k + 2:3 * k + 3, :]
    out = h + (FFN_RES * gate) * acc
    if final_norm:
        out = _rms(out, fg_ref[...])
    o_ref[...] = out


def _ffn(h, mod, g, w_gu, w_down, final_g, *, layer, k, n_tiles, geom, final_norm=False, mix=(), w_out=None):
    split = isinstance(h, tuple)
    hs = h if split else (h,)
    D = hs[0].shape[1]
    d_ff = w_down.shape[1]
    tm = TOKEN_TILE
    n_lat_tiles = geom[0]

    def lat_idx(i):
        return (jnp.minimum(i, n_lat_tiles - 1), 0)

    def ctx_idx(i):
        return (jnp.maximum(i - n_lat_tiles, 0), 0)

    if split:
        h_specs = [pl.BlockSpec((tm, D), lat_idx), pl.BlockSpec((tm, D), ctx_idx)]
    else:
        h_specs = [pl.BlockSpec((tm, D), lambda i: (i, 0))]
    mix_args, mix_specs = [], []
    for lat, ctx in mix:
        mix_args += [lat, ctx]
        mix_specs += [pl.BlockSpec((tm, lat.shape[1]), lat_idx), pl.BlockSpec((tm, lat.shape[1]), ctx_idx)]
    if mix:
        mix_args.append(w_out)
        mix_specs.append(_layer_resident(w_out, layer))
    return pl.pallas_call(
        functools.partial(_ffn_kernel, k=k, d_ff=d_ff, final_norm=final_norm, n_lat_tiles=n_lat_tiles,
                          split_input=split, n_mix=len(mix), mix_has_ctx=n_tiles > n_lat_tiles),
        grid=(n_tiles,),
        in_specs=h_specs + mix_specs + [
            pl.BlockSpec((1, N_MOD, D), _mod_index(*geom)),
            _resident((1, D)),
            _layer_resident(w_gu, layer),
            _layer_resident(w_down, layer),
            _resident((1, D)),
        ],
        out_specs=pl.BlockSpec((tm, D), lambda i: (i, 0)),
        out_shape=jax.ShapeDtypeStruct((n_tiles * tm, D), F32),
        compiler_params=_cparams(("arbitrary",)),
        name="ffn",
    )(*hs, *mix_args, mod, g, w_gu, w_down, final_g)


def _rope(x, cos, sin):
    lane = lax.broadcasted_iota(jnp.int32, x.shape, 1)
    first = (lane & (HEAD_DIM - 1)) < (HEAD_DIM // 2)
    partner = jnp.where(first, pltpu.roll(x, LANES - HEAD_DIM // 2, 1), pltpu.roll(x, HEAD_DIM // 2, 1))
    return x * cos + partner * sin


def _inproj_kernel(h_ref, mod_ref, g_ref, w_ref, cos_ref, sin_ref, lru_ref, qk_ref, vt_ref):
    xn = _modnorm(h_ref[...], g_ref[...], mod_ref, 1).astype(BF16)
    y = _dot(xn, w_ref[...])
    lru_ref[...] = y[:, IN_LRU:IN_SQ]
    cos = cos_ref[...]
    sin = sin_ref[...]
    q_scale = HEAD_DIM ** -0.5 * LOG2E

    def roped(c0, scale=None):
        blk = _rope(y[:, c0:c0 + LANES], cos, sin)
        return blk if scale is None else blk * scale

    def put(c0, blk):
        qk_ref[:, c0:c0 + LANES] = blk.astype(BF16)

    qa, qb = roped(IN_SQ, q_scale), roped(IN_SQ + LANES, q_scale)
    low = lax.broadcasted_iota(jnp.int32, qa.shape, 1) < HEAD_DIM
    put(COL_SQ, jnp.where(low, qa, pltpu.roll(qb, HEAD_DIM, 1)))
    put(COL_SQ + LANES, jnp.where(low, pltpu.roll(qa, HEAD_DIM, 1), qb))
    put(COL_SK, roped(IN_SK))
    for c in range(0, DIFF_HEADS * 2 * HEAD_DIM, LANES):
        put(COL_DQ + c, roped(IN_DQ + c, q_scale))
        put(COL_DK + c, roped(IN_DK + c))
    sv_w = SWA_KV_HEADS * HEAD_DIM
    vt_ref[ROW_SV:ROW_SV + sv_w, :] = y[:, IN_SV:IN_SV + sv_w].T.astype(BF16)
    vt_ref[ROW_DV:VT_ROWS, :] = y[:, IN_DV:IN_DV + VT_ROWS - ROW_DV].T.astype(BF16)


def _inproj(h, mod, g, w_in, cos_t, sin_t, *, layer, n_tiles, geom):
    T, D = h.shape
    tm = TOKEN_TILE
    n_lat_tiles, tiles_per_batch, _ = geom

    def tab_idx(i):
        return (jnp.where(i < n_lat_tiles, i % tiles_per_batch, tiles_per_batch), 0)

    return pl.pallas_call(
        _inproj_kernel,
        grid=(n_tiles,),
        in_specs=[
            pl.BlockSpec((tm, D), lambda i: (i, 0)),
            pl.BlockSpec((1, N_MOD, D), _mod_index(*geom)),
            _resident((1, D)),
            _layer_resident(w_in, layer),
            pl.BlockSpec((tm, LANES), tab_idx),
            pl.BlockSpec((tm, LANES), tab_idx),
        ],
        out_specs=[
            pl.BlockSpec((tm, 2 * LRU_WIDTH), lambda i: (i, 0)),
            pl.BlockSpec((tm, QK_WIDTH), lambda i: (i, 0)),
            pl.BlockSpec((VT_ROWS, tm), lambda i: (0, i)),
        ],
        out_shape=[
            jax.ShapeDtypeStruct((T, 2 * LRU_WIDTH), F32),
            jax.ShapeDtypeStruct((T, QK_WIDTH), BF16),
            jax.ShapeDtypeStruct((VT_ROWS, T), BF16),
        ],
        compiler_params=_cparams(("arbitrary",)),
        name="inproj",
    )(h, mod, g, w_in, cos_t, sin_t)


def _lru_kernel(lat_ref, ctx_ref, cw_ref, cb_ref, wr_ref, br_ref, wi_ref, bi_ref, lam_ref,
                ylat_ref, yctx_ref, xpad, u_scr, hf_scr, ge_scr, *g_scr, S, Lc):
    W = LRU_WIDTH
    TC = LRU_CHUNK
    PAD = SUBLANES
    row = lax.broadcasted_iota(jnp.int32, (TC, W), 0)
    group = lax.broadcasted_iota(jnp.int32, (TC // SUBLANES, W), 0)

    def conv_chunk(start):
        a = xpad[pl.ds(start, TC + 2 * PAD), :]
        u = cb_ref[...] + jnp.zeros((TC, W), F32)
        for k in range(CONV_WIDTH):
            sh = (CONV_LEFT - k) % (TC + 2 * PAD)
            r = a if sh == 0 else pltpu.roll(a, sh, 0)
            u = u + r[PAD:PAD + TC] * cw_ref[k:k + 1, :]
        return u

    def scan_chunk(u, d, carry, reverse):
        r = jax.nn.sigmoid(_dot(u, wr_ref[d]) + br_ref[d])
        i = jax.nn.sigmoid(_dot(u, wi_ref[d]) + bi_ref[d])
        log_a = (-LRU_C) * r * jax.nn.softplus(-lam_ref[d])
        A = jnp.exp(log_a)
        Bv = jnp.sqrt(-_expm1_of_square(A, 2.0 * log_a)) * i * u

        def combine(A, Bv, pos, axis):
            n = A.shape[axis]
            s = 1
            while s < n:
                if reverse:
                    keep = pos < n - s
                    sh = n - s
                else:
                    keep = pos >= s
                    sh = s
                a_sh = jnp.where(keep, pltpu.roll(A, sh, axis), 1.0)
                b_sh = jnp.where(keep, pltpu.roll(Bv, sh, axis), 0.0)
                Bv = A * b_sh + Bv
                A = A * a_sh
                s *= 2
            return A, Bv

        G = SUBLANES
        ng = TC // G
        A, Bv = combine(A.reshape(ng, G, W), Bv.reshape(ng, G, W), lax.broadcasted_iota(jnp.int32, (ng, G, W), 1), 1)
        A = A.reshape(TC, W)
        Bv = Bv.reshape(TC, W)
        edge = 0 if reverse else G - 1

        def boundary_rows(planes, x):
            out = []
            for c, scr in enumerate(planes):
                scr[...] = x[:, c * LANES:(c + 1) * LANES]
                out.append(scr[pl.ds(edge, ng, stride=G), :])
            return jnp.concatenate(out, axis=1)

        n_pl = W // LANES
        Ag, Bg = combine(boundary_rows(g_scr[:n_pl], A), boundary_rows(g_scr[n_pl:], Bv), group, 0)
        ge_scr[...] = Ag * carry + Bg
        parts = []
        for g in range(ng):
            nb_g = g + 1 if reverse else g - 1
            h_in = carry if (nb_g < 0 or nb_g >= ng) else ge_scr[nb_g:nb_g + 1, :]
            parts.append(A[g * G:(g + 1) * G] * h_in + Bv[g * G:(g + 1) * G])
        h = jnp.concatenate(parts, axis=0)
        return h, (h[0:1] if reverse else h[TC - 1:TC])

    def gelu(x):
        return jax.nn.gelu(x)

    zero_pad = jnp.zeros((PAD, W), F32)
    zero_state = jnp.zeros((1, W), F32)

    xpad[0:PAD, :] = zero_pad
    xpad[PAD:PAD + Lc, :] = ctx_ref[:, 0:W]
    xpad[PAD + Lc:2 * PAD + Lc, :] = zero_pad
    uc = conv_chunk(0)
    hcf, carry_f = scan_chunk(uc, 0, zero_state, False)
    hcb, carry_b = scan_chunk(uc, 1, zero_state, True)
    yctx_ref[...] = ((hcf + hcb) * gelu(ctx_ref[:, W:2 * W])).astype(BF16)

    xpad[PAD:PAD + S, :] = lat_ref[:, 0:W]
    xpad[PAD + S:2 * PAD + S, :] = zero_pad
    nc = S // TC

    def conv_body(c, _):
        st = pl.multiple_of(c * TC, TC)
        u_scr[pl.ds(st, TC), :] = conv_chunk(st)
        return 0

    lax.fori_loop(0, nc, conv_body, 0)

    def fwd_body(c, carry):
        st = pl.multiple_of(c * TC, TC)
        h, carry = scan_chunk(u_scr[pl.ds(st, TC), :], 0, carry, False)
        hf_scr[pl.ds(st, TC), :] = h
        return carry

    lax.fori_loop(0, nc, fwd_body, carry_f)

    def bwd_body(c, carry):
        st = pl.multiple_of((nc - 1 - c) * TC, TC)
        h, carry = scan_chunk(u_scr[pl.ds(st, TC), :], 1, carry, True)
        y = (hf_scr[pl.ds(st, TC), :] + h) * gelu(lat_ref[pl.ds(st, TC), W:2 * W])
        ylat_ref[pl.ds(st, TC), :] = y.astype(BF16)
        return carry

    lax.fori_loop(0, nc, bwd_body, carry_b)


def _lru(lru_in, conv_w, conv_b, w_r, b_r, w_i, b_i, lam, *, B, S, Lc):
    W = LRU_WIDTH
    t_lat = B * S
    ctx_blk0 = t_lat // Lc
    return pl.pallas_call(
        functools.partial(_lru_kernel, S=S, Lc=Lc),
        grid=(B,),
        in_specs=[
            pl.BlockSpec((S, 2 * W), lambda b: (b, 0)),
            pl.BlockSpec((Lc, 2 * W), lambda b: (ctx_blk0 + b, 0)),
            _resident(conv_w.shape),
            _resident(conv_b.shape),
            _resident(w_r.shape),
            _resident(b_r.shape),
            _resident(w_i.shape),
            _resident(b_i.shape),
            _resident(lam.shape),
        ],
        out_specs=[
            pl.BlockSpec((S, W), lambda b: (b, 0)),
            pl.BlockSpec((Lc, W), lambda b: (b, 0)),
        ],
        out_shape=[
            jax.ShapeDtypeStruct((t_lat, W), BF16),
            jax.ShapeDtypeStruct((B * Lc, W), BF16),
        ],
        scratch_shapes=[
            pltpu.VMEM((S + 2 * SUBLANES, W), F32),
            pltpu.VMEM((S, W), F32),
            pltpu.VMEM((S, W), F32),
            pltpu.VMEM((LRU_CHUNK // SUBLANES, W), F32),
        ] + [pltpu.VMEM((LRU_CHUNK, LANES), F32)] * (2 * (W // LANES)) + [
        ],
        compiler_params=_cparams(("arbitrary",)),
        name="lru",
    )(lru_in, lru_in, conv_w, conv_b, w_r, b_r, w_i, b_i, lam)


def _swa_kernel(sink_ref, q_ref, k_ref, vt_ref, qc_ref, kc_ref, vtc_ref, o_ref, oc_ref,
                vt3, s_a, s_b, p_a, p_b, *, S, Lc):
    TQ = ATT_TQ
    KB = TQ + 2 * WINDOW
    NKB = KB + Lc
    hd = HEAD_DIM
    nh = SWA_HEADS
    n_lat_ch = S // LANES
    n_ctx_ch = Lc // LANES
    band_ch = KB // LANES
    ones = jnp.ones((ONES_ROWS, LANES), BF16)
    for c in range(n_lat_ch + n_ctx_ch):
        src = vt_ref[:, c * LANES:(c + 1) * LANES] if c < n_lat_ch else \
            vtc_ref[:, (c - n_lat_ch) * LANES:(c - n_lat_ch + 1) * LANES]
        vt3[c, 0:2 * hd, :] = src
        vt3[c, 2 * hd:2 * hd + ONES_ROWS, :] = ones
    low = lax.broadcasted_iota(jnp.int32, (TQ, LANES), 1) < hd
    col = lax.broadcasted_iota(jnp.int32, (1, nh * TQ), 1)
    sink = jnp.full((1, nh * TQ), sink_ref[0, SWA_HEAD_ORDER[-1]] * LOG2E, F32)
    for c in range(nh - 2, -1, -1):
        sink = jnp.where(col < (c + 1) * TQ, sink_ref[0, SWA_HEAD_ORDER[c]] * LOG2E, sink)

    def band_start(t):
        return jnp.clip(2 * t - 1, 0, n_lat_ch - band_ch)

    def scores(q, s_dst, t=None):
        zero = jnp.zeros((TQ, LANES), BF16)
        rows = []
        for grp in range(2):
            qg = q[:, grp * LANES:(grp + 1) * LANES]
            rows += [jnp.where(low, qg, zero), jnp.where(low, zero, qg)]
        q4 = jnp.concatenate(rows, axis=0)
        s_c = _dot_nt(kc_ref[...], q4)
        s_dst[KB:NKB, :] = s_c
        m = jnp.max(s_c, axis=0, keepdims=True)
        if t is not None:
            st = band_start(t) * LANES
            kb = k_ref[pl.ds(pl.multiple_of(st, LANES), KB), :]
            k_abs = st + lax.broadcasted_iota(jnp.int32, (KB, TQ), 0)
            q_abs = t * TQ + lax.broadcasted_iota(jnp.int32, (KB, TQ), 1)
            valid = jnp.abs(q_abs - k_abs) <= WINDOW
            s_l = _dot_nt(kb, q4)
            s_l = jnp.concatenate(
                [jnp.where(valid, s_l[:, c * TQ:(c + 1) * TQ], NEG_INF) for c in range(nh)], axis=1)
            s_dst[0:KB, :] = s_l
            m = jnp.maximum(m, jnp.max(s_l, axis=0, keepdims=True))
        return jnp.maximum(m, sink)

    def probs(s_src, m, p_dst, lo=0):
        p_dst[lo:NKB, :] = jnp.exp2(s_src[lo:NKB, :] - m).astype(BF16)

    def attend(p_src, m, t=None):
        ctx_v = [vt3[n_lat_ch + c] for c in range(n_ctx_ch)]
        if t is None:
            lo, vt = KB, jnp.concatenate(ctx_v, axis=1)
        else:
            st = band_start(t)
            lo, vt = 0, jnp.concatenate([vt3[st + d] for d in range(band_ch)] + ctx_v, axis=1)
        acc = _dot(vt, p_src[lo:NKB, :])
        den = acc[2 * hd:2 * hd + 1] + jnp.exp2(sink - m)
        o_n = acc[0:2 * hd] * (1.0 / den)
        pieces = []
        for h in range(nh):
            c = SWA_HEAD_ORDER.index(h)
            pieces.append(o_n[(c % 2) * hd:(c % 2 + 1) * hd, c * TQ:(c + 1) * TQ])
        return jnp.concatenate(pieces, axis=0).T.astype(BF16)

    def q_block(i):
        return q_ref[pl.ds(pl.multiple_of(i * TQ, TQ), TQ), :]

    def emit(i, y):
        o_ref[pl.ds(pl.multiple_of(i * TQ, TQ), TQ), :] = y

    nb = S // TQ
    m_a = scores(q_block(0), s_a, 0)
    m_b = scores(q_block(1), s_b, 1)
    probs(s_a, m_a, p_a)

    def pair(j, carry):
        m_b, m_pa = carry
        b = 2 * j
        m_a = scores(q_block(b + 2), s_a, b + 2)
        probs(s_b, m_b, p_b)
        emit(b, attend(p_a, m_pa, b))
        m_b2 = scores(q_block(b + 3), s_b, b + 3)
        probs(s_a, m_a, p_a)
        emit(b + 1, attend(p_b, m_b, b + 1))
        return m_b2, m_a

    m_b, m_pa = lax.fori_loop(0, nb // 2 - 1, pair, (m_b, m_a))
    probs(s_b, m_b, p_b)
    emit(nb - 2, attend(p_a, m_pa, nb - 2))
    emit(nb - 1, attend(p_b, m_b, nb - 1))

    m_c = scores(qc_ref[...], s_a)
    probs(s_a, m_c, p_a, KB)
    oc_ref[...] = attend(p_a, m_c)


def _swa(qk, vt, sink, *, B, S, Lc):
    qw = SWA_HEADS * HEAD_DIM
    ctx_blk0 = B * S // Lc
    sv_blk = ROW_SV // LANES
    NKB = ATT_TQ + 2 * WINDOW + Lc
    return pl.pallas_call(
        functools.partial(_swa_kernel, S=S, Lc=Lc),
        grid=(B,),
        in_specs=[
            pl.BlockSpec(memory_space=pltpu.SMEM),
            pl.BlockSpec((S, qw), lambda b: (b, COL_SQ // qw)),
            pl.BlockSpec((S, LANES), lambda b: (b, COL_SK // LANES)),
            pl.BlockSpec((LANES, S), lambda b: (sv_blk, b)),
            pl.BlockSpec((Lc, qw), lambda b: (ctx_blk0 + b, COL_SQ // qw)),
            pl.BlockSpec((Lc, LANES), lambda b: (ctx_blk0 + b, COL_SK // LANES)),
            pl.BlockSpec((LANES, Lc), lambda b: (sv_blk, ctx_blk0 + b)),
        ],
        out_specs=[
            pl.BlockSpec((S, qw), lambda b: (b, 0)),
            pl.BlockSpec((Lc, qw), lambda b: (b, 0)),
        ],
        out_shape=[
            jax.ShapeDtypeStruct((B * S, qw), BF16),
            jax.ShapeDtypeStruct((B * Lc, qw), BF16),
        ],
        scratch_shapes=[
            pltpu.VMEM(((S + Lc) // LANES, 2 * HEAD_DIM + ONES_ROWS, LANES), BF16),
            pltpu.VMEM((NKB, SWA_HEADS * ATT_TQ), F32),
            pltpu.VMEM((NKB, SWA_HEADS * ATT_TQ), F32),
            pltpu.VMEM((NKB, SWA_HEADS * ATT_TQ), BF16),
            pltpu.VMEM((NKB, SWA_HEADS * ATT_TQ), BF16),
        ],
        compiler_params=_cparams(("arbitrary",)),
        name="swa",
    )(sink, qk, qk, vt, qk, qk, vt)


def _diff_kernel(dl_ref, g_ref, q_ref, k_ref, vt_ref, qc_ref, kc_ref, vtc_ref, o_ref, oc_ref,
                 vt_scr, s_a, s_b, p_a, p_b, *, S, Lc, lambda_init):
    TQ = ATT_TQ
    NK = S + Lc
    dv = 2 * HEAD_DIM
    vt_scr[0:dv, 0:S] = vt_ref[...]
    vt_scr[0:dv, S:NK] = vtc_ref[...]
    vt_scr[dv:dv + ONES_ROWS, :] = jnp.ones((ONES_ROWS, NK), BF16)
    dl = dl_ref[...]
    lam = (jnp.exp(jnp.sum(dl[0:1] * dl[1:2], axis=-1, keepdims=True))
           - jnp.exp(jnp.sum(dl[2:3] * dl[3:4], axis=-1, keepdims=True)) + lambda_init)
    low = lax.broadcasted_iota(jnp.int32, (TQ, LANES), 1) < HEAD_DIM
    lat_pieces = [(r, r + DIFF_CHUNK, k_ref, r) for r in range(0, S, DIFF_CHUNK)]
    ctx_piece = (S, NK, kc_ref, 0)

    def scores(q, s_dst, lo=0):
        zero = jnp.zeros_like(q)
        q2 = jnp.concatenate([jnp.where(low, q, zero), jnp.where(low, zero, q)], axis=0)
        m_acc = None
        for r0, r1, kref, off in ([] if lo else lat_pieces) + [ctx_piece]:
            blk = _dot_nt(kref[off:off + (r1 - r0), :], q2)
            s_dst[r0:r1, :] = blk
            for r in range(0, r1 - r0, DIFF_ROWS):
                part = blk[r:r + DIFF_ROWS]
                m_acc = part if m_acc is None else jnp.maximum(m_acc, part)
        return jnp.max(m_acc, axis=0, keepdims=True)

    def probs(s_src, m, p_dst, lo=0):
        p_dst[lo:NK, :] = jnp.exp2(s_src[lo:NK, :] - m).astype(BF16)

    def attend(p_src, lo=0):
        acc = _dot(vt_scr[:, lo:NK], p_src[lo:NK, :])
        o_n = acc[0:dv] * (1.0 / acc[dv:dv + 1])
        o_t = o_n[:, 0:TQ] - lam * o_n[:, TQ:2 * TQ]
        ms = jnp.mean(o_t * o_t, axis=0, keepdims=True)
        y_t = o_t * (lax.rsqrt(ms + EPS) * (1.0 - lambda_init))
        return (y_t.T * g_ref[...]).astype(BF16)

    def q_block(i):
        return q_ref[pl.ds(pl.multiple_of(i * TQ, TQ), TQ), :]

    def emit(i, y):
        o_ref[pl.ds(pl.multiple_of(i * TQ, TQ), TQ), :] = y

    nb = S // TQ
    m_a = scores(q_block(0), s_a)
    m_b = scores(q_block(1), s_b)
    probs(s_a, m_a, p_a)

    def pair(j, m_b):
        b = 2 * j
        m_a = scores(q_block(b + 2), s_a)
        probs(s_b, m_b, p_b)
        emit(b, attend(p_a))
        m_b = scores(q_block(b + 3), s_b)
        probs(s_a, m_a, p_a)
        emit(b + 1, attend(p_b))
        return m_b

    m_b = lax.fori_loop(0, nb // 2 - 1, pair, m_b)
    probs(s_b, m_b, p_b)
    emit(nb - 2, attend(p_a))
    emit(nb - 1, attend(p_b))

    m_c = scores(qc_ref[...], s_a, S)
    probs(s_a, m_c, p_a, S)
    oc_ref[...] = attend(p_a, S)


def _diff(qk, vt, diff_lambda, subln_g, *, B, S, Lc, lambda_init):
    H = DIFF_HEADS
    dv = 2 * HEAD_DIM
    NK = S + Lc
    ctx_blk0 = B * S // Lc
    return pl.pallas_call(
        functools.partial(_diff_kernel, S=S, Lc=Lc, lambda_init=lambda_init),
        grid=(B, H),
        in_specs=[
            _resident(diff_lambda.shape),
            _resident(subln_g.shape),
            pl.BlockSpec((S, LANES), lambda b, h: (b, COL_DQ // LANES + h)),
            pl.BlockSpec((S, LANES), lambda b, h: (b, COL_DK // LANES + h)),
            pl.BlockSpec((dv, S), lambda b, h: (ROW_DV // dv + h, b)),
            pl.BlockSpec((Lc, LANES), lambda b, h: (ctx_blk0 + b, COL_DQ // LANES + h)),
            pl.BlockSpec((Lc, LANES), lambda b, h: (ctx_blk0 + b, COL_DK // LANES + h)),
            pl.BlockSpec((dv, Lc), lambda b, h: (ROW_DV // dv + h, ctx_blk0 + b)),
        ],
        out_specs=[
            pl.BlockSpec((S, dv), lambda b, h: (b, h)),
            pl.BlockSpec((Lc, dv), lambda b, h: (b, h)),
        ],
        out_shape=[
            jax.ShapeDtypeStruct((B * S, H * dv), BF16),
            jax.ShapeDtypeStruct((B * Lc, H * dv), BF16),
        ],
        scratch_shapes=[
            pltpu.VMEM((dv + ONES_ROWS, NK), BF16),
            pltpu.VMEM((NK, 2 * ATT_TQ), F32),
            pltpu.VMEM((NK, 2 * ATT_TQ), F32),
            pltpu.VMEM((NK, 2 * ATT_TQ), BF16),
            pltpu.VMEM((NK, 2 * ATT_TQ), BF16),
        ],
        compiler_params=_cparams(("arbitrary", "arbitrary")),
        name="diff",
    )(diff_lambda, subln_g, qk, qk, vt, qk, qk, vt)


def _rope_tables(S, tm):
    rows = S // GRID_W
    row = jnp.repeat(jnp.arange(rows, dtype=F32), GRID_W)
    col = jnp.tile(jnp.arange(GRID_W, dtype=F32), rows)
    n_freq = HEAD_DIM // 4
    inv_freq = ROPE_BASE ** (-jnp.arange(n_freq, dtype=F32) / n_freq)
    ang = jnp.concatenate([row[:, None] * inv_freq, col[:, None] * inv_freq], axis=-1)
    cos, sin = jnp.cos(ang), jnp.sin(ang)
    reps = LANES // (HEAD_DIM // 2)
    sign = np.tile(np.concatenate([-np.ones(HEAD_DIM // 2), np.ones(HEAD_DIM // 2)]), LANES // HEAD_DIM)
    cos_t = jnp.concatenate([jnp.tile(cos, (1, reps)), jnp.ones((tm, LANES), F32)], axis=0)
    sin_t = jnp.concatenate([jnp.tile(sin, (1, reps)) * sign.astype(np.float32), jnp.zeros((tm, LANES), F32)], axis=0)
    return cos_t, sin_t


def _block_diag(w):
    nd, K, c, _ = w.shape
    eye = jnp.eye(K, dtype=w.dtype)
    return jnp.einsum('dkij,kl->dkilj', w, eye).reshape(nd, K * c, K * c)


def kernel(x, c, ctx, c_ctx, w_ada, b_ada, norm_g, ffn1_w_gu, ffn1_w_down, ffn2_w_gu, ffn2_w_down,
           w_in, w_out, conv_w, conv_b, lru_w_r, lru_b_r, lru_w_i, lru_b_i, lru_lambda,
           swa_sink, diff_lambda, diff_subln_g, final_g):
    B, S, D = x.shape
    Lc = ctx.shape[1]
    depth = w_ada.shape[0]
    tm = TOKEN_TILE
    assert S % tm == 0 and (B * Lc) % tm == 0 and Lc == ATT_TQ and S % GRID_W == 0 and S % LRU_CHUNK == 0
    assert Lc == LRU_CHUNK and S >= ATT_TQ + 2 * WINDOW
    assert S % DIFF_CHUNK == 0 and (S // ATT_TQ) % 2 == 0
    n_lat_tiles = B * S // tm
    n_tiles = n_lat_tiles + B * Lc // tm
    geom = (n_lat_tiles, S // tm, B)

    h = (x.reshape(B * S, D), ctx.reshape(B * Lc, D))

    n_cond = B + 1
    pad = (-n_cond) % SUBLANES
    cond = jnp.concatenate([c, c_ctx[None], jnp.zeros((pad, D), F32)], axis=0)
    mod_all = _ada(cond, w_ada, b_ada).reshape(depth, n_cond + pad, N_MOD, D)

    cos_t, sin_t = _rope_tables(S, tm)
    w1_gu, w1_down = ffn1_w_gu.astype(BF16), ffn1_w_down.astype(BF16)
    w2_gu, w2_down = ffn2_w_gu.astype(BF16), ffn2_w_down.astype(BF16)
    w_in_b, w_out_b = w_in.astype(BF16), w_out.astype(BF16)

    for l in range(depth):
        last = l == depth - 1
        mod = mod_all[l]
        lambda_init = 0.8 - 0.6 * math.exp(-0.3 * l)
        g = norm_g[l]
        h = _ffn(h, mod, g[0:1], w1_gu, w1_down, final_g[None], layer=l, k=0, n_tiles=n_tiles, geom=geom)
        lru_in, qk, vt = _inproj(h, mod, g[1:2], w_in_b, cos_t, sin_t, layer=l, n_tiles=n_tiles, geom=geom)
        y_lru = _lru(lru_in, conv_w[l], conv_b[l][None], _block_diag(lru_w_r[l]), lru_b_r[l][:, None],
                     _block_diag(lru_w_i[l]), lru_b_i[l][:, None], lru_lambda[l][:, None], B=B, S=S, Lc=Lc)
        y_swa = _swa(qk, vt, swa_sink[l][None], B=B, S=S, Lc=Lc)
        y_diff = _diff(qk, vt, diff_lambda[l], diff_subln_g[l][None], B=B, S=S, Lc=Lc, lambda_init=lambda_init)
        n_out = n_lat_tiles if last else n_tiles
        h = _ffn(h, mod, g[2:3], w2_gu, w2_down, final_g[None], layer=l, k=2, n_tiles=n_out, geom=geom,
                 final_norm=last, mix=(y_lru, y_swa, y_diff), w_out=w_out_b)
    return h[:B * S].reshape(B, S, D)
```

```python
import functools
import math

import numpy as np
import jax
import jax.numpy as jnp
from jax import lax
from jax.experimental import pallas as pl
from jax.experimental.pallas import tpu as pltpu

F32 = jnp.float32
BF16 = jnp.bfloat16

GRID_W = 64
N_MOD = 9
EPS = 1e-6
FFN_RES = 0.5
HEAD_DIM = 64
ROPE_BASE = 10000.0
LRU_WIDTH = 256
LRU_BLOCKS = 4
CONV_WIDTH = 4
CONV_LEFT = 2
LRU_C = 8.0
SWA_HEADS = 4
SWA_KV_HEADS = 2
WINDOW = 128
DIFF_HEADS = 4
NEG_INF = -1e30

LANES = 128
SUBLANES = 8
VMEM_LIMIT_BYTES = 56 * 1024 * 1024

TOKEN_TILE = 512
FFN_CHUNKS = 1
ADA_COLS = 1024
ATT_TQ = 256
LRU_CHUNK = 256

QK_WIDTH = 1408
COL_SQ, COL_DQ, COL_DK, COL_SK = 0, 256, 768, 1280
VT_ROWS = 640
ROW_DV, ROW_SV = 0, 512
IN_LRU, IN_SQ, IN_SK, IN_SV, IN_DQ, IN_DK, IN_DV = 0, 512, 768, 896, 1024, 1536, 2048
LOG2E = math.log2(math.e)
DIFF_ROWS = 16
DIFF_CHUNK = 1024
DIFF_HEADS_STEP = 2
ONES_ROWS = 16
SWA_HEAD_ORDER = (0, 2, 1, 3)


def _cparams(semantics):
    return pltpu.CompilerParams(dimension_semantics=semantics, vmem_limit_bytes=VMEM_LIMIT_BYTES)


def _resident(shape):
    nd = len(shape)
    return pl.BlockSpec(shape, lambda *_: (0,) * nd, pipeline_mode=pl.Buffered(1))


def _layer_resident(stacked, layer):
    nd = stacked.ndim - 1
    return pl.BlockSpec((None,) + stacked.shape[1:], lambda *_: (layer,) + (0,) * nd, pipeline_mode=pl.Buffered(1))


def _dot(a, b):
    return jnp.dot(a, b, preferred_element_type=F32)


def _dot_nt(a, b):
    return lax.dot_general(a, b, (((1,), (1,)), ((), ())), preferred_element_type=F32)


def _rms(x, g):
    return x * lax.rsqrt(jnp.mean(x * x, axis=-1, keepdims=True) + EPS) * g


def _modnorm(h, g, mod_ref, k):
    shift = mod_ref[0, 3 * k:3 * k + 1, :]
    scale = mod_ref[0, 3 * k + 1:3 * k + 2, :]
    return _rms(h, g) * (1.0 + scale) + shift


def _neg_expm1_2x(a, x):
    return (1.0 + a * a) * jnp.tanh(-x)


def _ada_kernel(c_ref, w_ref, b_ref, o_ref):
    c = c_ref[...]
    s = (c * jax.nn.sigmoid(c)).astype(BF16)
    o_ref[0] = _dot(s, w_ref[0].astype(BF16)) + b_ref[0]


def _ada(cond, w_ada, b_ada):
    L, D, N = w_ada.shape
    R = cond.shape[0]
    return pl.pallas_call(
        _ada_kernel,
        grid=(L, N // ADA_COLS),
        in_specs=[
            pl.BlockSpec((R, D), lambda l, n: (0, 0)),
            pl.BlockSpec((1, D, ADA_COLS), lambda l, n: (l, 0, n)),
            pl.BlockSpec((1, 1, ADA_COLS), lambda l, n: (l, 0, n)),
        ],
        out_specs=pl.BlockSpec((1, R, ADA_COLS), lambda l, n: (l, 0, n)),
        out_shape=jax.ShapeDtypeStruct((L, R, N), F32),
        compiler_params=_cparams(("arbitrary", "arbitrary")),
        name="ada",
    )(cond, w_ada, b_ada.reshape(L, 1, N))


def _mod_index(n_lat_tiles, tiles_per_batch, n_batch):
    def idx(i):
        return (jnp.where(i < n_lat_tiles, i // tiles_per_batch, n_batch), 0, 0)
    return idx


def _ffn_kernel(*refs, k, d_ff, final_norm, n_lat_tiles, split_input, n_mix, mix_has_ctx):
    mod_ref, g_ref, wgu_ref, wd_ref, fg_ref, o_ref = refs[-6:]
    is_lat = pl.program_id(0) < n_lat_tiles
    if split_input:
        h = jnp.where(is_lat, refs[0][...], refs[1][...])
    else:
        h = refs[0][...]
    if n_mix:
        mix_refs = refs[-7 - 2 * n_mix:-7]
        parts = []
        for lat_ref, ctx_ref in zip(mix_refs[0::2], mix_refs[1::2]):
            parts.append(jnp.where(is_lat, lat_ref[...], ctx_ref[...]) if mix_has_ctx else lat_ref[...])
        h = h + mod_ref[0, 5:6, :] * _dot(jnp.concatenate(parts, axis=1), refs[-7][...])
    xn = _modnorm(h, g_ref[...], mod_ref, k).astype(BF16)
    tf = d_ff // FFN_CHUNKS
    acc = None
    for c in range(FFN_CHUNKS):
        lo = c * tf
        g = _dot(xn, wgu_ref[:, lo:lo + tf])
        u = _dot(xn, wgu_ref[:, d_ff + lo:d_ff + lo + tf])
        a = (g * jax.nn.sigmoid(g) * u).astype(BF16)
        part = _dot(a, wd_ref[lo:lo + tf, :])
        acc = part if acc is None else acc + part
    gate = mod_ref[0, 3 * k + 2:3 * k + 3, :]
    out = h + (FFN_RES * gate) * acc
    if final_norm:
        out = _rms(out, fg_ref[...])
    o_ref[...] = out


def _ffn(h, mod, g, w_gu, w_down, final_g, *, layer, k, n_tiles, geom, final_norm=False, mix=(), w_out=None):
    split = isinstance(h, tuple)
    hs = h if split else (h,)
    D = hs[0].shape[1]
    d_ff = w_down.shape[1]
    tm = TOKEN_TILE
    n_lat_tiles = geom[0]

    def lat_idx(i):
        return (jnp.minimum(i, n_lat_tiles - 1), 0)

    def ctx_idx(i):
        return (jnp.maximum(i - n_lat_tiles, 0), 0)

    if split:
        h_specs = [pl.BlockSpec((tm, D), lat_idx), pl.BlockSpec((tm, D), ctx_idx)]
    else:
        h_specs = [pl.BlockSpec((tm, D), lambda i: (i, 0))]
    mix_args, mix_specs = [], []
    for lat, ctx in mix:
        mix_args += [lat, ctx]
        mix_specs += [pl.BlockSpec((tm, lat.shape[1]), lat_idx), pl.BlockSpec((tm, lat.shape[1]), ctx_idx)]
    if mix:
        mix_args.append(w_out)
        mix_specs.append(_layer_resident(w_out, layer))
    return pl.pallas_call(
        functools.partial(_ffn_kernel, k=k, d_ff=d_ff, final_norm=final_norm, n_lat_tiles=n_lat_tiles,
                          split_input=split, n_mix=len(mix), mix_has_ctx=n_tiles > n_lat_tiles),
        grid=(n_tiles,),
        in_specs=h_specs + mix_specs + [
            pl.BlockSpec((1, N_MOD, D), _mod_index(*geom)),
            _resident((1, D)),
            _layer_resident(w_gu, layer),
            _layer_resident(w_down, layer),
            _resident((1, D)),
        ],
        out_specs=pl.BlockSpec((tm, D), lambda i: (i, 0)),
        out_shape=jax.ShapeDtypeStruct((n_tiles * tm, D), F32),
        compiler_params=_cparams(("arbitrary",)),
        name="ffn",
    )(*hs, *mix_args, mod, g, w_gu, w_down, final_g)


def _rope(x, cos, sin):
    lane = lax.broadcasted_iota(jnp.int32, x.shape, 1)
    first = (lane & (HEAD_DIM - 1)) < (HEAD_DIM // 2)
    partner = jnp.where(first, pltpu.roll(x, LANES - HEAD_DIM // 2, 1), pltpu.roll(x, HEAD_DIM // 2, 1))
    return x * cos + partner * sin


def _inproj_kernel(h_ref, mod_ref, g_ref, w_ref, cos_ref, sin_ref, lru_ref, qk_ref, vt_ref):
    xn = _modnorm(h_ref[...], g_ref[...], mod_ref, 1).astype(BF16)
    y = _dot(xn, w_ref[...])
    lru_ref[...] = y[:, IN_LRU:IN_SQ]
    cos = cos_ref[...]
    sin = sin_ref[...]
    q_scale = HEAD_DIM ** -0.5 * LOG2E

    def roped(c0, scale=None):
        blk = _rope(y[:, c0:c0 + LANES], cos, sin)
        return blk if scale is None else blk * scale

    def put(c0, blk):
        qk_ref[:, c0:c0 + LANES] = blk.astype(BF16)

    qa, qb = roped(IN_SQ, q_scale), roped(IN_SQ + LANES, q_scale)
    low = lax.broadcasted_iota(jnp.int32, qa.shape, 1) < HEAD_DIM
    put(COL_SQ, jnp.where(low, qa, pltpu.roll(qb, HEAD_DIM, 1)))
    put(COL_SQ + LANES, jnp.where(low, pltpu.roll(qa, HEAD_DIM, 1), qb))
    put(COL_SK, roped(IN_SK))
    for c in range(0, DIFF_HEADS * 2 * HEAD_DIM, LANES):
        put(COL_DQ + c, roped(IN_DQ + c, q_scale))
        put(COL_DK + c, roped(IN_DK + c))
    sv_w = SWA_KV_HEADS * HEAD_DIM
    vt_ref[ROW_SV:ROW_SV + sv_w, :] = y[:, IN_SV:IN_SV + sv_w].T.astype(BF16)
    dv_w = DIFF_HEADS * 2 * HEAD_DIM
    vt_ref[ROW_DV:ROW_DV + dv_w, :] = y[:, IN_DV:IN_DV + dv_w].T.astype(BF16)


def _inproj(h, mod, g, w_in, cos_t, sin_t, *, layer, n_tiles, geom):
    T, D = h.shape
    tm = TOKEN_TILE
    n_lat_tiles, tiles_per_batch, _ = geom

    def tab_idx(i):
        return (jnp.where(i < n_lat_tiles, i % tiles_per_batch, tiles_per_batch), 0)

    return pl.pallas_call(
        _inproj_kernel,
        grid=(n_tiles,),
        in_specs=[
            pl.BlockSpec((tm, D), lambda i: (i, 0)),
            pl.BlockSpec((1, N_MOD, D), _mod_index(*geom)),
            _resident((1, D)),
            _layer_resident(w_in, layer),
            pl.BlockSpec((tm, LANES), tab_idx),
            pl.BlockSpec((tm, LANES), tab_idx),
        ],
        out_specs=[
            pl.BlockSpec((tm, 2 * LRU_WIDTH), lambda i: (i, 0)),
            pl.BlockSpec((tm, QK_WIDTH), lambda i: (i, 0)),
            pl.BlockSpec((VT_ROWS, tm), lambda i: (0, i)),
        ],
        out_shape=[
            jax.ShapeDtypeStruct((T, 2 * LRU_WIDTH), F32),
            jax.ShapeDtypeStruct((T, QK_WIDTH), BF16),
            jax.ShapeDtypeStruct((VT_ROWS, T), BF16),
        ],
        compiler_params=_cparams(("arbitrary",)),
        name="inproj",
    )(h, mod, g, w_in, cos_t, sin_t)


def _lru_kernel(lat_ref, ctx_ref, cw_ref, cb_ref, wr_ref, br_ref, wi_ref, bi_ref, lam_ref,
                ylat_ref, yctx_ref, xpad, u_scr, hf_scr, ge_scr, *g_scr, S, Lc):
    W = LRU_WIDTH
    TC = LRU_CHUNK
    PAD = SUBLANES
    row = lax.broadcasted_iota(jnp.int32, (TC, W), 0)
    group = lax.broadcasted_iota(jnp.int32, (TC // SUBLANES, W), 0)

    def conv_chunk(start):
        a = xpad[pl.ds(start, TC + 2 * PAD), :]
        u = cb_ref[...] + jnp.zeros((TC, W), F32)
        for k in range(CONV_WIDTH):
            sh = (CONV_LEFT - k) % (TC + 2 * PAD)
            r = a if sh == 0 else pltpu.roll(a, sh, 0)
            u = u + r[PAD:PAD + TC] * cw_ref[k:k + 1, :]
        return u

    def scan_chunk(u, d, carry, reverse):
        r = jax.nn.sigmoid(_dot(u, wr_ref[d]) + br_ref[d])
        i = jax.nn.sigmoid(_dot(u, wi_ref[d]) + bi_ref[d])
        log_a = (-LRU_C) * r * jax.nn.softplus(-lam_ref[d])
        A = jnp.exp(log_a)
        Bv = jnp.sqrt(_neg_expm1_2x(A, log_a)) * i * u

        def combine(A, Bv, pos, axis):
            n = A.shape[axis]
            s = 1
            while s < n:
                if reverse:
                    keep = pos < n - s
                    sh = n - s
                else:
                    keep = pos >= s
                    sh = s
                a_sh = jnp.where(keep, pltpu.roll(A, sh, axis), 1.0)
                b_sh = jnp.where(keep, pltpu.roll(Bv, sh, axis), 0.0)
                Bv = A * b_sh + Bv
                A = A * a_sh
                s *= 2
            return A, Bv

        G = SUBLANES
        ng = TC // G
        A, Bv = combine(A.reshape(ng, G, W), Bv.reshape(ng, G, W), lax.broadcasted_iota(jnp.int32, (ng, G, W), 1), 1)
        A = A.reshape(TC, W)
        Bv = Bv.reshape(TC, W)
        edge = 0 if reverse else G - 1

        def boundary_rows(planes, x):
            out = []
            for c, scr in enumerate(planes):
                scr[...] = x[:, c * LANES:(c + 1) * LANES]
                out.append(scr[pl.ds(edge, ng, stride=G), :])
            return jnp.concatenate(out, axis=1)

        n_pl = W // LANES
        Ag, Bg = combine(boundary_rows(g_scr[:n_pl], A), boundary_rows(g_scr[n_pl:], Bv), group, 0)
        ge_scr[...] = Ag * carry + Bg
        parts = []
        for g in range(ng):
            nb_g = g + 1 if reverse else g - 1
            h_in = carry if (nb_g < 0 or nb_g >= ng) else ge_scr[nb_g:nb_g + 1, :]
            parts.append(A[g * G:(g + 1) * G] * h_in + Bv[g * G:(g + 1) * G])
        h = jnp.concatenate(parts, axis=0)
        return h, (h[0:1] if reverse else h[TC - 1:TC])

    def gelu(x):
        return jax.nn.gelu(x)

    zero_pad = jnp.zeros((PAD, W), F32)
    zero_state = jnp.zeros((1, W), F32)

    xpad[0:PAD, :] = zero_pad
    xpad[PAD:PAD + Lc, :] = ctx_ref[:, 0:W]
    xpad[PAD + Lc:2 * PAD + Lc, :] = zero_pad
    uc = conv_chunk(0)
    hcf, carry_f = scan_chunk(uc, 0, zero_state, False)
    hcb, carry_b = scan_chunk(uc, 1, zero_state, True)
    yctx_ref[...] = ((hcf + hcb) * gelu(ctx_ref[:, W:2 * W])).astype(BF16)

    xpad[PAD:PAD + S, :] = lat_ref[:, 0:W]
    xpad[PAD + S:2 * PAD + S, :] = zero_pad
    nc = S // TC

    def conv_body(c, _):
        st = pl.multiple_of(c * TC, TC)
        u_scr[pl.ds(st, TC), :] = conv_chunk(st)
        return 0

    lax.fori_loop(0, nc, conv_body, 0)

    def fwd_body(c, carry):
        st = pl.multiple_of(c * TC, TC)
        h, carry = scan_chunk(u_scr[pl.ds(st, TC), :], 0, carry, False)
        hf_scr[pl.ds(st, TC), :] = h
        return carry

    lax.fori_loop(0, nc, fwd_body, carry_f)

    def bwd_body(c, carry):
        st = pl.multiple_of((nc - 1 - c) * TC, TC)
        h, carry = scan_chunk(u_scr[pl.ds(st, TC), :], 1, carry, True)
        y = (hf_scr[pl.ds(st, TC), :] + h) * gelu(lat_ref[pl.ds(st, TC), W:2 * W])
        ylat_ref[pl.ds(st, TC), :] = y.astype(BF16)
        return carry

    lax.fori_loop(0, nc, bwd_body, carry_b)


def _lru(lru_in, conv_w, conv_b, w_r, b_r, w_i, b_i, lam, *, B, S, Lc):
    W = LRU_WIDTH
    t_lat = B * S
    ctx_blk0 = t_lat // Lc
    return pl.pallas_call(
        functools.partial(_lru_kernel, S=S, Lc=Lc),
        grid=(B,),
        in_specs=[
            pl.BlockSpec((S, 2 * W), lambda b: (b, 0)),
            pl.BlockSpec((Lc, 2 * W), lambda b: (ctx_blk0 + b, 0)),
            _resident(conv_w.shape),
            _resident(conv_b.shape),
            _resident(w_r.shape),
            _resident(b_r.shape),
            _resident(w_i.shape),
            _resident(b_i.shape),
            _resident(lam.shape),
        ],
        out_specs=[
            pl.BlockSpec((S, W), lambda b: (b, 0)),
            pl.BlockSpec((Lc, W), lambda b: (b, 0)),
        ],
        out_shape=[
            jax.ShapeDtypeStruct((t_lat, W), BF16),
            jax.ShapeDtypeStruct((B * Lc, W), BF16),
        ],
        scratch_shapes=[
            pltpu.VMEM((S + 2 * SUBLANES, W), F32),
            pltpu.VMEM((S, W), F32),
            pltpu.VMEM((S, W), F32),
            pltpu.VMEM((LRU_CHUNK // SUBLANES, W), F32),
        ] + [pltpu.VMEM((LRU_CHUNK, LANES), F32)] * (2 * (W // LANES)) + [
        ],
        compiler_params=_cparams(("arbitrary",)),
        name="lru",
    )(lru_in, lru_in, conv_w, conv_b, w_r, b_r, w_i, b_i, lam)


def _swa_kernel(sink_ref, q_ref, k_ref, vt_ref, qc_ref, kc_ref, vtc_ref, o_ref, oc_ref,
                vt3, s_a, s_b, p_a, p_b, *, S, Lc):
    TQ = ATT_TQ
    KB = TQ + 2 * WINDOW
    NKB = KB + Lc
    hd = HEAD_DIM
    nh = SWA_HEADS
    n_lat_ch = S // LANES
    n_ctx_ch = Lc // LANES
    band_ch = KB // LANES
    ones = jnp.ones((ONES_ROWS, LANES), BF16)
    for c in range(n_lat_ch + n_ctx_ch):
        src = vt_ref[:, c * LANES:(c + 1) * LANES] if c < n_lat_ch else \
            vtc_ref[:, (c - n_lat_ch) * LANES:(c - n_lat_ch + 1) * LANES]
        vt3[c, 0:2 * hd, :] = src
        vt3[c, 2 * hd:2 * hd + ONES_ROWS, :] = ones
    low = lax.broadcasted_iota(jnp.int32, (TQ, LANES), 1) < hd
    col = lax.broadcasted_iota(jnp.int32, (1, nh * TQ), 1)
    sink = jnp.full((1, nh * TQ), sink_ref[0, SWA_HEAD_ORDER[-1]] * LOG2E, F32)
    for c in range(nh - 2, -1, -1):
        sink = jnp.where(col < (c + 1) * TQ, sink_ref[0, SWA_HEAD_ORDER[c]] * LOG2E, sink)

    def band_start(t):
        return jnp.clip(2 * t - 1, 0, n_lat_ch - band_ch)

    def scores(q, s_dst, t=None):
        zero = jnp.zeros((TQ, LANES), BF16)
        rows = []
        for grp in range(2):
            qg = q[:, grp * LANES:(grp + 1) * LANES]
            rows += [jnp.where(low, qg, zero), jnp.where(low, zero, qg)]
        q4 = jnp.concatenate(rows, axis=0)
        s_c = _dot_nt(kc_ref[...], q4)
        s_dst[KB:NKB, :] = s_c
        m = jnp.max(s_c, axis=0, keepdims=True)
        if t is not None:
            st = band_start(t) * LANES
            kb = k_ref[pl.ds(pl.multiple_of(st, LANES), KB), :]
            k_abs = st + lax.broadcasted_iota(jnp.int32, (KB, TQ), 0)
            q_abs = t * TQ + lax.broadcasted_iota(jnp.int32, (KB, TQ), 1)
            valid = jnp.abs(q_abs - k_abs) <= WINDOW
            s_l = _dot_nt(kb, q4)
            s_l = jnp.concatenate(
                [jnp.where(valid, s_l[:, c * TQ:(c + 1) * TQ], NEG_INF) for c in range(nh)], axis=1)
            s_dst[0:KB, :] = s_l
            m = jnp.maximum(m, jnp.max(s_l, axis=0, keepdims=True))
        return jnp.maximum(m, sink)

    def probs(s_src, m, p_dst, lo=0):
        p_dst[lo:NKB, :] = jnp.exp2(s_src[lo:NKB, :] - m).astype(BF16)

    def attend(p_src, m, t=None):
        ctx_v = [vt3[n_lat_ch + c] for c in range(n_ctx_ch)]
        if t is None:
            lo, vt = KB, jnp.concatenate(ctx_v, axis=1)
        else:
            st = band_start(t)
            lo, vt = 0, jnp.concatenate([vt3[st + d] for d in range(band_ch)] + ctx_v, axis=1)
        acc = _dot(vt, p_src[lo:NKB, :])
        den = acc[2 * hd:2 * hd + 1] + jnp.exp2(sink - m)
        o_n = acc[0:2 * hd] * (1.0 / den)
        pieces = []
        for h in range(nh):
            c = SWA_HEAD_ORDER.index(h)
            pieces.append(o_n[(c % 2) * hd:(c % 2 + 1) * hd, c * TQ:(c + 1) * TQ])
        return jnp.concatenate(pieces, axis=0).T.astype(BF16)

    def q_block(i):
        return q_ref[pl.ds(pl.multiple_of(i * TQ, TQ), TQ), :]

    def emit(i, y):
        o_ref[pl.ds(pl.multiple_of(i * TQ, TQ), TQ), :] = y

    nb = S // TQ
    m_a = scores(q_block(0), s_a, 0)
    m_b = scores(q_block(1), s_b, 1)
    probs(s_a, m_a, p_a)

    def pair(j, carry):
        m_b, m_pa = carry
        b = 2 * j
        m_a = scores(q_block(b + 2), s_a, b + 2)
        probs(s_b, m_b, p_b)
        emit(b, attend(p_a, m_pa, b))
        m_b2 = scores(q_block(b + 3), s_b, b + 3)
        probs(s_a, m_a, p_a)
        emit(b + 1, attend(p_b, m_b, b + 1))
        return m_b2, m_a

    m_b, m_pa = lax.fori_loop(0, nb // 2 - 1, pair, (m_b, m_a))
    probs(s_b, m_b, p_b)
    emit(nb - 2, attend(p_a, m_pa, nb - 2))
    emit(nb - 1, attend(p_b, m_b, nb - 1))

    m_c = scores(qc_ref[...], s_a)
    probs(s_a, m_c, p_a, KB)
    oc_ref[...] = attend(p_a, m_c)


def _swa(qk, vt, sink, *, B, S, Lc):
    qw = SWA_HEADS * HEAD_DIM
    ctx_blk0 = B * S // Lc
    sv_blk = ROW_SV // LANES
    NKB = ATT_TQ + 2 * WINDOW + Lc
    return pl.pallas_call(
        functools.partial(_swa_kernel, S=S, Lc=Lc),
        grid=(B,),
        in_specs=[
            pl.BlockSpec(memory_space=pltpu.SMEM),
            pl.BlockSpec((S, qw), lambda b: (b, COL_SQ // qw)),
            pl.BlockSpec((S, LANES), lambda b: (b, COL_SK // LANES)),
            pl.BlockSpec((LANES, S), lambda b: (sv_blk, b)),
            pl.BlockSpec((Lc, qw), lambda b: (ctx_blk0 + b, COL_SQ // qw)),
            pl.BlockSpec((Lc, LANES), lambda b: (ctx_blk0 + b, COL_SK // LANES)),
            pl.BlockSpec((LANES, Lc), lambda b: (sv_blk, ctx_blk0 + b)),
        ],
        out_specs=[
            pl.BlockSpec((S, qw), lambda b: (b, 0)),
            pl.BlockSpec((Lc, qw), lambda b: (b, 0)),
        ],
        out_shape=[
            jax.ShapeDtypeStruct((B * S, qw), BF16),
            jax.ShapeDtypeStruct((B * Lc, qw), BF16),
        ],
        scratch_shapes=[
            pltpu.VMEM(((S + Lc) // LANES, 2 * HEAD_DIM + ONES_ROWS, LANES), BF16),
            pltpu.VMEM((NKB, SWA_HEADS * ATT_TQ), F32),
            pltpu.VMEM((NKB, SWA_HEADS * ATT_TQ), F32),
            pltpu.VMEM((NKB, SWA_HEADS * ATT_TQ), BF16),
            pltpu.VMEM((NKB, SWA_HEADS * ATT_TQ), BF16),
        ],
        compiler_params=_cparams(("arbitrary",)),
        name="swa",
    )(sink, qk, qk, vt, qk, qk, vt)


def _diff_kernel(dl_ref, g_ref, q_ref, k_ref, vt_ref, qc_ref, kc_ref, vtc_ref, o_ref, oc_ref,
                 vt_scr, s_a, s_b, p_a, p_b, *, S, Lc, lambda_init):
    TQ = ATT_TQ
    NK = S + Lc
    dv = 2 * HEAD_DIM
    nhs = DIFF_HEADS_STEP
    for hi in range(nhs):
        vt_scr[hi, 0:dv, 0:S] = vt_ref[hi * dv:(hi + 1) * dv, :]
        vt_scr[hi, 0:dv, S:NK] = vtc_ref[hi * dv:(hi + 1) * dv, :]
        vt_scr[hi, dv:dv + ONES_ROWS, :] = jnp.ones((ONES_ROWS, NK), BF16)
    dl = dl_ref[...]
    lam = (jnp.exp(jnp.sum(dl[0:1] * dl[1:2], axis=-1, keepdims=True))
           - jnp.exp(jnp.sum(dl[2:3] * dl[3:4], axis=-1, keepdims=True)) + lambda_init)
    low = lax.broadcasted_iota(jnp.int32, (TQ, LANES), 1) < HEAD_DIM
    lat_pieces = [(r, r + DIFF_CHUNK, k_ref, r) for r in range(0, S, DIFF_CHUNK)]
    ctx_piece = (S, NK, kc_ref, 0)

    def head_lanes(hi):
        return slice(hi * LANES, (hi + 1) * LANES)

    def scores(q, hi, s_dst, lo=0):
        zero = jnp.zeros_like(q)
        q2 = jnp.concatenate([jnp.where(low, q, zero), jnp.where(low, zero, q)], axis=0)
        m_acc = None
        for r0, r1, kref, off in ([] if lo else lat_pieces) + [ctx_piece]:
            blk = _dot_nt(kref[off:off + (r1 - r0), head_lanes(hi)], q2)
            s_dst[r0:r1, :] = blk
            for r in range(0, r1 - r0, DIFF_ROWS):
                part = blk[r:r + DIFF_ROWS]
                m_acc = part if m_acc is None else jnp.maximum(m_acc, part)
        return jnp.max(m_acc, axis=0, keepdims=True)

    def probs(s_src, m, p_dst, lo=0):
        p_dst[lo:NK, :] = jnp.exp2(s_src[lo:NK, :] - m).astype(BF16)

    def attend(p_src, hi, lo=0):
        acc = _dot(vt_scr[hi, :, lo:NK], p_src[lo:NK, :])
        o_n = acc[0:dv] * (1.0 / acc[dv:dv + 1])
        o_t = o_n[:, 0:TQ] - lam * o_n[:, TQ:2 * TQ]
        ms = jnp.mean(o_t * o_t, axis=0, keepdims=True)
        y_t = o_t * (lax.rsqrt(ms + EPS) * (1.0 - lambda_init))
        return (y_t.T * g_ref[...]).astype(BF16)

    def q_block(hi, i):
        return q_ref[pl.ds(pl.multiple_of(i * TQ, TQ), TQ), head_lanes(hi)]

    def emit(hi, i, y):
        o_ref[pl.ds(pl.multiple_of(i * TQ, TQ), TQ), head_lanes(hi)] = y

    nb = S // TQ
    m_a = scores(q_block(0, 0), 0, s_a)
    m_b = scores(q_block(0, 1), 0, s_b)
    probs(s_a, m_a, p_a)
    for hi in range(nhs):
        def pair(j, m_b, hi=hi):
            b = 2 * j
            m_a = scores(q_block(hi, b + 2), hi, s_a)
            probs(s_b, m_b, p_b)
            emit(hi, b, attend(p_a, hi))
            m_b = scores(q_block(hi, b + 3), hi, s_b)
            probs(s_a, m_a, p_a)
            emit(hi, b + 1, attend(p_b, hi))
            return m_b

        m_b = lax.fori_loop(0, nb // 2 - 1, pair, m_b)
        if hi + 1 < nhs:
            m_a = scores(q_block(hi + 1, 0), hi + 1, s_a)
            probs(s_b, m_b, p_b)
            emit(hi, nb - 2, attend(p_a, hi))
            m_b = scores(q_block(hi + 1, 1), hi + 1, s_b)
            probs(s_a, m_a, p_a)
            emit(hi, nb - 1, attend(p_b, hi))
        else:
            probs(s_b, m_b, p_b)
            emit(hi, nb - 2, attend(p_a, hi))
            emit(hi, nb - 1, attend(p_b, hi))

    for hi in range(nhs):
        m_c = scores(qc_ref[:, head_lanes(hi)], hi, s_a, S)
        probs(s_a, m_c, p_a, S)
        oc_ref[:, head_lanes(hi)] = attend(p_a, hi, S)


def _diff(qk, vt, diff_lambda, subln_g, *, B, S, Lc, lambda_init):
    H = DIFF_HEADS
    dv = 2 * HEAD_DIM
    nhs = DIFF_HEADS_STEP
    NK = S + Lc
    ctx_blk0 = B * S // Lc
    w = nhs * LANES
    return pl.pallas_call(
        functools.partial(_diff_kernel, S=S, Lc=Lc, lambda_init=lambda_init),
        grid=(B, H // nhs),
        in_specs=[
            _resident(diff_lambda.shape),
            _resident(subln_g.shape),
            pl.BlockSpec((S, w), lambda b, h: (b, COL_DQ // w + h)),
            pl.BlockSpec((S, w), lambda b, h: (b, COL_DK // w + h)),
            pl.BlockSpec((w, S), lambda b, h: (ROW_DV // w + h, b)),
            pl.BlockSpec((Lc, w), lambda b, h: (ctx_blk0 + b, COL_DQ // w + h)),
            pl.BlockSpec((Lc, w), lambda b, h: (ctx_blk0 + b, COL_DK // w + h)),
            pl.BlockSpec((w, Lc), lambda b, h: (ROW_DV // w + h, ctx_blk0 + b)),
        ],
        out_specs=[
            pl.BlockSpec((S, w), lambda b, h: (b, h)),
            pl.BlockSpec((Lc, w), lambda b, h: (b, h)),
        ],
        out_shape=[
            jax.ShapeDtypeStruct((B * S, H * dv), BF16),
            jax.ShapeDtypeStruct((B * Lc, H * dv), BF16),
        ],
        scratch_shapes=[
            pltpu.VMEM((nhs, dv + ONES_ROWS, NK), BF16),
            pltpu.VMEM((NK, 2 * ATT_TQ), F32),
            pltpu.VMEM((NK, 2 * ATT_TQ), F32),
            pltpu.VMEM((NK, 2 * ATT_TQ), BF16),
            pltpu.VMEM((NK, 2 * ATT_TQ), BF16),
        ],
        compiler_params=_cparams(("arbitrary", "arbitrary")),
        name="diff",
    )(diff_lambda, subln_g, qk, qk, vt, qk, qk, vt)


def _rope_tables(S, tm):
    rows = S // GRID_W
    row = jnp.repeat(jnp.arange(rows, dtype=F32), GRID_W)
    col = jnp.tile(jnp.arange(GRID_W, dtype=F32), rows)
    n_freq = HEAD_DIM // 4
    inv_freq = ROPE_BASE ** (-jnp.arange(n_freq, dtype=F32) / n_freq)
    ang = jnp.concatenate([row[:, None] * inv_freq, col[:, None] * inv_freq], axis=-1)
    cos, sin = jnp.cos(ang), jnp.sin(ang)
    reps = LANES // (HEAD_DIM // 2)
    sign = np.tile(np.concatenate([-np.ones(HEAD_DIM // 2), np.ones(HEAD_DIM // 2)]), LANES // HEAD_DIM)
    cos_t = jnp.concatenate([jnp.tile(cos, (1, reps)), jnp.ones((tm, LANES), F32)], axis=0)
    sin_t = jnp.concatenate([jnp.tile(sin, (1, reps)) * sign.astype(np.float32), jnp.zeros((tm, LANES), F32)], axis=0)
    return cos_t, sin_t


def _block_diag(w):
    nd, K, c, _ = w.shape
    eye = jnp.eye(K, dtype=w.dtype)
    return jnp.einsum('dkij,kl->dkilj', w, eye).reshape(nd, K * c, K * c)


def kernel(x, c, ctx, c_ctx, w_ada, b_ada, norm_g, ffn1_w_gu, ffn1_w_down, ffn2_w_gu, ffn2_w_down,
           w_in, w_out, conv_w, conv_b, lru_w_r, lru_b_r, lru_w_i, lru_b_i, lru_lambda,
           swa_sink, diff_lambda, diff_subln_g, final_g):
    B, S, D = x.shape
    Lc = ctx.shape[1]
    depth = w_ada.shape[0]
    tm = TOKEN_TILE
    assert S % tm == 0 and (B * Lc) % tm == 0 and Lc == ATT_TQ and S % GRID_W == 0 and S % LRU_CHUNK == 0
    assert Lc == LRU_CHUNK and S >= ATT_TQ + 2 * WINDOW
    assert S % DIFF_CHUNK == 0 and (S // ATT_TQ) % 2 == 0
    n_lat_tiles = B * S // tm
    n_tiles = n_lat_tiles + B * Lc // tm
    geom = (n_lat_tiles, S // tm, B)

    h = (x.reshape(B * S, D), ctx.reshape(B * Lc, D))

    n_cond = B + 1
    pad = (-n_cond) % SUBLANES
    cond = jnp.concatenate([c, c_ctx[None], jnp.zeros((pad, D), F32)], axis=0)
    mod_all = _ada(cond, w_ada, b_ada).reshape(depth, n_cond + pad, N_MOD, D)

    cos_t, sin_t = _rope_tables(S, tm)
    w1_gu, w1_down = ffn1_w_gu.astype(BF16), ffn1_w_down.astype(BF16)
    w2_gu, w2_down = ffn2_w_gu.astype(BF16), ffn2_w_down.astype(BF16)
    w_in_b, w_out_b = w_in.astype(BF16), w_out.astype(BF16)

    for l in range(depth):
        last = l == depth - 1
        mod = mod_all[l]
        lambda_init = 0.8 - 0.6 * math.exp(-0.3 * l)
        g = norm_g[l]
        h = _ffn(h, mod, g[0:1], w1_gu, w1_down, final_g[None], layer=l, k=0, n_tiles=n_tiles, geom=geom)
        lru_in, qk, vt = _inproj(h, mod, g[1:2], w_in_b, cos_t, sin_t, layer=l, n_tiles=n_tiles, geom=geom)
        y_lru = _lru(lru_in, conv_w[l], conv_b[l][None], _block_diag(lru_w_r[l]), lru_b_r[l][:, None],
                     _block_diag(lru_w_i[l]), lru_b_i[l][:, None], lru_lambda[l][:, None], B=B, S=S, Lc=Lc)
        y_swa = _swa(qk, vt, swa_sink[l][None], B=B, S=S, Lc=Lc)
        y_diff = _diff(qk, vt, diff_lambda[l], diff_subln_g[l][None], B=B, S=S, Lc=Lc, lambda_init=lambda_init)
        n_out = n_lat_tiles if last else n_tiles
        h = _ffn(h, mod, g[2:3], w2_gu, w2_down, final_g[None], layer=l, k=2, n_tiles=n_out, geom=geom,
                 final_norm=last, mix=(y_lru, y_swa, y_diff), w_out=w_out_b)
    return h[:B * S].reshape(B, S, D)
```

```python
import functools
import math

import numpy as np
import jax
import jax.numpy as jnp
from jax import lax
from jax.experimental import pallas as pl
from jax.experimental.pallas import tpu as pltpu

F32 = jnp.float32
BF16 = jnp.bfloat16

GRID_W = 64
N_MOD = 9
EPS = 1e-6
FFN_RES = 0.5
HEAD_DIM = 64
ROPE_BASE = 10000.0
LRU_WIDTH = 256
LRU_BLOCKS = 4
CONV_WIDTH = 4
CONV_LEFT = 2
LRU_C = 8.0
SWA_HEADS = 4
SWA_KV_HEADS = 2
WINDOW = 128
DIFF_HEADS = 4
NEG_INF = -1e30

LANES = 128
SUBLANES = 8
VMEM_LIMIT_BYTES = 56 * 1024 * 1024

TOKEN_TILE = 512
FFN_CHUNKS = 1
ADA_COLS = 1024
ATT_TQ = 256
LRU_CHUNK = 256

QK_WIDTH = 1408
COL_SQ, COL_DQ, COL_DK, COL_SK = 0, 256, 768, 1280
VT_ROWS = 640
ROW_DV, ROW_SV = 0, 512
IN_LRU, IN_SQ, IN_SK, IN_SV, IN_DQ, IN_DK, IN_DV = 0, 512, 768, 896, 1024, 1536, 2048
LOG2E = math.log2(math.e)
DIFF_ROWS = 16
DIFF_CHUNK = 1024
DIFF_HEADS_STEP = 2
DIFF_PAIRS_PER_TRIP = 1
ONES_ROWS = 16
SWA_HEAD_ORDER = (0, 2, 1, 3)


def _cparams(semantics):
    return pltpu.CompilerParams(dimension_semantics=semantics, vmem_limit_bytes=VMEM_LIMIT_BYTES)


def _resident(shape):
    nd = len(shape)
    return pl.BlockSpec(shape, lambda *_: (0,) * nd, pipeline_mode=pl.Buffered(1))


def _layer_resident(stacked, layer):
    nd = stacked.ndim - 1
    return pl.BlockSpec((None,) + stacked.shape[1:], lambda *_: (layer,) + (0,) * nd, pipeline_mode=pl.Buffered(1))


def _dot(a, b):
    return jnp.dot(a, b, preferred_element_type=F32)


def _dot_nt(a, b):
    return lax.dot_general(a, b, (((1,), (1,)), ((), ())), preferred_element_type=F32)


def _rms(x, g):
    return x * lax.rsqrt(jnp.mean(x * x, axis=-1, keepdims=True) + EPS) * g


def _modnorm(h, g, mod_ref, k):
    shift = mod_ref[0, 3 * k:3 * k + 1, :]
    scale = mod_ref[0, 3 * k + 1:3 * k + 2, :]
    return _rms(h, g) * (1.0 + scale) + shift


def _neg_expm1_2x(a, x):
    return (1.0 + a * a) * jnp.tanh(-x)


def _ada_kernel(c_ref, w_ref, b_ref, o_ref):
    c = c_ref[...]
    s = (c * jax.nn.sigmoid(c)).astype(BF16)
    o_ref[0] = _dot(s, w_ref[0].astype(BF16)) + b_ref[0]


def _ada(cond, w_ada, b_ada):
    L, D, N = w_ada.shape
    R = cond.shape[0]
    return pl.pallas_call(
        _ada_kernel,
        grid=(L, N // ADA_COLS),
        in_specs=[
            pl.BlockSpec((R, D), lambda l, n: (0, 0)),
            pl.BlockSpec((1, D, ADA_COLS), lambda l, n: (l, 0, n)),
            pl.BlockSpec((1, 1, ADA_COLS), lambda l, n: (l, 0, n)),
        ],
        out_specs=pl.BlockSpec((1, R, ADA_COLS), lambda l, n: (l, 0, n)),
        out_shape=jax.ShapeDtypeStruct((L, R, N), F32),
        compiler_params=_cparams(("arbitrary", "arbitrary")),
        name="ada",
    )(cond, w_ada, b_ada.reshape(L, 1, N))


def _mod_index(n_lat_tiles, tiles_per_batch, n_batch):
    def idx(i):
        return (jnp.where(i < n_lat_tiles, i // tiles_per_batch, n_batch), 0, 0)
    return idx


def _ffn_kernel(*refs, k, d_ff, final_norm, n_lat_tiles, split_input, n_mix, mix_has_ctx):
    if n_mix:
        wo_bf = refs[-1]
        refs = refs[:-1]

        @pl.when(pl.program_id(0) == 0)
        def _():
            wo_bf[...] = refs[-7][...].astype(BF16)

    mod_ref, g_ref, wgu_ref, wd_ref, fg_ref, o_ref = refs[-6:]
    is_lat = pl.program_id(0) < n_lat_tiles
    if split_input:
        h = jnp.where(is_lat, refs[0][...], refs[1][...])
    else:
        h = refs[0][...]
    if n_mix:
        mix_refs = refs[-7 - 2 * n_mix:-7]
        parts = []
        for lat_ref, ctx_ref in zip(mix_refs[0::2], mix_refs[1::2]):
            parts.append(jnp.where(is_lat, lat_ref[...], ctx_ref[...]) if mix_has_ctx else lat_ref[...])
        h = h + mod_ref[0, 5:6, :] * _dot(jnp.concatenate(parts, axis=1), wo_bf[...])
    xn = _modnorm(h, g_ref[...], mod_ref, k).astype(BF16)
    tf = d_ff // FFN_CHUNKS
    acc = None
    for c in range(FFN_CHUNKS):
        lo = c * tf
        g = _dot(xn, wgu_ref[:, lo:lo + tf])
        u = _dot(xn, wgu_ref[:, d_ff + lo:d_ff + lo + tf])
        a = (g * jax.nn.sigmoid(g) * u).astype(BF16)
        part = _dot(a, wd_ref[lo:lo + tf, :])
        acc = part if acc is None else acc + part
    gate = mod_ref[0, 3 * k + 2:3 * k + 3, :]
    out = h + (FFN_RES * gate) * acc
    if final_norm:
        out = _rms(out, fg_ref[...])
    o_ref[...] = out


def _ffn(h, mod, g, w_gu, w_down, final_g, *, layer, k, n_tiles, geom, final_norm=False, mix=(), w_out=None):
    split = isinstance(h, tuple)
    hs = h if split else (h,)
    D = hs[0].shape[1]
    d_ff = w_down.shape[1]
    tm = TOKEN_TILE
    n_lat_tiles = geom[0]

    def lat_idx(i):
        return (jnp.minimum(i, n_lat_tiles - 1), 0)

    def ctx_idx(i):
        return (jnp.maximum(i - n_lat_tiles, 0), 0)

    if split:
        h_specs = [pl.BlockSpec((tm, D), lat_idx), pl.BlockSpec((tm, D), ctx_idx)]
    else:
        h_specs = [pl.BlockSpec((tm, D), lambda i: (i, 0))]
    mix_args, mix_specs = [], []
    for lat, ctx in mix:
        mix_args += [lat, ctx]
        mix_specs += [pl.BlockSpec((tm, lat.shape[1]), lat_idx), pl.BlockSpec((tm, lat.shape[1]), ctx_idx)]
    if mix:
        mix_args.append(w_out)
        mix_specs.append(_layer_resident(w_out, layer))
    return pl.pallas_call(
        functools.partial(_ffn_kernel, k=k, d_ff=d_ff, final_norm=final_norm, n_lat_tiles=n_lat_tiles,
                          split_input=split, n_mix=len(mix), mix_has_ctx=n_tiles > n_lat_tiles),
        grid=(n_tiles,),
        in_specs=h_specs + mix_specs + [
            pl.BlockSpec((1, N_MOD, D), _mod_index(*geom)),
            _resident((1, D)),
            _layer_resident(w_gu, layer),
            _layer_resident(w_down, layer),
            _resident((1, D)),
        ],
        out_specs=pl.BlockSpec((tm, D), lambda i: (i, 0)),
        out_shape=jax.ShapeDtypeStruct((n_tiles * tm, D), F32),
        scratch_shapes=[pltpu.VMEM(w_out.shape[1:], BF16)] if mix else [],
        compiler_params=_cparams(("arbitrary",)),
        name="ffn",
    )(*hs, *mix_args, mod, g, w_gu, w_down, final_g)


def _rope(x, cos, sin):
    lane = lax.broadcasted_iota(jnp.int32, x.shape, 1)
    first = (lane & (HEAD_DIM - 1)) < (HEAD_DIM // 2)
    partner = jnp.where(first, pltpu.roll(x, LANES - HEAD_DIM // 2, 1), pltpu.roll(x, HEAD_DIM // 2, 1))
    return x * cos + partner * sin


def _inproj_kernel(h_ref, mod_ref, g_ref, w_ref, cos_ref, sin_ref, lru_ref, qk_ref, vt_ref, w_bf):
    @pl.when(pl.program_id(0) == 0)
    def _():
        w_bf[...] = w_ref[...].astype(BF16)

    xn = _modnorm(h_ref[...], g_ref[...], mod_ref, 1).astype(BF16)
    y = _dot(xn, w_bf[...])
    lru_ref[...] = y[:, IN_LRU:IN_SQ]
    cos = cos_ref[...]
    sin = sin_ref[...]
    q_scale = HEAD_DIM ** -0.5 * LOG2E

    def roped(c0, scale=None):
        blk = _rope(y[:, c0:c0 + LANES], cos, sin)
        return blk if scale is None else blk * scale

    def put(c0, blk):
        qk_ref[:, c0:c0 + LANES] = blk.astype(BF16)

    qa, qb = roped(IN_SQ, q_scale), roped(IN_SQ + LANES, q_scale)
    low = lax.broadcasted_iota(jnp.int32, qa.shape, 1) < HEAD_DIM
    put(COL_SQ, jnp.where(low, qa, pltpu.roll(qb, HEAD_DIM, 1)))
    put(COL_SQ + LANES, jnp.where(low, pltpu.roll(qa, HEAD_DIM, 1), qb))
    put(COL_SK, roped(IN_SK))
    for c in range(0, DIFF_HEADS * 2 * HEAD_DIM, LANES):
        put(COL_DQ + c, roped(IN_DQ + c, q_scale))
        put(COL_DK + c, roped(IN_DK + c))
    sv_w = SWA_KV_HEADS * HEAD_DIM
    vt_ref[ROW_SV:ROW_SV + sv_w, :] = y[:, IN_SV:IN_SV + sv_w].T.astype(BF16)
    dv_w = DIFF_HEADS * 2 * HEAD_DIM
    vt_ref[ROW_DV:ROW_DV + dv_w, :] = y[:, IN_DV:IN_DV + dv_w].T.astype(BF16)


def _inproj_tile(S, n_ctx_rows):
    big = 2 * TOKEN_TILE
    return big if S % big == 0 and n_ctx_rows % big == 0 else TOKEN_TILE


def _inproj(h, mod, g, w_in, cos_t, sin_t, *, layer, B, S):
    T, D = h.shape
    tm = _inproj_tile(S, T - B * S)
    n_tiles = T // tm
    geom = (B * S // tm, S // tm, B)
    n_lat_tiles, tiles_per_batch, _ = geom

    def tab_idx(i):
        return (jnp.where(i < n_lat_tiles, i % tiles_per_batch, tiles_per_batch), 0)

    return pl.pallas_call(
        _inproj_kernel,
        grid=(n_tiles,),
        in_specs=[
            pl.BlockSpec((tm, D), lambda i: (i, 0)),
            pl.BlockSpec((1, N_MOD, D), _mod_index(*geom)),
            _resident((1, D)),
            _layer_resident(w_in, layer),
            pl.BlockSpec((tm, LANES), tab_idx),
            pl.BlockSpec((tm, LANES), tab_idx),
        ],
        out_specs=[
            pl.BlockSpec((tm, 2 * LRU_WIDTH), lambda i: (i, 0)),
            pl.BlockSpec((tm, QK_WIDTH), lambda i: (i, 0)),
            pl.BlockSpec((VT_ROWS, tm), lambda i: (0, i)),
        ],
        out_shape=[
            jax.ShapeDtypeStruct((T, 2 * LRU_WIDTH), F32),
            jax.ShapeDtypeStruct((T, QK_WIDTH), BF16),
            jax.ShapeDtypeStruct((VT_ROWS, T), BF16),
        ],
        scratch_shapes=[pltpu.VMEM(w_in.shape[1:], BF16)],
        compiler_params=_cparams(("arbitrary",)),
        name="inproj",
    )(h, mod, g, w_in, cos_t, sin_t)


def _lru_kernel(lat_ref, ctx_ref, cw_ref, cb_ref, wr_ref, br_ref, wi_ref, bi_ref, lam_ref,
                ylat_ref, yctx_ref, xpad, u_scr, hf_scr, ge_scr, *g_scr, S, Lc):
    W = LRU_WIDTH
    TC = LRU_CHUNK
    PAD = SUBLANES
    row = lax.broadcasted_iota(jnp.int32, (TC, W), 0)
    group = lax.broadcasted_iota(jnp.int32, (TC // SUBLANES, W), 0)

    def conv_chunk(start):
        a = xpad[pl.ds(start, TC + 2 * PAD), :]
        u = cb_ref[...] + jnp.zeros((TC, W), F32)
        for k in range(CONV_WIDTH):
            sh = (CONV_LEFT - k) % (TC + 2 * PAD)
            r = a if sh == 0 else pltpu.roll(a, sh, 0)
            u = u + r[PAD:PAD + TC] * cw_ref[k:k + 1, :]
        return u

    def scan_chunk(u, d, carry, reverse):
        r = jax.nn.sigmoid(_dot(u, wr_ref[d]) + br_ref[d])
        i = jax.nn.sigmoid(_dot(u, wi_ref[d]) + bi_ref[d])
        log_a = (-LRU_C) * r * jax.nn.softplus(-lam_ref[d])
        A = jnp.exp(log_a)
        Bv = jnp.sqrt(_neg_expm1_2x(A, log_a)) * i * u

        def combine(A, Bv, pos, axis):
            n = A.shape[axis]
            s = 1
            while s < n:
                if reverse:
                    keep = pos < n - s
                    sh = n - s
                else:
                    keep = pos >= s
                    sh = s
                a_sh = jnp.where(keep, pltpu.roll(A, sh, axis), 1.0)
                b_sh = jnp.where(keep, pltpu.roll(Bv, sh, axis), 0.0)
                Bv = A * b_sh + Bv
                A = A * a_sh
                s *= 2
            return A, Bv

        G = SUBLANES
        ng = TC // G
        A, Bv = combine(A.reshape(ng, G, W), Bv.reshape(ng, G, W), lax.broadcasted_iota(jnp.int32, (ng, G, W), 1), 1)
        A = A.reshape(TC, W)
        Bv = Bv.reshape(TC, W)
        edge = 0 if reverse else G - 1

        def boundary_rows(planes, x):
            out = []
            for c, scr in enumerate(planes):
                scr[...] = x[:, c * LANES:(c + 1) * LANES]
                out.append(scr[pl.ds(edge, ng, stride=G), :])
            return jnp.concatenate(out, axis=1)

        n_pl = W // LANES
        Ag, Bg = combine(boundary_rows(g_scr[:n_pl], A), boundary_rows(g_scr[n_pl:], Bv), group, 0)
        ge_scr[...] = Ag * carry + Bg
        parts = []
        for g in range(ng):
            nb_g = g + 1 if reverse else g - 1
            h_in = carry if (nb_g < 0 or nb_g >= ng) else ge_scr[nb_g:nb_g + 1, :]
            parts.append(A[g * G:(g + 1) * G] * h_in + Bv[g * G:(g + 1) * G])
        h = jnp.concatenate(parts, axis=0)
        return h, (h[0:1] if reverse else h[TC - 1:TC])

    def gelu(x):
        return jax.nn.gelu(x)

    zero_pad = jnp.zeros((PAD, W), F32)
    zero_state = jnp.zeros((1, W), F32)

    xpad[0:PAD, :] = zero_pad
    xpad[PAD:PAD + Lc, :] = ctx_ref[:, 0:W]
    xpad[PAD + Lc:2 * PAD + Lc, :] = zero_pad
    uc = conv_chunk(0)
    hcf, carry_f = scan_chunk(uc, 0, zero_state, False)
    hcb, carry_b = scan_chunk(uc, 1, zero_state, True)
    yctx_ref[...] = ((hcf + hcb) * gelu(ctx_ref[:, W:2 * W])).astype(BF16)

    xpad[PAD:PAD + S, :] = lat_ref[:, 0:W]
    xpad[PAD + S:2 * PAD + S, :] = zero_pad
    nc = S // TC

    def conv_body(c, _):
        st = pl.multiple_of(c * TC, TC)
        u_scr[pl.ds(st, TC), :] = conv_chunk(st)
        return 0

    lax.fori_loop(0, nc, conv_body, 0)

    def fwd_body(c, carry):
        st = pl.multiple_of(c * TC, TC)
        h, carry = scan_chunk(u_scr[pl.ds(st, TC), :], 0, carry, False)
        hf_scr[pl.ds(st, TC), :] = h
        return carry

    lax.fori_loop(0, nc, fwd_body, carry_f)

    def bwd_body(c, carry):
        st = pl.multiple_of((nc - 1 - c) * TC, TC)
        h, carry = scan_chunk(u_scr[pl.ds(st, TC), :], 1, carry, True)
        y = (hf_scr[pl.ds(st, TC), :] + h) * gelu(lat_ref[pl.ds(st, TC), W:2 * W])
        ylat_ref[pl.ds(st, TC), :] = y.astype(BF16)
        return carry

    lax.fori_loop(0, nc, bwd_body, carry_b)


def _lru(lru_in, conv_w, conv_b, w_r, b_r, w_i, b_i, lam, *, B, S, Lc):
    W = LRU_WIDTH
    t_lat = B * S
    ctx_blk0 = t_lat // Lc
    return pl.pallas_call(
        functools.partial(_lru_kernel, S=S, Lc=Lc),
        grid=(B,),
        in_specs=[
            pl.BlockSpec((S, 2 * W), lambda b: (b, 0)),
            pl.BlockSpec((Lc, 2 * W), lambda b: (ctx_blk0 + b, 0)),
            _resident(conv_w.shape),
            _resident(conv_b.shape),
            _resident(w_r.shape),
            _resident(b_r.shape),
            _resident(w_i.shape),
            _resident(b_i.shape),
            _resident(lam.shape),
        ],
        out_specs=[
            pl.BlockSpec((S, W), lambda b: (b, 0)),
            pl.BlockSpec((Lc, W), lambda b: (b, 0)),
        ],
        out_shape=[
            jax.ShapeDtypeStruct((t_lat, W), BF16),
            jax.ShapeDtypeStruct((B * Lc, W), BF16),
        ],
        scratch_shapes=[
            pltpu.VMEM((S + 2 * SUBLANES, W), F32),
            pltpu.VMEM((S, W), F32),
            pltpu.VMEM((S, W), F32),
            pltpu.VMEM((LRU_CHUNK // SUBLANES, W), F32),
        ] + [pltpu.VMEM((LRU_CHUNK, LANES), F32)] * (2 * (W // LANES)) + [
        ],
        compiler_params=_cparams(("arbitrary",)),
        name="lru",
    )(lru_in, lru_in, conv_w, conv_b, w_r, b_r, w_i, b_i, lam)


def _swa_kernel(sink_ref, q_ref, k_ref, vt_ref, qc_ref, kc_ref, vtc_ref, o_ref, oc_ref,
                vt3, s_a, s_b, p_a, p_b, *, S, Lc):
    TQ = ATT_TQ
    KB = TQ + 2 * WINDOW
    NKB = KB + Lc
    hd = HEAD_DIM
    nh = SWA_HEADS
    n_lat_ch = S // LANES
    n_ctx_ch = Lc // LANES
    band_ch = KB // LANES
    ones = jnp.ones((ONES_ROWS, LANES), BF16)
    for c in range(n_lat_ch + n_ctx_ch):
        src = vt_ref[:, c * LANES:(c + 1) * LANES] if c < n_lat_ch else \
            vtc_ref[:, (c - n_lat_ch) * LANES:(c - n_lat_ch + 1) * LANES]
        vt3[c, 0:2 * hd, :] = src
        vt3[c, 2 * hd:2 * hd + ONES_ROWS, :] = ones
    low = lax.broadcasted_iota(jnp.int32, (TQ, LANES), 1) < hd
    col = lax.broadcasted_iota(jnp.int32, (1, nh * TQ), 1)
    sink = jnp.full((1, nh * TQ), sink_ref[0, SWA_HEAD_ORDER[-1]] * LOG2E, F32)
    for c in range(nh - 2, -1, -1):
        sink = jnp.where(col < (c + 1) * TQ, sink_ref[0, SWA_HEAD_ORDER[c]] * LOG2E, sink)

    def band_start(t):
        return jnp.clip(2 * t - 1, 0, n_lat_ch - band_ch)

    def scores(q, s_dst, t=None):
        zero = jnp.zeros((TQ, LANES), BF16)
        rows = []
        for grp in range(2):
            qg = q[:, grp * LANES:(grp + 1) * LANES]
            rows += [jnp.where(low, qg, zero), jnp.where(low, zero, qg)]
        q4 = jnp.concatenate(rows, axis=0)
        s_c = _dot_nt(kc_ref[...], q4)
        s_dst[KB:NKB, :] = s_c
        m = jnp.max(s_c, axis=0, keepdims=True)
        if t is not None:
            st = band_start(t) * LANES
            kb = k_ref[pl.ds(pl.multiple_of(st, LANES), KB), :]
            k_abs = st + lax.broadcasted_iota(jnp.int32, (KB, TQ), 0)
            q_abs = t * TQ + lax.broadcasted_iota(jnp.int32, (KB, TQ), 1)
            valid = jnp.abs(q_abs - k_abs) <= WINDOW
            s_l = _dot_nt(kb, q4)
            s_l = jnp.concatenate(
                [jnp.where(valid, s_l[:, c * TQ:(c + 1) * TQ], NEG_INF) for c in range(nh)], axis=1)
            s_dst[0:KB, :] = s_l
            m = jnp.maximum(m, jnp.max(s_l, axis=0, keepdims=True))
        return jnp.maximum(m, sink)

    def probs(s_src, m, p_dst, lo=0):
        p_dst[lo:NKB, :] = jnp.exp2(s_src[lo:NKB, :] - m).astype(BF16)

    def attend(p_src, m, t=None):
        ctx_v = [vt3[n_lat_ch + c] for c in range(n_ctx_ch)]
        if t is None:
            lo, vt = KB, jnp.concatenate(ctx_v, axis=1)
        else:
            st = band_start(t)
            lo, vt = 0, jnp.concatenate([vt3[st + d] for d in range(band_ch)] + ctx_v, axis=1)
        acc = _dot(vt, p_src[lo:NKB, :])
        den = acc[2 * hd:2 * hd + 1] + jnp.exp2(sink - m)
        o_n = acc[0:2 * hd] * (1.0 / den)
        pieces = []
        for h in range(nh):
            c = SWA_HEAD_ORDER.index(h)
            pieces.append(o_n[(c % 2) * hd:(c % 2 + 1) * hd, c * TQ:(c + 1) * TQ])
        return jnp.concatenate(pieces, axis=0).T.astype(BF16)

    def q_block(i):
        return q_ref[pl.ds(pl.multiple_of(i * TQ, TQ), TQ), :]

    def emit(i, y):
        o_ref[pl.ds(pl.multiple_of(i * TQ, TQ), TQ), :] = y

    nb = S // TQ
    m_a = scores(q_block(0), s_a, 0)
    m_b = scores(q_block(1), s_b, 1)
    probs(s_a, m_a, p_a)

    def pair(j, carry):
        m_b, m_pa = carry
        b = 2 * j
        m_a = scores(q_block(b + 2), s_a, b + 2)
        probs(s_b, m_b, p_b)
        emit(b, attend(p_a, m_pa, b))
        m_b2 = scores(q_block(b + 3), s_b, b + 3)
        probs(s_a, m_a, p_a)
        emit(b + 1, attend(p_b, m_b, b + 1))
        return m_b2, m_a

    m_b, m_pa = lax.fori_loop(0, nb // 2 - 1, pair, (m_b, m_a))
    probs(s_b, m_b, p_b)
    emit(nb - 2, attend(p_a, m_pa, nb - 2))
    emit(nb - 1, attend(p_b, m_b, nb - 1))

    m_c = scores(qc_ref[...], s_a)
    probs(s_a, m_c, p_a, KB)
    oc_ref[...] = attend(p_a, m_c)


def _swa(qk, vt, sink, *, B, S, Lc):
    qw = SWA_HEADS * HEAD_DIM
    ctx_blk0 = B * S // Lc
    sv_blk = ROW_SV // LANES
    NKB = ATT_TQ + 2 * WINDOW + Lc
    return pl.pallas_call(
        functools.partial(_swa_kernel, S=S, Lc=Lc),
        grid=(B,),
        in_specs=[
            pl.BlockSpec(memory_space=pltpu.SMEM),
            pl.BlockSpec((S, qw), lambda b: (b, COL_SQ // qw)),
            pl.BlockSpec((S, LANES), lambda b: (b, COL_SK // LANES)),
            pl.BlockSpec((LANES, S), lambda b: (sv_blk, b)),
            pl.BlockSpec((Lc, qw), lambda b: (ctx_blk0 + b, COL_SQ // qw)),
            pl.BlockSpec((Lc, LANES), lambda b: (ctx_blk0 + b, COL_SK // LANES)),
            pl.BlockSpec((LANES, Lc), lambda b: (sv_blk, ctx_blk0 + b)),
        ],
        out_specs=[
            pl.BlockSpec((S, qw), lambda b: (b, 0)),
            pl.BlockSpec((Lc, qw), lambda b: (b, 0)),
        ],
        out_shape=[
            jax.ShapeDtypeStruct((B * S, qw), BF16),
            jax.ShapeDtypeStruct((B * Lc, qw), BF16),
        ],
        scratch_shapes=[
            pltpu.VMEM(((S + Lc) // LANES, 2 * HEAD_DIM + ONES_ROWS, LANES), BF16),
            pltpu.VMEM((NKB, SWA_HEADS * ATT_TQ), F32),
            pltpu.VMEM((NKB, SWA_HEADS * ATT_TQ), F32),
            pltpu.VMEM((NKB, SWA_HEADS * ATT_TQ), BF16),
            pltpu.VMEM((NKB, SWA_HEADS * ATT_TQ), BF16),
        ],
        compiler_params=_cparams(("arbitrary",)),
        name="swa",
    )(sink, qk, qk, vt, qk, qk, vt)


def _diff_kernel(dl_ref, g_ref, q_ref, k_ref, vt_ref, qc_ref, kc_ref, vtc_ref, o_ref, oc_ref,
                 vt_scr, s_a, s_b, p_a, p_b, *, S, Lc, lambda_init):
    TQ = ATT_TQ
    NK = S + Lc
    dv = 2 * HEAD_DIM
    nhs = DIFF_HEADS_STEP
    for hi in range(nhs):
        vt_scr[hi, 0:dv, 0:S] = vt_ref[hi * dv:(hi + 1) * dv, :]
        vt_scr[hi, 0:dv, S:NK] = vtc_ref[hi * dv:(hi + 1) * dv, :]
        vt_scr[hi, dv:dv + ONES_ROWS, :] = jnp.ones((ONES_ROWS, NK), BF16)
    dl = dl_ref[...]
    lam = (jnp.exp(jnp.sum(dl[0:1] * dl[1:2], axis=-1, keepdims=True))
           - jnp.exp(jnp.sum(dl[2:3] * dl[3:4], axis=-1, keepdims=True)) + lambda_init)
    low = lax.broadcasted_iota(jnp.int32, (TQ, LANES), 1) < HEAD_DIM
    lat_pieces = [(r, r + DIFF_CHUNK, k_ref, r) for r in range(0, S, DIFF_CHUNK)]
    ctx_piece = (S, NK, kc_ref, 0)

    def head_lanes(hi):
        return slice(hi * LANES, (hi + 1) * LANES)

    def scores(q, hi, s_dst, lo=0):
        zero = jnp.zeros_like(q)
        q2 = jnp.concatenate([jnp.where(low, q, zero), jnp.where(low, zero, q)], axis=0)
        m_acc = None
        for r0, r1, kref, off in ([] if lo else lat_pieces) + [ctx_piece]:
            blk = _dot_nt(kref[off:off + (r1 - r0), head_lanes(hi)], q2)
            s_dst[r0:r1, :] = blk
            for r in range(0, r1 - r0, DIFF_ROWS):
                part = blk[r:r + DIFF_ROWS]
                m_acc = part if m_acc is None else jnp.maximum(m_acc, part)
        return jnp.max(m_acc, axis=0, keepdims=True)

    def probs(s_src, m, p_dst, lo=0):
        p_dst[lo:NK, :] = jnp.exp2(s_src[lo:NK, :] - m).astype(BF16)

    def attend(p_src, hi, lo=0):
        acc = _dot(vt_scr[hi, :, lo:NK], p_src[lo:NK, :])
        o_n = acc[0:dv] * (1.0 / acc[dv:dv + 1])
        o_t = o_n[:, 0:TQ] - lam * o_n[:, TQ:2 * TQ]
        ms = jnp.mean(o_t * o_t, axis=0, keepdims=True)
        y_t = o_t * (lax.rsqrt(ms + EPS) * (1.0 - lambda_init))
        return (y_t.T * g_ref[...]).astype(BF16)

    def block_rows(i):
        return pl.ds(i * TQ, TQ) if isinstance(i, int) else pl.ds(pl.multiple_of(i * TQ, TQ), TQ)

    def q_block(hi, i):
        return q_ref[block_rows(i), head_lanes(hi)]

    def emit(hi, i, y):
        o_ref[block_rows(i), head_lanes(hi)] = y

    def pair(hi, b, m_b):
        m_a = scores(q_block(hi, b + 2), hi, s_a)
        probs(s_b, m_b, p_b)
        emit(hi, b, attend(p_a, hi))
        m_b = scores(q_block(hi, b + 3), hi, s_b)
        probs(s_a, m_a, p_a)
        emit(hi, b + 1, attend(p_b, hi))
        return m_b

    nb = S // TQ
    n_pairs = nb // 2 - 1
    n_trips = n_pairs // DIFF_PAIRS_PER_TRIP
    m_a = scores(q_block(0, 0), 0, s_a)
    m_b = scores(q_block(0, 1), 0, s_b)
    probs(s_a, m_a, p_a)
    for hi in range(nhs):
        def trip(j, m_b, hi=hi):
            for r in range(DIFF_PAIRS_PER_TRIP):
                m_b = pair(hi, 2 * (DIFF_PAIRS_PER_TRIP * j + r), m_b)
            return m_b

        m_b = lax.fori_loop(0, n_trips, trip, m_b)
        for r in range(n_trips * DIFF_PAIRS_PER_TRIP, n_pairs):
            m_b = pair(hi, 2 * r, m_b)
        if hi + 1 < nhs:
            m_a = scores(q_block(hi + 1, 0), hi + 1, s_a)
            probs(s_b, m_b, p_b)
            emit(hi, nb - 2, attend(p_a, hi))
            m_b = scores(q_block(hi + 1, 1), hi + 1, s_b)
            probs(s_a, m_a, p_a)
            emit(hi, nb - 1, attend(p_b, hi))
        else:
            probs(s_b, m_b, p_b)
            emit(hi, nb - 2, attend(p_a, hi))
            emit(hi, nb - 1, attend(p_b, hi))

    for hi in range(nhs):
        m_c = scores(qc_ref[:, head_lanes(hi)], hi, s_a, S)
        probs(s_a, m_c, p_a, S)
        oc_ref[:, head_lanes(hi)] = attend(p_a, hi, S)


def _diff(qk, vt, diff_lambda, subln_g, *, B, S, Lc, lambda_init):
    H = DIFF_HEADS
    dv = 2 * HEAD_DIM
    nhs = DIFF_HEADS_STEP
    NK = S + Lc
    ctx_blk0 = B * S // Lc
    w = nhs * LANES
    return pl.pallas_call(
        functools.partial(_diff_kernel, S=S, Lc=Lc, lambda_init=lambda_init),
        grid=(B, H // nhs),
        in_specs=[
            _resident(diff_lambda.shape),
            _resident(subln_g.shape),
            pl.BlockSpec((S, w), lambda b, h: (b, COL_DQ // w + h)),
            pl.BlockSpec((S, w), lambda b, h: (b, COL_DK // w + h)),
            pl.BlockSpec((w, S), lambda b, h: (ROW_DV // w + h, b)),
            pl.BlockSpec((Lc, w), lambda b, h: (ctx_blk0 + b, COL_DQ // w + h)),
            pl.BlockSpec((Lc, w), lambda b, h: (ctx_blk0 + b, COL_DK // w + h)),
            pl.BlockSpec((w, Lc), lambda b, h: (ROW_DV // w + h, ctx_blk0 + b)),
        ],
        out_specs=[
            pl.BlockSpec((S, w), lambda b, h: (b, h)),
            pl.BlockSpec((Lc, w), lambda b, h: (b, h)),
        ],
        out_shape=[
            jax.ShapeDtypeStruct((B * S, H * dv), BF16),
            jax.ShapeDtypeStruct((B * Lc, H * dv), BF16),
        ],
        scratch_shapes=[
            pltpu.VMEM((nhs, dv + ONES_ROWS, NK), BF16),
            pltpu.VMEM((NK, 2 * ATT_TQ), F32),
            pltpu.VMEM((NK, 2 * ATT_TQ), F32),
            pltpu.VMEM((NK, 2 * ATT_TQ), BF16),
            pltpu.VMEM((NK, 2 * ATT_TQ), BF16),
        ],
        compiler_params=_cparams(("arbitrary", "arbitrary")),
        name="diff",
    )(diff_lambda, subln_g, qk, qk, vt, qk, qk, vt)


def _rope_tables(S, tm):
    rows = S // GRID_W
    row = jnp.repeat(jnp.arange(rows, dtype=F32), GRID_W)
    col = jnp.tile(jnp.arange(GRID_W, dtype=F32), rows)
    n_freq = HEAD_DIM // 4
    inv_freq = ROPE_BASE ** (-jnp.arange(n_freq, dtype=F32) / n_freq)
    ang = jnp.concatenate([row[:, None] * inv_freq, col[:, None] * inv_freq], axis=-1)
    cos, sin = jnp.cos(ang), jnp.sin(ang)
    reps = LANES // (HEAD_DIM // 2)
    sign = np.tile(np.concatenate([-np.ones(HEAD_DIM // 2), np.ones(HEAD_DIM // 2)]), LANES // HEAD_DIM)
    cos_t = jnp.concatenate([jnp.tile(cos, (1, reps)), jnp.ones((tm, LANES), F32)], axis=0)
    sin_t = jnp.concatenate([jnp.tile(sin, (1, reps)) * sign.astype(np.float32), jnp.zeros((tm, LANES), F32)], axis=0)
    return cos_t, sin_t


def _block_diag(w):
    nd, K, c, _ = w.shape
    eye = jnp.eye(K, dtype=w.dtype)
    return jnp.einsum('dkij,kl->dkilj', w, eye).reshape(nd, K * c, K * c)


def kernel(x, c, ctx, c_ctx, w_ada, b_ada, norm_g, ffn1_w_gu, ffn1_w_down, ffn2_w_gu, ffn2_w_down,
           w_in, w_out, conv_w, conv_b, lru_w_r, lru_b_r, lru_w_i, lru_b_i, lru_lambda,
           swa_sink, diff_lambda, diff_subln_g, final_g):
    B, S, D = x.shape
    Lc = ctx.shape[1]
    depth = w_ada.shape[0]
    tm = TOKEN_TILE
    assert S % tm == 0 and (B * Lc) % tm == 0 and Lc == ATT_TQ and S % GRID_W == 0 and S % LRU_CHUNK == 0
    assert Lc == LRU_CHUNK and S >= ATT_TQ + 2 * WINDOW
    assert S % DIFF_CHUNK == 0 and (S // ATT_TQ) % 2 == 0
    n_lat_tiles = B * S // tm
    n_tiles = n_lat_tiles + B * Lc // tm
    geom = (n_lat_tiles, S // tm, B)

    h = (x.reshape(B * S, D), ctx.reshape(B * Lc, D))

    n_cond = B + 1
    pad = (-n_cond) % SUBLANES
    cond = jnp.concatenate([c, c_ctx[None], jnp.zeros((pad, D), F32)], axis=0)
    mod_all = _ada(cond, w_ada, b_ada).reshape(depth, n_cond + pad, N_MOD, D)

    cos_t, sin_t = _rope_tables(S, _inproj_tile(S, B * Lc))
    w1_gu, w1_down = ffn1_w_gu.astype(BF16), ffn1_w_down.astype(BF16)
    w2_gu, w2_down = ffn2_w_gu.astype(BF16), ffn2_w_down.astype(BF16)

    for l in range(depth):
        last = l == depth - 1
        mod = mod_all[l]
        lambda_init = 0.8 - 0.6 * math.exp(-0.3 * l)
        g = norm_g[l]
        h = _ffn(h, mod, g[0:1], w1_gu, w1_down, final_g[None], layer=l, k=0, n_tiles=n_tiles, geom=geom)
        lru_in, qk, vt = _inproj(h, mod, g[1:2], w_in, cos_t, sin_t, layer=l, B=B, S=S)
        y_lru = _lru(lru_in, conv_w[l], conv_b[l][None], _block_diag(lru_w_r[l]), lru_b_r[l][:, None],
                     _block_diag(lru_w_i[l]), lru_b_i[l][:, None], lru_lambda[l][:, None], B=B, S=S, Lc=Lc)
        y_swa = _swa(qk, vt, swa_sink[l][None], B=B, S=S, Lc=Lc)
        y_diff = _diff(qk, vt, diff_lambda[l], diff_subln_g[l][None], B=B, S=S, Lc=Lc, lambda_init=lambda_init)
        n_out = n_lat_tiles if last else n_tiles
        h = _ffn(h, mod, g[2:3], w2_gu, w2_down, final_g[None], layer=l, k=2, n_tiles=n_out, geom=geom,
                 final_norm=last, mix=(y_lru, y_swa, y_diff), w_out=w_out)
    return h[:B * S].reshape(B, S, D)
```

```python
import functools
import math

import numpy as np
import jax
import jax.numpy as jnp
from jax import lax
from jax.experimental import pallas as pl
from jax.experimental.pallas import tpu as pltpu

F32 = jnp.float32
BF16 = jnp.bfloat16

GRID_W = 64
N_MOD = 9
EPS = 1e-6
FFN_RES = 0.5
HEAD_DIM = 64
ROPE_BASE = 10000.0
LRU_WIDTH = 256
LRU_BLOCKS = 4
CONV_WIDTH = 4
CONV_LEFT = 2
LRU_C = 8.0
SWA_HEADS = 4
SWA_KV_HEADS = 2
WINDOW = 128
DIFF_HEADS = 4
NEG_INF = -1e30

LANES = 128
SUBLANES = 8
VMEM_LIMIT_BYTES = 56 * 1024 * 1024

TOKEN_TILE = 512
FFN_CHUNKS = 1
ADA_COLS = 2304
ATT_TQ = 256
LRU_CHUNK = 256

QK_WIDTH = 1408
COL_SQ, COL_DQ, COL_DK, COL_SK = 0, 256, 768, 1280
VT_ROWS = 640
ROW_DV, ROW_SV = 0, 512
IN_LRU, IN_SQ, IN_SK, IN_SV, IN_DQ, IN_DK, IN_DV = 0, 512, 768, 896, 1024, 1536, 2048
LOG2E = math.log2(math.e)
DIFF_ROWS = 16
DIFF_CHUNK = 1024
DIFF_HEADS_STEP = 2
ONES_ROWS = 16
SWA_HEAD_ORDER = (0, 2, 1, 3)


def _cparams(semantics):
    return pltpu.CompilerParams(dimension_semantics=semantics, vmem_limit_bytes=VMEM_LIMIT_BYTES)


def _resident(shape):
    nd = len(shape)
    return pl.BlockSpec(shape, lambda *_: (0,) * nd, pipeline_mode=pl.Buffered(1))


def _layer_resident(stacked, layer):
    nd = stacked.ndim - 1
    return pl.BlockSpec((None,) + stacked.shape[1:], lambda *_: (layer,) + (0,) * nd, pipeline_mode=pl.Buffered(1))


def _dot(a, b):
    return jnp.dot(a, b, preferred_element_type=F32)


def _dot_nt(a, b):
    return lax.dot_general(a, b, (((1,), (1,)), ((), ())), preferred_element_type=F32)


def _rms(x, g):
    return x * lax.rsqrt(jnp.mean(x * x, axis=-1, keepdims=True) + EPS) * g


def _modnorm(h, g, mod_ref, k):
    shift = mod_ref[0, 3 * k:3 * k + 1, :]
    scale = mod_ref[0, 3 * k + 1:3 * k + 2, :]
    return _rms(h, g) * (1.0 + scale) + shift


def _neg_expm1_2x(a, x):
    return (1.0 + a * a) * jnp.tanh(-x)


def _ada_kernel(c_ref, w_ref, b_ref, o_ref):
    c = c_ref[...]
    s = (c * jax.nn.sigmoid(c)).astype(BF16)
    o_ref[0] = _dot(s, w_ref[0].astype(BF16)) + b_ref[0]


def _ada(cond, w_ada, b_ada):
    L, D, N = w_ada.shape
    R = cond.shape[0]
    return pl.pallas_call(
        _ada_kernel,
        grid=(L, N // ADA_COLS),
        in_specs=[
            pl.BlockSpec((R, D), lambda l, n: (0, 0)),
            pl.BlockSpec((1, D, ADA_COLS), lambda l, n: (l, 0, n)),
            pl.BlockSpec((1, 1, ADA_COLS), lambda l, n: (l, 0, n)),
        ],
        out_specs=pl.BlockSpec((1, R, ADA_COLS), lambda l, n: (l, 0, n)),
        out_shape=jax.ShapeDtypeStruct((L, R, N), F32),
        compiler_params=_cparams(("arbitrary", "arbitrary")),
        name="ada",
    )(cond, w_ada, b_ada.reshape(L, 1, N))


def _mod_index(n_lat_tiles, tiles_per_batch, n_batch):
    def idx(i):
        return (jnp.where(i < n_lat_tiles, i // tiles_per_batch, n_batch), 0, 0)
    return idx


def _ffn_kernel(*refs, k, d_ff, final_norm, n_lat_tiles, split_input, n_mix, mix_has_ctx):
    if n_mix:
        wo_bf = refs[-1]
        refs = refs[:-1]

        @pl.when(pl.program_id(0) == 0)
        def _():
            wo_bf[...] = refs[-7][...].astype(BF16)

    mod_ref, g_ref, wgu_ref, wd_ref, fg_ref, o_ref = refs[-6:]
    is_lat = pl.program_id(0) < n_lat_tiles
    if split_input:
        h = jnp.where(is_lat, refs[0][...], refs[1][...])
    else:
        h = refs[0][...]
    if n_mix:
        mix_refs = refs[-7 - 2 * n_mix:-7]
        parts = []
        for lat_ref, ctx_ref in zip(mix_refs[0::2], mix_refs[1::2]):
            parts.append(jnp.where(is_lat, lat_ref[...], ctx_ref[...]) if mix_has_ctx else lat_ref[...])
        h = h + mod_ref[0, 5:6, :] * _dot(jnp.concatenate(parts, axis=1), wo_bf[...])
    xn = _modnorm(h, g_ref[...], mod_ref, k).astype(BF16)
    tf = d_ff // FFN_CHUNKS
    acc = None
    for c in range(FFN_CHUNKS):
        lo = c * tf
        g = _dot(xn, wgu_ref[:, lo:lo + tf])
        u = _dot(xn, wgu_ref[:, d_ff + lo:d_ff + lo + tf])
        a = (g * jax.nn.sigmoid(g) * u).astype(BF16)
        part = _dot(a, wd_ref[lo:lo + tf, :])
        acc = part if acc is None else acc + part
    gate = mod_ref[0, 3 * k + 2:3 * k + 3, :]
    out = h + (FFN_RES * gate) * acc
    if final_norm:
        out = _rms(out, fg_ref[...])
    o_ref[...] = out


def _ffn(h, mod, g, w_gu, w_down, final_g, *, layer, k, n_tiles, geom, final_norm=False, mix=(), w_out=None):
    split = isinstance(h, tuple)
    hs = h if split else (h,)
    D = hs[0].shape[1]
    d_ff = w_down.shape[1]
    tm = TOKEN_TILE
    n_lat_tiles = geom[0]

    def lat_idx(i):
        return (jnp.minimum(i, n_lat_tiles - 1), 0)

    def ctx_idx(i):
        return (jnp.maximum(i - n_lat_tiles, 0), 0)

    if split:
        h_specs = [pl.BlockSpec((tm, D), lat_idx), pl.BlockSpec((tm, D), ctx_idx)]
    else:
        h_specs = [pl.BlockSpec((tm, D), lambda i: (i, 0))]
    mix_args, mix_specs = [], []
    for lat, ctx in mix:
        mix_args += [lat, ctx]
        mix_specs += [pl.BlockSpec((tm, lat.shape[1]), lat_idx), pl.BlockSpec((tm, lat.shape[1]), ctx_idx)]
    if mix:
        mix_args.append(w_out)
        mix_specs.append(_layer_resident(w_out, layer))
    return pl.pallas_call(
        functools.partial(_ffn_kernel, k=k, d_ff=d_ff, final_norm=final_norm, n_lat_tiles=n_lat_tiles,
                          split_input=split, n_mix=len(mix), mix_has_ctx=n_tiles > n_lat_tiles),
        grid=(n_tiles,),
        in_specs=h_specs + mix_specs + [
            pl.BlockSpec((1, N_MOD, D), _mod_index(*geom)),
            _resident((1, D)),
            _layer_resident(w_gu, layer),
            _layer_resident(w_down, layer),
            _resident((1, D)),
        ],
        out_specs=pl.BlockSpec((tm, D), lambda i: (i, 0)),
        out_shape=jax.ShapeDtypeStruct((n_tiles * tm, D), F32),
        scratch_shapes=[pltpu.VMEM(w_out.shape[1:], BF16)] if mix else [],
        compiler_params=_cparams(("arbitrary",)),
        name="ffn",
    )(*hs, *mix_args, mod, g, w_gu, w_down, final_g)


def _rope(x, cos, sin):
    lane = lax.broadcasted_iota(jnp.int32, x.shape, 1)
    first = (lane & (HEAD_DIM - 1)) < (HEAD_DIM // 2)
    partner = jnp.where(first, pltpu.roll(x, LANES - HEAD_DIM // 2, 1), pltpu.roll(x, HEAD_DIM // 2, 1))
    return x * cos + partner * sin


def _inproj_kernel(h_ref, mod_ref, g_ref, w_ref, cos_ref, sin_ref, lru_ref, qk_ref, vt_ref, w_bf):
    @pl.when(pl.program_id(0) == 0)
    def _():
        w_bf[...] = w_ref[...].astype(BF16)

    xn = _modnorm(h_ref[...], g_ref[...], mod_ref, 1).astype(BF16)
    y = _dot(xn, w_bf[...])
    lru_ref[...] = y[:, IN_LRU:IN_SQ]
    cos = cos_ref[...]
    sin = sin_ref[...]
    q_scale = HEAD_DIM ** -0.5 * LOG2E

    def roped(c0, scale=None):
        blk = _rope(y[:, c0:c0 + LANES], cos, sin)
        return blk if scale is None else blk * scale

    def put(c0, blk):
        qk_ref[:, c0:c0 + LANES] = blk.astype(BF16)

    qa, qb = roped(IN_SQ, q_scale), roped(IN_SQ + LANES, q_scale)
    low = lax.broadcasted_iota(jnp.int32, qa.shape, 1) < HEAD_DIM
    put(COL_SQ, jnp.where(low, qa, pltpu.roll(qb, HEAD_DIM, 1)))
    put(COL_SQ + LANES, jnp.where(low, pltpu.roll(qa, HEAD_DIM, 1), qb))
    put(COL_SK, roped(IN_SK))
    for c in range(0, DIFF_HEADS * 2 * HEAD_DIM, LANES):
        put(COL_DQ + c, roped(IN_DQ + c, q_scale))
        put(COL_DK + c, roped(IN_DK + c))
    sv_w = SWA_KV_HEADS * HEAD_DIM
    vt_ref[ROW_SV:ROW_SV + sv_w, :] = y[:, IN_SV:IN_SV + sv_w].T.astype(BF16)
    dv_w = DIFF_HEADS * 2 * HEAD_DIM
    vt_ref[ROW_DV:ROW_DV + dv_w, :] = y[:, IN_DV:IN_DV + dv_w].T.astype(BF16)


def _inproj_tile(S, n_ctx_rows):
    big = 2 * TOKEN_TILE
    return big if S % big == 0 and n_ctx_rows % big == 0 else TOKEN_TILE


def _inproj(h, mod, g, w_in, cos_t, sin_t, *, layer, B, S):
    T, D = h.shape
    tm = _inproj_tile(S, T - B * S)
    n_tiles = T // tm
    geom = (B * S // tm, S // tm, B)
    n_lat_tiles, tiles_per_batch, _ = geom

    def tab_idx(i):
        return (jnp.where(i < n_lat_tiles, i % tiles_per_batch, tiles_per_batch), 0)

    return pl.pallas_call(
        _inproj_kernel,
        grid=(n_tiles,),
        in_specs=[
            pl.BlockSpec((tm, D), lambda i: (i, 0)),
            pl.BlockSpec((1, N_MOD, D), _mod_index(*geom)),
            _resident((1, D)),
            _layer_resident(w_in, layer),
            pl.BlockSpec((tm, LANES), tab_idx),
            pl.BlockSpec((tm, LANES), tab_idx),
        ],
        out_specs=[
            pl.BlockSpec((tm, 2 * LRU_WIDTH), lambda i: (i, 0)),
            pl.BlockSpec((tm, QK_WIDTH), lambda i: (i, 0)),
            pl.BlockSpec((VT_ROWS, tm), lambda i: (0, i)),
        ],
        out_shape=[
            jax.ShapeDtypeStruct((T, 2 * LRU_WIDTH), F32),
            jax.ShapeDtypeStruct((T, QK_WIDTH), BF16),
            jax.ShapeDtypeStruct((VT_ROWS, T), BF16),
        ],
        scratch_shapes=[pltpu.VMEM(w_in.shape[1:], BF16)],
        compiler_params=_cparams(("arbitrary",)),
        name="inproj",
    )(h, mod, g, w_in, cos_t, sin_t)


def _lru_kernel(lat_ref, ctx_ref, cw_ref, cb_ref, wr_ref, br_ref, wi_ref, bi_ref, lam_ref,
                ylat_ref, yctx_ref, xpad, u_scr, hf_scr, ge_scr, *g_scr, S, Lc):
    W = LRU_WIDTH
    TC = LRU_CHUNK
    PAD = SUBLANES
    row = lax.broadcasted_iota(jnp.int32, (TC, W), 0)
    group = lax.broadcasted_iota(jnp.int32, (TC // SUBLANES, W), 0)

    def conv_chunk(start):
        a = xpad[pl.ds(start, TC + 2 * PAD), :]
        u = cb_ref[...] + jnp.zeros((TC, W), F32)
        for k in range(CONV_WIDTH):
            sh = (CONV_LEFT - k) % (TC + 2 * PAD)
            r = a if sh == 0 else pltpu.roll(a, sh, 0)
            u = u + r[PAD:PAD + TC] * cw_ref[k:k + 1, :]
        return u

    def scan_chunk(u, d, carry, reverse):
        r = jax.nn.sigmoid(_dot(u, wr_ref[d]) + br_ref[d])
        i = jax.nn.sigmoid(_dot(u, wi_ref[d]) + bi_ref[d])
        log_a = (-LRU_C) * r * jax.nn.softplus(-lam_ref[d])
        A = jnp.exp(log_a)
        Bv = jnp.sqrt(_neg_expm1_2x(A, log_a)) * i * u

        def combine(A, Bv, pos, axis):
            n = A.shape[axis]
            s = 1
            while s < n:
                if reverse:
                    keep = pos < n - s
                    sh = n - s
                else:
                    keep = pos >= s
                    sh = s
                a_sh = jnp.where(keep, pltpu.roll(A, sh, axis), 1.0)
                b_sh = jnp.where(keep, pltpu.roll(Bv, sh, axis), 0.0)
                Bv = A * b_sh + Bv
                A = A * a_sh
                s *= 2
            return A, Bv

        G = SUBLANES
        ng = TC // G
        A, Bv = combine(A.reshape(ng, G, W), Bv.reshape(ng, G, W), lax.broadcasted_iota(jnp.int32, (ng, G, W), 1), 1)
        A = A.reshape(TC, W)
        Bv = Bv.reshape(TC, W)
        edge = 0 if reverse else G - 1

        def boundary_rows(planes, x):
            out = []
            for c, scr in enumerate(planes):
                scr[...] = x[:, c * LANES:(c + 1) * LANES]
                out.append(scr[pl.ds(edge, ng, stride=G), :])
            return jnp.concatenate(out, axis=1)

        n_pl = W // LANES
        Ag, Bg = combine(boundary_rows(g_scr[:n_pl], A), boundary_rows(g_scr[n_pl:], Bv), group, 0)
        ge_scr[...] = Ag * carry + Bg
        parts = []
        for g in range(ng):
            nb_g = g + 1 if reverse else g - 1
            h_in = carry if (nb_g < 0 or nb_g >= ng) else ge_scr[nb_g:nb_g + 1, :]
            parts.append(A[g * G:(g + 1) * G] * h_in + Bv[g * G:(g + 1) * G])
        h = jnp.concatenate(parts, axis=0)
        return h, (h[0:1] if reverse else h[TC - 1:TC])

    def gelu(x):
        return jax.nn.gelu(x)

    zero_pad = jnp.zeros((PAD, W), F32)
    zero_state = jnp.zeros((1, W), F32)

    xpad[0:PAD, :] = zero_pad
    xpad[PAD:PAD + Lc, :] = ctx_ref[:, 0:W]
    xpad[PAD + Lc:2 * PAD + Lc, :] = zero_pad
    uc = conv_chunk(0)
    hcf, carry_f = scan_chunk(uc, 0, zero_state, False)
    hcb, carry_b = scan_chunk(uc, 1, zero_state, True)
    yctx_ref[...] = ((hcf + hcb) * gelu(ctx_ref[:, W:2 * W])).astype(BF16)

    xpad[PAD:PAD + S, :] = lat_ref[:, 0:W]
    xpad[PAD + S:2 * PAD + S, :] = zero_pad
    nc = S // TC

    def conv_body(c, _):
        st = pl.multiple_of(c * TC, TC)
        u_scr[pl.ds(st, TC), :] = conv_chunk(st)
        return 0

    lax.fori_loop(0, nc, conv_body, 0)

    def fwd_body(c, carry):
        st = pl.multiple_of(c * TC, TC)
        h, carry = scan_chunk(u_scr[pl.ds(st, TC), :], 0, carry, False)
        hf_scr[pl.ds(st, TC), :] = h
        return carry

    lax.fori_loop(0, nc, fwd_body, carry_f)

    def bwd_body(c, carry):
        st = pl.multiple_of((nc - 1 - c) * TC, TC)
        h, carry = scan_chunk(u_scr[pl.ds(st, TC), :], 1, carry, True)
        y = (hf_scr[pl.ds(st, TC), :] + h) * gelu(lat_ref[pl.ds(st, TC), W:2 * W])
        ylat_ref[pl.ds(st, TC), :] = y.astype(BF16)
        return carry

    lax.fori_loop(0, nc, bwd_body, carry_b)


def _lru(lru_in, conv_w, conv_b, w_r, b_r, w_i, b_i, lam, *, B, S, Lc):
    W = LRU_WIDTH
    t_lat = B * S
    ctx_blk0 = t_lat // Lc
    return pl.pallas_call(
        functools.partial(_lru_kernel, S=S, Lc=Lc),
        grid=(B,),
        in_specs=[
            pl.BlockSpec((S, 2 * W), lambda b: (b, 0)),
            pl.BlockSpec((Lc, 2 * W), lambda b: (ctx_blk0 + b, 0)),
            _resident(conv_w.shape),
            _resident(conv_b.shape),
            _resident(w_r.shape),
            _resident(b_r.shape),
            _resident(w_i.shape),
            _resident(b_i.shape),
            _resident(lam.shape),
        ],
        out_specs=[
            pl.BlockSpec((S, W), lambda b: (b, 0)),
            pl.BlockSpec((Lc, W), lambda b: (b, 0)),
        ],
        out_shape=[
            jax.ShapeDtypeStruct((t_lat, W), BF16),
            jax.ShapeDtypeStruct((B * Lc, W), BF16),
        ],
        scratch_shapes=[
            pltpu.VMEM((S + 2 * SUBLANES, W), F32),
            pltpu.VMEM((S, W), F32),
            pltpu.VMEM((S, W), F32),
            pltpu.VMEM((LRU_CHUNK // SUBLANES, W), F32),
        ] + [pltpu.VMEM((LRU_CHUNK, LANES), F32)] * (2 * (W // LANES)) + [
        ],
        compiler_params=_cparams(("arbitrary",)),
        name="lru",
    )(lru_in, lru_in, conv_w, conv_b, w_r, b_r, w_i, b_i, lam)


def _swa_kernel(sink_ref, q_ref, k_ref, vt_ref, qc_ref, kc_ref, vtc_ref, o_ref, oc_ref,
                vt3, s_a, s_b, p_a, p_b, *, S, Lc):
    TQ = ATT_TQ
    KB = TQ + 2 * WINDOW
    NKB = KB + Lc
    hd = HEAD_DIM
    nh = SWA_HEADS
    n_lat_ch = S // LANES
    n_ctx_ch = Lc // LANES
    band_ch = KB // LANES
    ones = jnp.ones((ONES_ROWS, LANES), BF16)
    for c in range(n_lat_ch + n_ctx_ch):
        src = vt_ref[:, c * LANES:(c + 1) * LANES] if c < n_lat_ch else \
            vtc_ref[:, (c - n_lat_ch) * LANES:(c - n_lat_ch + 1) * LANES]
        vt3[c, 0:2 * hd, :] = src
        vt3[c, 2 * hd:2 * hd + ONES_ROWS, :] = ones
    low = lax.broadcasted_iota(jnp.int32, (TQ, LANES), 1) < hd
    col = lax.broadcasted_iota(jnp.int32, (1, nh * TQ), 1)
    sink = jnp.full((1, nh * TQ), sink_ref[0, SWA_HEAD_ORDER[-1]] * LOG2E, F32)
    for c in range(nh - 2, -1, -1):
        sink = jnp.where(col < (c + 1) * TQ, sink_ref[0, SWA_HEAD_ORDER[c]] * LOG2E, sink)

    def band_start(t):
        return jnp.clip(2 * t - 1, 0, n_lat_ch - band_ch)

    def scores(q, s_dst, t=None):
        zero = jnp.zeros((TQ, LANES), BF16)
        rows = []
        for grp in range(2):
            qg = q[:, grp * LANES:(grp + 1) * LANES]
            rows += [jnp.where(low, qg, zero), jnp.where(low, zero, qg)]
        q4 = jnp.concatenate(rows, axis=0)
        s_c = _dot_nt(kc_ref[...], q4)
        s_dst[KB:NKB, :] = s_c
        m = jnp.max(s_c, axis=0, keepdims=True)
        if t is not None:
            st = band_start(t) * LANES
            kb = k_ref[pl.ds(pl.multiple_of(st, LANES), KB), :]
            k_abs = st + lax.broadcasted_iota(jnp.int32, (KB, TQ), 0)
            q_abs = t * TQ + lax.broadcasted_iota(jnp.int32, (KB, TQ), 1)
            valid = jnp.abs(q_abs - k_abs) <= WINDOW
            s_l = _dot_nt(kb, q4)
            s_l = jnp.concatenate(
                [jnp.where(valid, s_l[:, c * TQ:(c + 1) * TQ], NEG_INF) for c in range(nh)], axis=1)
            s_dst[0:KB, :] = s_l
            m = jnp.maximum(m, jnp.max(s_l, axis=0, keepdims=True))
        return jnp.maximum(m, sink)

    def probs(s_src, m, p_dst, lo=0):
        p_dst[lo:NKB, :] = jnp.exp2(s_src[lo:NKB, :] - m).astype(BF16)

    def attend(p_src, m, t=None):
        ctx_v = [vt3[n_lat_ch + c] for c in range(n_ctx_ch)]
        if t is None:
            lo, vt = KB, jnp.concatenate(ctx_v, axis=1)
        else:
            st = band_start(t)
            lo, vt = 0, jnp.concatenate([vt3[st + d] for d in range(band_ch)] + ctx_v, axis=1)
        acc = _dot(vt, p_src[lo:NKB, :])
        den = acc[2 * hd:2 * hd + 1] + jnp.exp2(sink - m)
        o_n = acc[0:2 * hd] * (1.0 / den)
        pieces = []
        for h in range(nh):
            c = SWA_HEAD_ORDER.index(h)
            pieces.append(o_n[(c % 2) * hd:(c % 2 + 1) * hd, c * TQ:(c + 1) * TQ])
        return jnp.concatenate(pieces, axis=0).T.astype(BF16)

    def q_block(i):
        return q_ref[pl.ds(pl.multiple_of(i * TQ, TQ), TQ), :]

    def emit(i, y):
        o_ref[pl.ds(pl.multiple_of(i * TQ, TQ), TQ), :] = y

    nb = S // TQ
    m_a = scores(q_block(0), s_a, 0)
    m_b = scores(q_block(1), s_b, 1)
    probs(s_a, m_a, p_a)

    def pair(j, carry):
        m_b, m_pa = carry
        b = 2 * j
        m_a = scores(q_block(b + 2), s_a, b + 2)
        probs(s_b, m_b, p_b)
        emit(b, attend(p_a, m_pa, b))
        m_b2 = scores(q_block(b + 3), s_b, b + 3)
        probs(s_a, m_a, p_a)
        emit(b + 1, attend(p_b, m_b, b + 1))
        return m_b2, m_a

    m_b, m_pa = lax.fori_loop(0, nb // 2 - 1, pair, (m_b, m_a))
    probs(s_b, m_b, p_b)
    emit(nb - 2, attend(p_a, m_pa, nb - 2))
    emit(nb - 1, attend(p_b, m_b, nb - 1))

    m_c = scores(qc_ref[...], s_a)
    probs(s_a, m_c, p_a, KB)
    oc_ref[...] = attend(p_a, m_c)


def _swa(qk, vt, sink, *, B, S, Lc):
    qw = SWA_HEADS * HEAD_DIM
    ctx_blk0 = B * S // Lc
    sv_blk = ROW_SV // LANES
    NKB = ATT_TQ + 2 * WINDOW + Lc
    return pl.pallas_call(
        functools.partial(_swa_kernel, S=S, Lc=Lc),
        grid=(B,),
        in_specs=[
            pl.BlockSpec(memory_space=pltpu.SMEM),
            pl.BlockSpec((S, qw), lambda b: (b, COL_SQ // qw)),
            pl.BlockSpec((S, LANES), lambda b: (b, COL_SK // LANES)),
            pl.BlockSpec((LANES, S), lambda b: (sv_blk, b)),
            pl.BlockSpec((Lc, qw), lambda b: (ctx_blk0 + b, COL_SQ // qw)),
            pl.BlockSpec((Lc, LANES), lambda b: (ctx_blk0 + b, COL_SK // LANES)),
            pl.BlockSpec((LANES, Lc), lambda b: (sv_blk, ctx_blk0 + b)),
        ],
        out_specs=[
            pl.BlockSpec((S, qw), lambda b: (b, 0)),
            pl.BlockSpec((Lc, qw), lambda b: (b, 0)),
        ],
        out_shape=[
            jax.ShapeDtypeStruct((B * S, qw), BF16),
            jax.ShapeDtypeStruct((B * Lc, qw), BF16),
        ],
        scratch_shapes=[
            pltpu.VMEM(((S + Lc) // LANES, 2 * HEAD_DIM + ONES_ROWS, LANES), BF16),
            pltpu.VMEM((NKB, SWA_HEADS * ATT_TQ), F32),
            pltpu.VMEM((NKB, SWA_HEADS * ATT_TQ), F32),
            pltpu.VMEM((NKB, SWA_HEADS * ATT_TQ), BF16),
            pltpu.VMEM((NKB, SWA_HEADS * ATT_TQ), BF16),
        ],
        compiler_params=_cparams(("arbitrary",)),
        name="swa",
    )(sink, qk, qk, vt, qk, qk, vt)


def _diff_kernel(dl_ref, g_ref, q_ref, k_ref, vt_ref, qc_ref, kc_ref, vtc_ref, o_ref, oc_ref,
                 vt_scr, s_a, s_b, p_a, p_b, *, S, Lc, lambda_init):
    TQ = ATT_TQ
    NK = S + Lc
    dv = 2 * HEAD_DIM
    nhs = DIFF_HEADS_STEP
    for hi in range(nhs):
        vt_scr[hi, 0:dv, 0:S] = vt_ref[hi * dv:(hi + 1) * dv, :]
        vt_scr[hi, 0:dv, S:NK] = vtc_ref[hi * dv:(hi + 1) * dv, :]
        vt_scr[hi, dv:dv + ONES_ROWS, :] = jnp.ones((ONES_ROWS, NK), BF16)
    dl = dl_ref[...]
    lam = (jnp.exp(jnp.sum(dl[0:1] * dl[1:2], axis=-1, keepdims=True))
           - jnp.exp(jnp.sum(dl[2:3] * dl[3:4], axis=-1, keepdims=True)) + lambda_init)
    low = lax.broadcasted_iota(jnp.int32, (TQ, LANES), 1) < HEAD_DIM
    lat_pieces = [(r, r + DIFF_CHUNK, k_ref, r) for r in range(0, S, DIFF_CHUNK)]
    ctx_piece = (S, NK, kc_ref, 0)

    def head_lanes(hi):
        return slice(hi * LANES, (hi + 1) * LANES)

    def scores(q, hi, s_dst, lo=0):
        zero = jnp.zeros_like(q)
        q2 = jnp.concatenate([jnp.where(low, q, zero), jnp.where(low, zero, q)], axis=0)
        m_acc = None
        for r0, r1, kref, off in ([] if lo else lat_pieces) + [ctx_piece]:
            blk = _dot_nt(kref[off:off + (r1 - r0), head_lanes(hi)], q2)
            s_dst[r0:r1, :] = blk
            for r in range(0, r1 - r0, DIFF_ROWS):
                part = blk[r:r + DIFF_ROWS]
                m_acc = part if m_acc is None else jnp.maximum(m_acc, part)
        return jnp.max(m_acc, axis=0, keepdims=True)

    def probs(s_src, m, p_dst, lo=0):
        p_dst[lo:NK, :] = jnp.exp2(s_src[lo:NK, :] - m).astype(BF16)

    def attend(p_src, hi, lo=0):
        acc = _dot(vt_scr[hi, :, lo:NK], p_src[lo:NK, :])
        o_n = acc[0:dv] * (1.0 / acc[dv:dv + 1])
        o_t = o_n[:, 0:TQ] - lam * o_n[:, TQ:2 * TQ]
        ms = jnp.mean(o_t * o_t, axis=0, keepdims=True)
        y_t = o_t * (lax.rsqrt(ms + EPS) * (1.0 - lambda_init))
        return (y_t.T * g_ref[...]).astype(BF16)

    def block_rows(i):
        return pl.ds(i * TQ, TQ) if isinstance(i, int) else pl.ds(pl.multiple_of(i * TQ, TQ), TQ)

    def q_block(hi, i):
        return q_ref[block_rows(i), head_lanes(hi)]

    def emit(hi, i, y):
        o_ref[block_rows(i), head_lanes(hi)] = y

    def pair(hi, b, m_b):
        m_a = scores(q_block(hi, b + 2), hi, s_a)
        probs(s_b, m_b, p_b)
        emit(hi, b, attend(p_a, hi))
        m_b = scores(q_block(hi, b + 3), hi, s_b)
        probs(s_a, m_a, p_a)
        emit(hi, b + 1, attend(p_b, hi))
        return m_b

    nb = S // TQ
    n_pairs = nb // 2 - 1
    m_a = scores(q_block(0, 0), 0, s_a)
    m_b = scores(q_block(0, 1), 0, s_b)
    probs(s_a, m_a, p_a)
    for hi in range(nhs):
        m_b = lax.fori_loop(0, n_pairs, lambda j, m_b, hi=hi: pair(hi, 2 * j, m_b), m_b)
        if hi + 1 < nhs:
            m_a = scores(q_block(hi + 1, 0), hi + 1, s_a)
            probs(s_b, m_b, p_b)
            emit(hi, nb - 2, attend(p_a, hi))
            m_b = scores(q_block(hi + 1, 1), hi + 1, s_b)
            probs(s_a, m_a, p_a)
            emit(hi, nb - 1, attend(p_b, hi))
        else:
            probs(s_b, m_b, p_b)
            emit(hi, nb - 2, attend(p_a, hi))
            emit(hi, nb - 1, attend(p_b, hi))

    for hi in range(nhs):
        m_c = scores(qc_ref[:, head_lanes(hi)], hi, s_a, S)
        probs(s_a, m_c, p_a, S)
        oc_ref[:, head_lanes(hi)] = attend(p_a, hi, S)


def _diff(qk, vt, diff_lambda, subln_g, *, B, S, Lc, lambda_init):
    H = DIFF_HEADS
    dv = 2 * HEAD_DIM
    nhs = DIFF_HEADS_STEP
    NK = S + Lc
    ctx_blk0 = B * S // Lc
    w = nhs * LANES
    return pl.pallas_call(
        functools.partial(_diff_kernel, S=S, Lc=Lc, lambda_init=lambda_init),
        grid=(B, H // nhs),
        in_specs=[
            _resident(diff_lambda.shape),
            _resident(subln_g.shape),
            pl.BlockSpec((S, w), lambda b, h: (b, COL_DQ // w + h)),
            pl.BlockSpec((S, w), lambda b, h: (b, COL_DK // w + h)),
            pl.BlockSpec((w, S), lambda b, h: (ROW_DV // w + h, b)),
            pl.BlockSpec((Lc, w), lambda b, h: (ctx_blk0 + b, COL_DQ // w + h)),
            pl.BlockSpec((Lc, w), lambda b, h: (ctx_blk0 + b, COL_DK // w + h)),
            pl.BlockSpec((w, Lc), lambda b, h: (ROW_DV // w + h, ctx_blk0 + b)),
        ],
        out_specs=[
            pl.BlockSpec((S, w), lambda b, h: (b, h)),
            pl.BlockSpec((Lc, w), lambda b, h: (b, h)),
        ],
        out_shape=[
            jax.ShapeDtypeStruct((B * S, H * dv), BF16),
            jax.ShapeDtypeStruct((B * Lc, H * dv), BF16),
        ],
        scratch_shapes=[
            pltpu.VMEM((nhs, dv + ONES_ROWS, NK), BF16),
            pltpu.VMEM((NK, 2 * ATT_TQ), F32),
            pltpu.VMEM((NK, 2 * ATT_TQ), F32),
            pltpu.VMEM((NK, 2 * ATT_TQ), BF16),
            pltpu.VMEM((NK, 2 * ATT_TQ), BF16),
        ],
        compiler_params=_cparams(("arbitrary", "arbitrary")),
        name="diff",
    )(diff_lambda, subln_g, qk, qk, vt, qk, qk, vt)


def _rope_tables(S, tm):
    rows = S // GRID_W
    row = jnp.repeat(jnp.arange(rows, dtype=F32), GRID_W)
    col = jnp.tile(jnp.arange(GRID_W, dtype=F32), rows)
    n_freq = HEAD_DIM // 4
    inv_freq = ROPE_BASE ** (-jnp.arange(n_freq, dtype=F32) / n_freq)
    ang = jnp.concatenate([row[:, None] * inv_freq, col[:, None] * inv_freq], axis=-1)
    cos, sin = jnp.cos(ang), jnp.sin(ang)
    reps = LANES // (HEAD_DIM // 2)
    sign = np.tile(np.concatenate([-np.ones(HEAD_DIM // 2), np.ones(HEAD_DIM // 2)]), LANES // HEAD_DIM)
    cos_t = jnp.concatenate([jnp.tile(cos, (1, reps)), jnp.ones((tm, LANES), F32)], axis=0)
    sin_t = jnp.concatenate([jnp.tile(sin, (1, reps)) * sign.astype(np.float32), jnp.zeros((tm, LANES), F32)], axis=0)
    return cos_t, sin_t


def _block_diag(w):
    nd, K, c, _ = w.shape
    eye = jnp.eye(K, dtype=w.dtype)
    return jnp.einsum('dkij,kl->dkilj', w, eye).reshape(nd, K * c, K * c)


def kernel(x, c, ctx, c_ctx, w_ada, b_ada, norm_g, ffn1_w_gu, ffn1_w_down, ffn2_w_gu, ffn2_w_down,
           w_in, w_out, conv_w, conv_b, lru_w_r, lru_b_r, lru_w_i, lru_b_i, lru_lambda,
           swa_sink, diff_lambda, diff_subln_g, final_g):
    B, S, D = x.shape
    Lc = ctx.shape[1]
    depth = w_ada.shape[0]
    tm = TOKEN_TILE
    assert S % tm == 0 and (B * Lc) % tm == 0 and Lc == ATT_TQ and S % GRID_W == 0 and S % LRU_CHUNK == 0
    assert Lc == LRU_CHUNK and S >= ATT_TQ + 2 * WINDOW
    assert S % DIFF_CHUNK == 0 and (S // ATT_TQ) % 2 == 0
    n_lat_tiles = B * S // tm
    n_tiles = n_lat_tiles + B * Lc // tm
    geom = (n_lat_tiles, S // tm, B)

    h = (x.reshape(B * S, D), ctx.reshape(B * Lc, D))

    n_cond = B + 1
    pad = (-n_cond) % SUBLANES
    cond = jnp.concatenate([c, c_ctx[None], jnp.zeros((pad, D), F32)], axis=0)
    mod_all = _ada(cond, w_ada, b_ada).reshape(depth, n_cond + pad, N_MOD, D)

    cos_t, sin_t = _rope_tables(S, _inproj_tile(S, B * Lc))
    w1_gu, w1_down = ffn1_w_gu.astype(BF16), ffn1_w_down.astype(BF16)
    w2_gu, w2_down = ffn2_w_gu.astype(BF16), ffn2_w_down.astype(BF16)

    for l in range(depth):
        last = l == depth - 1
        mod = mod_all[l]
        lambda_init = 0.8 - 0.6 * math.exp(-0.3 * l)
        g = norm_g[l]
        h = _ffn(h, mod, g[0:1], w1_gu, w1_down, final_g[None], layer=l, k=0, n_tiles=n_tiles, geom=geom)
        lru_in, qk, vt = _inproj(h, mod, g[1:2], w_in, cos_t, sin_t, layer=l, B=B, S=S)
        y_lru = _lru(lru_in, conv_w[l], conv_b[l][None], _block_diag(lru_w_r[l]), lru_b_r[l][:, None],
                     _block_diag(lru_w_i[l]), lru_b_i[l][:, None], lru_lambda[l][:, None], B=B, S=S, Lc=Lc)
        y_swa = _swa(qk, vt, swa_sink[l][None], B=B, S=S, Lc=Lc)
        y_diff = _diff(qk, vt, diff_lambda[l], diff_subln_g[l][None], B=B, S=S, Lc=Lc, lambda_init=lambda_init)
        n_out = n_lat_tiles if last else n_tiles
        h = _ffn(h, mod, g[2:3], w2_gu, w2_down, final_g[None], layer=l, k=2, n_tiles=n_out, geom=geom,
                 final_norm=last, mix=(y_lru, y_swa, y_diff), w_out=w_out)
    return h[:B * S].reshape(B, S, D)
```

```python
import functools
import math

import numpy as np
import jax
import jax.numpy as jnp
from jax import lax
from jax.experimental import pallas as pl
from jax.experimental.pallas import tpu as pltpu

F32 = jnp.float32
BF16 = jnp.bfloat16

GRID_W = 64
N_MOD = 9
EPS = 1e-6
FFN_RES = 0.5
HEAD_DIM = 64
ROPE_BASE = 10000.0
LRU_WIDTH = 256
LRU_BLOCKS = 4
CONV_WIDTH = 4
CONV_LEFT = 2
LRU_C = 8.0
SWA_HEADS = 4
SWA_KV_HEADS = 2
WINDOW = 128
DIFF_HEADS = 4
NEG_INF = -1e30

LANES = 128
SUBLANES = 8
VMEM_LIMIT_BYTES = 56 * 1024 * 1024

TOKEN_TILE = 512
FFN_CHUNKS = 1
FFN_STAGE_COLS = 512
FFN_STAGE_ROWS = 128
ADA_COLS = 1024
ATT_TQ = 256
LRU_CHUNK = 256

QK_WIDTH = 1408
COL_SQ, COL_DQ, COL_DK, COL_SK = 0, 256, 768, 1280
VT_ROWS = 640
ROW_DV, ROW_SV = 0, 512
IN_LRU, IN_SQ, IN_SK, IN_SV, IN_DQ, IN_DK, IN_DV = 0, 512, 768, 896, 1024, 1536, 2048
LOG2E = math.log2(math.e)
DIFF_ROWS = 16
DIFF_CHUNK = 1024
DIFF_HEADS_STEP = 2
DIFF_PAIRS_PER_TRIP = 1
ONES_ROWS = 16
SWA_HEAD_ORDER = (0, 2, 1, 3)


def _cparams(semantics):
    return pltpu.CompilerParams(dimension_semantics=semantics, vmem_limit_bytes=VMEM_LIMIT_BYTES)


def _resident(shape):
    nd = len(shape)
    return pl.BlockSpec(shape, lambda *_: (0,) * nd, pipeline_mode=pl.Buffered(1))


def _layer_resident(stacked, layer):
    nd = stacked.ndim - 1
    return pl.BlockSpec((None,) + stacked.shape[1:], lambda *_: (layer,) + (0,) * nd, pipeline_mode=pl.Buffered(1))


def _dot(a, b):
    return jnp.dot(a, b, preferred_element_type=F32)


def _dot_nt(a, b):
    return lax.dot_general(a, b, (((1,), (1,)), ((), ())), preferred_element_type=F32)


def _rms(x, g):
    return x * lax.rsqrt(jnp.mean(x * x, axis=-1, keepdims=True) + EPS) * g


def _modnorm(h, g, mod_ref, k):
    shift = mod_ref[0, 3 * k:3 * k + 1, :]
    scale = mod_ref[0, 3 * k + 1:3 * k + 2, :]
    return _rms(h, g) * (1.0 + scale) + shift


def _neg_expm1_2x(a, x):
    return (1.0 + a * a) * jnp.tanh(-x)


def _ada_kernel(c_ref, w_ref, b_ref, o_ref):
    c = c_ref[...]
    s = (c * jax.nn.sigmoid(c)).astype(BF16)
    o_ref[0] = _dot(s, w_ref[0].astype(BF16)) + b_ref[0]


def _ada(cond, w_ada, b_ada):
    L, D, N = w_ada.shape
    R = cond.shape[0]
    return pl.pallas_call(
        _ada_kernel,
        grid=(L, N // ADA_COLS),
        in_specs=[
            pl.BlockSpec((R, D), lambda l, n: (0, 0)),
            pl.BlockSpec((1, D, ADA_COLS), lambda l, n: (l, 0, n)),
            pl.BlockSpec((1, 1, ADA_COLS), lambda l, n: (l, 0, n)),
        ],
        out_specs=pl.BlockSpec((1, R, ADA_COLS), lambda l, n: (l, 0, n)),
        out_shape=jax.ShapeDtypeStruct((L, R, N), F32),
        compiler_params=_cparams(("arbitrary", "arbitrary")),
        name="ada",
    )(cond, w_ada, b_ada.reshape(L, 1, N))


def _mod_index(n_lat_tiles, tiles_per_batch, n_batch):
    def idx(i):
        return (jnp.where(i < n_lat_tiles, i // tiles_per_batch, n_batch), 0, 0)
    return idx


def _stream_to_bf16(n_chunks, src_chunk, dst_chunk, stage, sem):
    def copy(c):
        return pltpu.make_async_copy(src_chunk(c), stage.at[c % 2], sem.at[c % 2])

    copy(0).start()
    for c in range(n_chunks):
        if c + 1 < n_chunks:
            copy(c + 1).start()
        copy(c).wait()
        dst_chunk(c)[...] = stage[c % 2].astype(BF16)


def _ffn_kernel(*refs, layer, k, d_ff, final_norm, n_lat_tiles, split_input, n_mix, mix_has_ctx):
    if n_mix:
        wo_bf = refs[-1]
        refs = refs[:-1]
    wgu_ref, wd_ref, st_gu, st_d, sem_gu, sem_d = refs[-6:]
    refs = refs[:-6]
    mod_ref, g_ref, wgu_hbm, wd_hbm, fg_ref, o_ref = refs[-6:]

    @pl.when(pl.program_id(0) == 0)
    def _():
        cw = st_gu.shape[2]
        rw = st_d.shape[1]
        _stream_to_bf16(2 * d_ff // cw, lambda c: wgu_hbm.at[layer, :, pl.ds(c * cw, cw)],
                        lambda c: wgu_ref.at[:, pl.ds(c * cw, cw)], st_gu, sem_gu)
        _stream_to_bf16(d_ff // rw, lambda c: wd_hbm.at[layer, pl.ds(c * rw, rw), :],
                        lambda c: wd_ref.at[pl.ds(c * rw, rw), :], st_d, sem_d)
        if n_mix:
            wo_bf[...] = refs[-7][...].astype(BF16)

    is_lat = pl.program_id(0) < n_lat_tiles
    if split_input:
        h = jnp.where(is_lat, refs[0][...], refs[1][...])
    else:
        h = refs[0][...]
    if n_mix:
        mix_refs = refs[-7 - 2 * n_mix:-7]
        parts = []
        for lat_ref, ctx_ref in zip(mix_refs[0::2], mix_refs[1::2]):
            parts.append(jnp.where(is_lat, lat_ref[...], ctx_ref[...]) if mix_has_ctx else lat_ref[...])
        h = h + mod_ref[0, 5:6, :] * _dot(jnp.concatenate(parts, axis=1), wo_bf[...])
    xn = _modnorm(h, g_ref[...], mod_ref, k).astype(BF16)
    tf = d_ff // FFN_CHUNKS
    acc = None
    for c in range(FFN_CHUNKS):
        lo = c * tf
        g = _dot(xn, wgu_ref[:, lo:lo + tf])
        u = _dot(xn, wgu_ref[:, d_ff + lo:d_ff + lo + tf])
        a = (g * jax.nn.sigmoid(g) * u).astype(BF16)
        part = _dot(a, wd_ref[lo:lo + tf, :])
        acc = part if acc is None else acc + part
    gate = mod_ref[0, 3 * k + 2:3 * k + 3, :]
    out = h + (FFN_RES * gate) * acc
    if final_norm:
        out = _rms(out, fg_ref[...])
    o_ref[...] = out


def _ffn(h, mod, g, w_gu, w_down, final_g, *, layer, k, n_tiles, geom, final_norm=False, mix=(), w_out=None):
    split = isinstance(h, tuple)
    hs = h if split else (h,)
    D = hs[0].shape[1]
    d_ff = w_down.shape[1]
    tm = TOKEN_TILE
    n_lat_tiles = geom[0]

    def lat_idx(i):
        return (jnp.minimum(i, n_lat_tiles - 1), 0)

    def ctx_idx(i):
        return (jnp.maximum(i - n_lat_tiles, 0), 0)

    if split:
        h_specs = [pl.BlockSpec((tm, D), lat_idx), pl.BlockSpec((tm, D), ctx_idx)]
    else:
        h_specs = [pl.BlockSpec((tm, D), lambda i: (i, 0))]
    mix_args, mix_specs = [], []
    for lat, ctx in mix:
        mix_args += [lat, ctx]
        mix_specs += [pl.BlockSpec((tm, lat.shape[1]), lat_idx), pl.BlockSpec((tm, lat.shape[1]), ctx_idx)]
    if mix:
        mix_args.append(w_out)
        mix_specs.append(_layer_resident(w_out, layer))
    return pl.pallas_call(
        functools.partial(_ffn_kernel, layer=layer, k=k, d_ff=d_ff, final_norm=final_norm, n_lat_tiles=n_lat_tiles,
                          split_input=split, n_mix=len(mix), mix_has_ctx=n_tiles > n_lat_tiles),
        grid=(n_tiles,),
        in_specs=h_specs + mix_specs + [
            pl.BlockSpec((1, N_MOD, D), _mod_index(*geom)),
            _resident((1, D)),
            pl.BlockSpec(memory_space=pl.ANY),
            pl.BlockSpec(memory_space=pl.ANY),
            _resident((1, D)),
        ],
        out_specs=pl.BlockSpec((tm, D), lambda i: (i, 0)),
        out_shape=jax.ShapeDtypeStruct((n_tiles * tm, D), F32),
        scratch_shapes=[
            pltpu.VMEM((D, 2 * d_ff), BF16),
            pltpu.VMEM((d_ff, D), BF16),
            pltpu.VMEM((2, D, FFN_STAGE_COLS), F32),
            pltpu.VMEM((2, FFN_STAGE_ROWS, D), F32),
            pltpu.SemaphoreType.DMA((2,)),
            pltpu.SemaphoreType.DMA((2,)),
        ] + ([pltpu.VMEM(w_out.shape[1:], BF16)] if mix else []),
        compiler_params=_cparams(("arbitrary",)),
        name="ffn",
    )(*hs, *mix_args, mod, g, w_gu, w_down, final_g)


def _rope(x, cos, sin):
    lane = lax.broadcasted_iota(jnp.int32, x.shape, 1)
    first = (lane & (HEAD_DIM - 1)) < (HEAD_DIM // 2)
    partner = jnp.where(first, pltpu.roll(x, LANES - HEAD_DIM // 2, 1), pltpu.roll(x, HEAD_DIM // 2, 1))
    return x * cos + partner * sin


def _inproj_kernel(h_ref, mod_ref, g_ref, w_ref, cos_ref, sin_ref, lru_ref, qk_ref, vt_ref, w_bf):
    @pl.when(pl.program_id(0) == 0)
    def _():
        w_bf[...] = w_ref[...].astype(BF16)

    xn = _modnorm(h_ref[...], g_ref[...], mod_ref, 1).astype(BF16)
    y = _dot(xn, w_bf[...])
    lru_ref[...] = y[:, IN_LRU:IN_SQ]
    cos = cos_ref[...]
    sin = sin_ref[...]
    q_scale = HEAD_DIM ** -0.5 * LOG2E

    def roped(c0, scale=None):
        blk = _rope(y[:, c0:c0 + LANES], cos, sin)
        return blk if scale is None else blk * scale

    def put(c0, blk):
        qk_ref[:, c0:c0 + LANES] = blk.astype(BF16)

    qa, qb = roped(IN_SQ, q_scale), roped(IN_SQ + LANES, q_scale)
    low = lax.broadcasted_iota(jnp.int32, qa.shape, 1) < HEAD_DIM
    put(COL_SQ, jnp.where(low, qa, pltpu.roll(qb, HEAD_DIM, 1)))
    put(COL_SQ + LANES, jnp.where(low, pltpu.roll(qa, HEAD_DIM, 1), qb))
    put(COL_SK, roped(IN_SK))
    for c in range(0, DIFF_HEADS * 2 * HEAD_DIM, LANES):
        put(COL_DQ + c, roped(IN_DQ + c, q_scale))
        put(COL_DK + c, roped(IN_DK + c))
    sv_w = SWA_KV_HEADS * HEAD_DIM
    vt_ref[ROW_SV:ROW_SV + sv_w, :] = y[:, IN_SV:IN_SV + sv_w].T.astype(BF16)
    dv_w = DIFF_HEADS * 2 * HEAD_DIM
    vt_ref[ROW_DV:ROW_DV + dv_w, :] = y[:, IN_DV:IN_DV + dv_w].T.astype(BF16)


def _inproj_tile(S, n_ctx_rows):
    big = 2 * TOKEN_TILE
    return big if S % big == 0 and n_ctx_rows % big == 0 else TOKEN_TILE


def _inproj(h, mod, g, w_in, cos_t, sin_t, *, layer, B, S):
    T, D = h.shape
    tm = _inproj_tile(S, T - B * S)
    n_tiles = T // tm
    geom = (B * S // tm, S // tm, B)
    n_lat_tiles, tiles_per_batch, _ = geom

    def tab_idx(i):
        return (jnp.where(i < n_lat_tiles, i % tiles_per_batch, tiles_per_batch), 0)

    return pl.pallas_call(
        _inproj_kernel,
        grid=(n_tiles,),
        in_specs=[
            pl.BlockSpec((tm, D), lambda i: (i, 0)),
            pl.BlockSpec((1, N_MOD, D), _mod_index(*geom)),
            _resident((1, D)),
            _layer_resident(w_in, layer),
            pl.BlockSpec((tm, LANES), tab_idx),
            pl.BlockSpec((tm, LANES), tab_idx),
        ],
        out_specs=[
            pl.BlockSpec((tm, 2 * LRU_WIDTH), lambda i: (i, 0)),
            pl.BlockSpec((tm, QK_WIDTH), lambda i: (i, 0)),
            pl.BlockSpec((VT_ROWS, tm), lambda i: (0, i)),
        ],
        out_shape=[
            jax.ShapeDtypeStruct((T, 2 * LRU_WIDTH), F32),
            jax.ShapeDtypeStruct((T, QK_WIDTH), BF16),
            jax.ShapeDtypeStruct((VT_ROWS, T), BF16),
        ],
        scratch_shapes=[pltpu.VMEM(w_in.shape[1:], BF16)],
        compiler_params=_cparams(("arbitrary",)),
        name="inproj",
    )(h, mod, g, w_in, cos_t, sin_t)


def _lru_kernel(lat_ref, ctx_ref, cw_ref, cb_ref, wr_ref, br_ref, wi_ref, bi_ref, lam_ref,
                ylat_ref, yctx_ref, xpad, u_scr, hf_scr, ge_scr, *g_scr, S, Lc):
    W = LRU_WIDTH
    TC = LRU_CHUNK
    PAD = SUBLANES
    row = lax.broadcasted_iota(jnp.int32, (TC, W), 0)
    group = lax.broadcasted_iota(jnp.int32, (TC // SUBLANES, W), 0)

    def conv_chunk(start):
        a = xpad[pl.ds(start, TC + 2 * PAD), :]
        u = cb_ref[...] + jnp.zeros((TC, W), F32)
        for k in range(CONV_WIDTH):
            sh = (CONV_LEFT - k) % (TC + 2 * PAD)
            r = a if sh == 0 else pltpu.roll(a, sh, 0)
            u = u + r[PAD:PAD + TC] * cw_ref[k:k + 1, :]
        return u

    def scan_chunk(u, d, carry, reverse):
        r = jax.nn.sigmoid(_dot(u, wr_ref[d]) + br_ref[d])
        i = jax.nn.sigmoid(_dot(u, wi_ref[d]) + bi_ref[d])
        log_a = (-LRU_C) * r * jax.nn.softplus(-lam_ref[d])
        A = jnp.exp(log_a)
        Bv = jnp.sqrt(_neg_expm1_2x(A, log_a)) * i * u

        def combine(A, Bv, pos, axis):
            n = A.shape[axis]
            s = 1
            while s < n:
                if reverse:
                    keep = pos < n - s
                    sh = n - s
                else:
                    keep = pos >= s
                    sh = s
                a_sh = jnp.where(keep, pltpu.roll(A, sh, axis), 1.0)
                b_sh = jnp.where(keep, pltpu.roll(Bv, sh, axis), 0.0)
                Bv = A * b_sh + Bv
                A = A * a_sh
                s *= 2
            return A, Bv

        G = SUBLANES
        ng = TC // G
        A, Bv = combine(A.reshape(ng, G, W), Bv.reshape(ng, G, W), lax.broadcasted_iota(jnp.int32, (ng, G, W), 1), 1)
        A = A.reshape(TC, W)
        Bv = Bv.reshape(TC, W)
        edge = 0 if reverse else G - 1

        def boundary_rows(planes, x):
            out = []
            for c, scr in enumerate(planes):
                scr[...] = x[:, c * LANES:(c + 1) * LANES]
                out.append(scr[pl.ds(edge, ng, stride=G), :])
            return jnp.concatenate(out, axis=1)

        n_pl = W // LANES
        Ag, Bg = combine(boundary_rows(g_scr[:n_pl], A), boundary_rows(g_scr[n_pl:], Bv), group, 0)
        ge_scr[...] = Ag * carry + Bg
        parts = []
        for g in range(ng):
            nb_g = g + 1 if reverse else g - 1
            h_in = carry if (nb_g < 0 or nb_g >= ng) else ge_scr[nb_g:nb_g + 1, :]
            parts.append(A[g * G:(g + 1) * G] * h_in + Bv[g * G:(g + 1) * G])
        h = jnp.concatenate(parts, axis=0)
        return h, (h[0:1] if reverse else h[TC - 1:TC])

    def gelu(x):
        return jax.nn.gelu(x)

    zero_pad = jnp.zeros((PAD, W), F32)
    zero_state = jnp.zeros((1, W), F32)

    xpad[0:PAD, :] = zero_pad
    xpad[PAD:PAD + Lc, :] = ctx_ref[:, 0:W]
    xpad[PAD + Lc:2 * PAD + Lc, :] = zero_pad
    uc = conv_chunk(0)
    hcf, carry_f = scan_chunk(uc, 0, zero_state, False)
    hcb, carry_b = scan_chunk(uc, 1, zero_state, True)
    yctx_ref[...] = ((hcf + hcb) * gelu(ctx_ref[:, W:2 * W])).astype(BF16)

    xpad[PAD:PAD + S, :] = lat_ref[:, 0:W]
    xpad[PAD + S:2 * PAD + S, :] = zero_pad
    nc = S // TC

    def conv_body(c, _):
        st = pl.multiple_of(c * TC, TC)
        u_scr[pl.ds(st, TC), :] = conv_chunk(st)
        return 0

    lax.fori_loop(0, nc, conv_body, 0)

    def fwd_body(c, carry):
        st = pl.multiple_of(c * TC, TC)
        h, carry = scan_chunk(u_scr[pl.ds(st, TC), :], 0, carry, False)
        hf_scr[pl.ds(st, TC), :] = h
        return carry

    lax.fori_loop(0, nc, fwd_body, carry_f)

    def bwd_body(c, carry):
        st = pl.multiple_of((nc - 1 - c) * TC, TC)
        h, carry = scan_chunk(u_scr[pl.ds(st, TC), :], 1, carry, True)
        y = (hf_scr[pl.ds(st, TC), :] + h) * gelu(lat_ref[pl.ds(st, TC), W:2 * W])
        ylat_ref[pl.ds(st, TC), :] = y.astype(BF16)
        return carry

    lax.fori_loop(0, nc, bwd_body, carry_b)


def _lru(lru_in, conv_w, conv_b, w_r, b_r, w_i, b_i, lam, *, B, S, Lc):
    W = LRU_WIDTH
    t_lat = B * S
    ctx_blk0 = t_lat // Lc
    return pl.pallas_call(
        functools.partial(_lru_kernel, S=S, Lc=Lc),
        grid=(B,),
        in_specs=[
            pl.BlockSpec((S, 2 * W), lambda b: (b, 0)),
            pl.BlockSpec((Lc, 2 * W), lambda b: (ctx_blk0 + b, 0)),
            _resident(conv_w.shape),
            _resident(conv_b.shape),
            _resident(w_r.shape),
            _resident(b_r.shape),
            _resident(w_i.shape),
            _resident(b_i.shape),
            _resident(lam.shape),
        ],
        out_specs=[
            pl.BlockSpec((S, W), lambda b: (b, 0)),
            pl.BlockSpec((Lc, W), lambda b: (b, 0)),
        ],
        out_shape=[
            jax.ShapeDtypeStruct((t_lat, W), BF16),
            jax.ShapeDtypeStruct((B * Lc, W), BF16),
        ],
        scratch_shapes=[
            pltpu.VMEM((S + 2 * SUBLANES, W), F32),
            pltpu.VMEM((S, W), F32),
            pltpu.VMEM((S, W), F32),
            pltpu.VMEM((LRU_CHUNK // SUBLANES, W), F32),
        ] + [pltpu.VMEM((LRU_CHUNK, LANES), F32)] * (2 * (W // LANES)) + [
        ],
        compiler_params=_cparams(("arbitrary",)),
        name="lru",
    )(lru_in, lru_in, conv_w, conv_b, w_r, b_r, w_i, b_i, lam)


def _swa_kernel(sink_ref, q_ref, k_ref, vt_ref, qc_ref, kc_ref, vtc_ref, o_ref, oc_ref,
                vt3, s_a, s_b, p_a, p_b, *, S, Lc):
    TQ = ATT_TQ
    KB = TQ + 2 * WINDOW
    NKB = KB + Lc
    hd = HEAD_DIM
    nh = SWA_HEADS
    n_lat_ch = S // LANES
    n_ctx_ch = Lc // LANES
    band_ch = KB // LANES
    ones = jnp.ones((ONES_ROWS, LANES), BF16)
    for c in range(n_lat_ch + n_ctx_ch):
        src = vt_ref[:, c * LANES:(c + 1) * LANES] if c < n_lat_ch else \
            vtc_ref[:, (c - n_lat_ch) * LANES:(c - n_lat_ch + 1) * LANES]
        vt3[c, 0:2 * hd, :] = src
        vt3[c, 2 * hd:2 * hd + ONES_ROWS, :] = ones
    low = lax.broadcasted_iota(jnp.int32, (TQ, LANES), 1) < hd
    col = lax.broadcasted_iota(jnp.int32, (1, nh * TQ), 1)
    sink = jnp.full((1, nh * TQ), sink_ref[0, SWA_HEAD_ORDER[-1]] * LOG2E, F32)
    for c in range(nh - 2, -1, -1):
        sink = jnp.where(col < (c + 1) * TQ, sink_ref[0, SWA_HEAD_ORDER[c]] * LOG2E, sink)

    def band_start(t):
        return jnp.clip(2 * t - 1, 0, n_lat_ch - band_ch)

    def scores(q, s_dst, t=None):
        zero = jnp.zeros((TQ, LANES), BF16)
        rows = []
        for grp in range(2):
            qg = q[:, grp * LANES:(grp + 1) * LANES]
            rows += [jnp.where(low, qg, zero), jnp.where(low, zero, qg)]
        q4 = jnp.concatenate(rows, axis=0)
        s_c = _dot_nt(kc_ref[...], q4)
        s_dst[KB:NKB, :] = s_c
        m = jnp.max(s_c, axis=0, keepdims=True)
        if t is not None:
            st = band_start(t) * LANES
            kb = k_ref[pl.ds(pl.multiple_of(st, LANES), KB), :]
            k_abs = st + lax.broadcasted_iota(jnp.int32, (KB, TQ), 0)
            q_abs = t * TQ + lax.broadcasted_iota(jnp.int32, (KB, TQ), 1)
            valid = jnp.abs(q_abs - k_abs) <= WINDOW
            s_l = _dot_nt(kb, q4)
            s_l = jnp.concatenate(
                [jnp.where(valid, s_l[:, c * TQ:(c + 1) * TQ], NEG_INF) for c in range(nh)], axis=1)
            s_dst[0:KB, :] = s_l
            m = jnp.maximum(m, jnp.max(s_l, axis=0, keepdims=True))
        return jnp.maximum(m, sink)

    def probs(s_src, m, p_dst, lo=0):
        p_dst[lo:NKB, :] = jnp.exp2(s_src[lo:NKB, :] - m).astype(BF16)

    def attend(p_src, m, t=None):
        ctx_v = [vt3[n_lat_ch + c] for c in range(n_ctx_ch)]
        if t is None:
            lo, vt = KB, jnp.concatenate(ctx_v, axis=1)
        else:
            st = band_start(t)
            lo, vt = 0, jnp.concatenate([vt3[st + d] for d in range(band_ch)] + ctx_v, axis=1)
        acc = _dot(vt, p_src[lo:NKB, :])
        den = acc[2 * hd:2 * hd + 1] + jnp.exp2(sink - m)
        o_n = acc[0:2 * hd] * (1.0 / den)
        pieces = []
        for h in range(nh):
            c = SWA_HEAD_ORDER.index(h)
            pieces.append(o_n[(c % 2) * hd:(c % 2 + 1) * hd, c * TQ:(c + 1) * TQ])
        return jnp.concatenate(pieces, axis=0).T.astype(BF16)

    def q_block(i):
        return q_ref[pl.ds(pl.multiple_of(i * TQ, TQ), TQ), :]

    def emit(i, y):
        o_ref[pl.ds(pl.multiple_of(i * TQ, TQ), TQ), :] = y

    nb = S // TQ
    m_a = scores(q_block(0), s_a, 0)
    m_b = scores(q_block(1), s_b, 1)
    probs(s_a, m_a, p_a)

    def pair(j, carry):
        m_b, m_pa = carry
        b = 2 * j
        m_a = scores(q_block(b + 2), s_a, b + 2)
        probs(s_b, m_b, p_b)
        emit(b, attend(p_a, m_pa, b))
        m_b2 = scores(q_block(b + 3), s_b, b + 3)
        probs(s_a, m_a, p_a)
        emit(b + 1, attend(p_b, m_b, b + 1))
        return m_b2, m_a

    m_b, m_pa = lax.fori_loop(0, nb // 2 - 1, pair, (m_b, m_a))
    probs(s_b, m_b, p_b)
    emit(nb - 2, attend(p_a, m_pa, nb - 2))
    emit(nb - 1, attend(p_b, m_b, nb - 1))

    m_c = scores(qc_ref[...], s_a)
    probs(s_a, m_c, p_a, KB)
    oc_ref[...] = attend(p_a, m_c)


def _swa(qk, vt, sink, *, B, S, Lc):
    qw = SWA_HEADS * HEAD_DIM
    ctx_blk0 = B * S // Lc
    sv_blk = ROW_SV // LANES
    NKB = ATT_TQ + 2 * WINDOW + Lc
    return pl.pallas_call(
        functools.partial(_swa_kernel, S=S, Lc=Lc),
        grid=(B,),
        in_specs=[
            pl.BlockSpec(memory_space=pltpu.SMEM),
            pl.BlockSpec((S, qw), lambda b: (b, COL_SQ // qw)),
            pl.BlockSpec((S, LANES), lambda b: (b, COL_SK // LANES)),
            pl.BlockSpec((LANES, S), lambda b: (sv_blk, b)),
            pl.BlockSpec((Lc, qw), lambda b: (ctx_blk0 + b, COL_SQ // qw)),
            pl.BlockSpec((Lc, LANES), lambda b: (ctx_blk0 + b, COL_SK // LANES)),
            pl.BlockSpec((LANES, Lc), lambda b: (sv_blk, ctx_blk0 + b)),
        ],
        out_specs=[
            pl.BlockSpec((S, qw), lambda b: (b, 0)),
            pl.BlockSpec((Lc, qw), lambda b: (b, 0)),
        ],
        out_shape=[
            jax.ShapeDtypeStruct((B * S, qw), BF16),
            jax.ShapeDtypeStruct((B * Lc, qw), BF16),
        ],
        scratch_shapes=[
            pltpu.VMEM(((S + Lc) // LANES, 2 * HEAD_DIM + ONES_ROWS, LANES), BF16),
            pltpu.VMEM((NKB, SWA_HEADS * ATT_TQ), F32),
            pltpu.VMEM((NKB, SWA_HEADS * ATT_TQ), F32),
            pltpu.VMEM((NKB, SWA_HEADS * ATT_TQ), BF16),
            pltpu.VMEM((NKB, SWA_HEADS * ATT_TQ), BF16),
        ],
        compiler_params=_cparams(("arbitrary",)),
        name="swa",
    )(sink, qk, qk, vt, qk, qk, vt)


def _diff_kernel(dl_ref, g_ref, q_ref, k_ref, vt_ref, qc_ref, kc_ref, vtc_ref, o_ref, oc_ref,
                 vt_scr, s_a, s_b, p_a, p_b, *, S, Lc, lambda_init):
    TQ = ATT_TQ
    NK = S + Lc
    dv = 2 * HEAD_DIM
    nhs = DIFF_HEADS_STEP
    for hi in range(nhs):
        vt_scr[hi, 0:dv, 0:S] = vt_ref[hi * dv:(hi + 1) * dv, :]
        vt_scr[hi, 0:dv, S:NK] = vtc_ref[hi * dv:(hi + 1) * dv, :]
        vt_scr[hi, dv:dv + ONES_ROWS, :] = jnp.ones((ONES_ROWS, NK), BF16)
    dl = dl_ref[...]
    lam = (jnp.exp(jnp.sum(dl[0:1] * dl[1:2], axis=-1, keepdims=True))
           - jnp.exp(jnp.sum(dl[2:3] * dl[3:4], axis=-1, keepdims=True)) + lambda_init)
    low = lax.broadcasted_iota(jnp.int32, (TQ, LANES), 1) < HEAD_DIM
    lat_pieces = [(r, r + DIFF_CHUNK, k_ref, r) for r in range(0, S, DIFF_CHUNK)]
    ctx_piece = (S, NK, kc_ref, 0)

    def head_lanes(hi):
        return slice(hi * LANES, (hi + 1) * LANES)

    def scores(q, hi, s_dst, lo=0):
        zero = jnp.zeros_like(q)
        q2 = jnp.concatenate([jnp.where(low, q, zero), jnp.where(low, zero, q)], axis=0)
        m_acc = None
        for r0, r1, kref, off in ([] if lo else lat_pieces) + [ctx_piece]:
            blk = _dot_nt(kref[off:off + (r1 - r0), head_lanes(hi)], q2)
            s_dst[r0:r1, :] = blk
            for r in range(0, r1 - r0, DIFF_ROWS):
                part = blk[r:r + DIFF_ROWS]
                m_acc = part if m_acc is None else jnp.maximum(m_acc, part)
        return jnp.max(m_acc, axis=0, keepdims=True)

    def probs(s_src, m, p_dst, lo=0):
        p_dst[lo:NK, :] = jnp.exp2(s_src[lo:NK, :] - m).astype(BF16)

    def attend(p_src, hi, lo=0):
        acc = _dot(vt_scr[hi, :, lo:NK], p_src[lo:NK, :])
        o_n = acc[0:dv] * (1.0 / acc[dv:dv + 1])
        o_t = o_n[:, 0:TQ] - lam * o_n[:, TQ:2 * TQ]
        ms = jnp.mean(o_t * o_t, axis=0, keepdims=True)
        y_t = o_t * (lax.rsqrt(ms + EPS) * (1.0 - lambda_init))
        return (y_t.T * g_ref[...]).astype(BF16)

    def block_rows(i):
        return pl.ds(i * TQ, TQ) if isinstance(i, int) else pl.ds(pl.multiple_of(i * TQ, TQ), TQ)

    def q_block(hi, i):
        return q_ref[block_rows(i), head_lanes(hi)]

    def emit(hi, i, y):
        o_ref[block_rows(i), head_lanes(hi)] = y

    def pair(hi, b, m_b):
        m_a = scores(q_block(hi, b + 2), hi, s_a)
        probs(s_b, m_b, p_b)
        emit(hi, b, attend(p_a, hi))
        m_b = scores(q_block(hi, b + 3), hi, s_b)
        probs(s_a, m_a, p_a)
        emit(hi, b + 1, attend(p_b, hi))
        return m_b

    nb = S // TQ
    n_pairs = nb // 2 - 1
    n_trips = n_pairs // DIFF_PAIRS_PER_TRIP
    m_a = scores(q_block(0, 0), 0, s_a)
    m_b = scores(q_block(0, 1), 0, s_b)
    probs(s_a, m_a, p_a)
    for hi in range(nhs):
        def trip(j, m_b, hi=hi):
            for r in range(DIFF_PAIRS_PER_TRIP):
                m_b = pair(hi, 2 * (DIFF_PAIRS_PER_TRIP * j + r), m_b)
            return m_b

        m_b = lax.fori_loop(0, n_trips, trip, m_b)
        for r in range(n_trips * DIFF_PAIRS_PER_TRIP, n_pairs):
            m_b = pair(hi, 2 * r, m_b)
        if hi + 1 < nhs:
            m_a = scores(q_block(hi + 1, 0), hi + 1, s_a)
            probs(s_b, m_b, p_b)
            emit(hi, nb - 2, attend(p_a, hi))
            m_b = scores(q_block(hi + 1, 1), hi + 1, s_b)
            probs(s_a, m_a, p_a)
            emit(hi, nb - 1, attend(p_b, hi))
        else:
            probs(s_b, m_b, p_b)
            emit(hi, nb - 2, attend(p_a, hi))
            emit(hi, nb - 1, attend(p_b, hi))

    for hi in range(nhs):
        m_c = scores(qc_ref[:, head_lanes(hi)], hi, s_a, S)
        probs(s_a, m_c, p_a, S)
        oc_ref[:, head_lanes(hi)] = attend(p_a, hi, S)


def _diff(qk, vt, diff_lambda, subln_g, *, B, S, Lc, lambda_init):
    H = DIFF_HEADS
    dv = 2 * HEAD_DIM
    nhs = DIFF_HEADS_STEP
    NK = S + Lc
    ctx_blk0 = B * S // Lc
    w = nhs * LANES
    return pl.pallas_call(
        functools.partial(_diff_kernel, S=S, Lc=Lc, lambda_init=lambda_init),
        grid=(B, H // nhs),
        in_specs=[
            _resident(diff_lambda.shape),
            _resident(subln_g.shape),
            pl.BlockSpec((S, w), lambda b, h: (b, COL_DQ // w + h)),
            pl.BlockSpec((S, w), lambda b, h: (b, COL_DK // w + h)),
            pl.BlockSpec((w, S), lambda b, h: (ROW_DV // w + h, b)),
            pl.BlockSpec((Lc, w), lambda b, h: (ctx_blk0 + b, COL_DQ // w + h)),
            pl.BlockSpec((Lc, w), lambda b, h: (ctx_blk0 + b, COL_DK // w + h)),
            pl.BlockSpec((w, Lc), lambda b, h: (ROW_DV // w + h, ctx_blk0 + b)),
        ],
        out_specs=[
            pl.BlockSpec((S, w), lambda b, h: (b, h)),
            pl.BlockSpec((Lc, w), lambda b, h: (b, h)),
        ],
        out_shape=[
            jax.ShapeDtypeStruct((B * S, H * dv), BF16),
            jax.ShapeDtypeStruct((B * Lc, H * dv), BF16),
        ],
        scratch_shapes=[
            pltpu.VMEM((nhs, dv + ONES_ROWS, NK), BF16),
            pltpu.VMEM((NK, 2 * ATT_TQ), F32),
            pltpu.VMEM((NK, 2 * ATT_TQ), F32),
            pltpu.VMEM((NK, 2 * ATT_TQ), BF16),
            pltpu.VMEM((NK, 2 * ATT_TQ), BF16),
        ],
        compiler_params=_cparams(("arbitrary", "arbitrary")),
        name="diff",
    )(diff_lambda, subln_g, qk, qk, vt, qk, qk, vt)


def _rope_tables(S, tm):
    rows = S // GRID_W
    row = jnp.repeat(jnp.arange(rows, dtype=F32), GRID_W)
    col = jnp.tile(jnp.arange(GRID_W, dtype=F32), rows)
    n_freq = HEAD_DIM // 4
    inv_freq = ROPE_BASE ** (-jnp.arange(n_freq, dtype=F32) / n_freq)
    ang = jnp.concatenate([row[:, None] * inv_freq, col[:, None] * inv_freq], axis=-1)
    cos, sin = jnp.cos(ang), jnp.sin(ang)
    reps = LANES // (HEAD_DIM // 2)
    sign = np.tile(np.concatenate([-np.ones(HEAD_DIM // 2), np.ones(HEAD_DIM // 2)]), LANES // HEAD_DIM)
    cos_t = jnp.concatenate([jnp.tile(cos, (1, reps)), jnp.ones((tm, LANES), F32)], axis=0)
    sin_t = jnp.concatenate([jnp.tile(sin, (1, reps)) * sign.astype(np.float32), jnp.zeros((tm, LANES), F32)], axis=0)
    return cos_t, sin_t


def _block_diag(w):
    nd, K, c, _ = w.shape
    eye = jnp.eye(K, dtype=w.dtype)
    return jnp.einsum('dkij,kl->dkilj', w, eye).reshape(nd, K * c, K * c)


def kernel(x, c, ctx, c_ctx, w_ada, b_ada, norm_g, ffn1_w_gu, ffn1_w_down, ffn2_w_gu, ffn2_w_down,
           w_in, w_out, conv_w, conv_b, lru_w_r, lru_b_r, lru_w_i, lru_b_i, lru_lambda,
           swa_sink, diff_lambda, diff_subln_g, final_g):
    B, S, D = x.shape
    Lc = ctx.shape[1]
    depth = w_ada.shape[0]
    tm = TOKEN_TILE
    assert S % tm == 0 and (B * Lc) % tm == 0 and Lc == ATT_TQ and S % GRID_W == 0 and S % LRU_CHUNK == 0
    assert Lc == LRU_CHUNK and S >= ATT_TQ + 2 * WINDOW
    assert S % DIFF_CHUNK == 0 and (S // ATT_TQ) % 2 == 0
    n_lat_tiles = B * S // tm
    n_tiles = n_lat_tiles + B * Lc // tm
    geom = (n_lat_tiles, S // tm, B)

    h = (x.reshape(B * S, D), ctx.reshape(B * Lc, D))

    n_cond = B + 1
    pad = (-n_cond) % SUBLANES
    cond = jnp.concatenate([c, c_ctx[None], jnp.zeros((pad, D), F32)], axis=0)
    mod_all = _ada(cond, w_ada, b_ada).reshape(depth, n_cond + pad, N_MOD, D)

    cos_t, sin_t = _rope_tables(S, _inproj_tile(S, B * Lc))
    w1_gu, w1_down, w2_gu, w2_down = ffn1_w_gu, ffn1_w_down, ffn2_w_gu, ffn2_w_down

    for l in range(depth):
        last = l == depth - 1
        mod = mod_all[l]
        lambda_init = 0.8 - 0.6 * math.exp(-0.3 * l)
        g = norm_g[l]
        h = _ffn(h, mod, g[0:1], w1_gu, w1_down, final_g[None], layer=l, k=0, n_tiles=n_tiles, geom=geom)
        lru_in, qk, vt = _inproj(h, mod, g[1:2], w_in, cos_t, sin_t, layer=l, B=B, S=S)
        y_lru = _lru(lru_in, conv_w[l], conv_b[l][None], _block_diag(lru_w_r[l]), lru_b_r[l][:, None],
                     _block_diag(lru_w_i[l]), lru_b_i[l][:, None], lru_lambda[l][:, None], B=B, S=S, Lc=Lc)
        y_swa = _swa(qk, vt, swa_sink[l][None], B=B, S=S, Lc=Lc)
        y_diff = _diff(qk, vt, diff_lambda[l], diff_subln_g[l][None], B=B, S=S, Lc=Lc, lambda_init=lambda_init)
        n_out = n_lat_tiles if last else n_tiles
        h = _ffn(h, mod, g[2:3], w2_gu, w2_down, final_g[None], layer=l, k=2, n_tiles=n_out, geom=geom,
                 final_norm=last, mix=(y_lru, y_swa, y_diff), w_out=w_out)
    return h[:B * S].reshape(B, S, D)
```

```python
import functools
import math

import numpy as np
import jax
import jax.numpy as jnp
from jax import lax
from jax.experimental import pallas as pl
from jax.experimental.pallas import tpu as pltpu

F32 = jnp.float32
BF16 = jnp.bfloat16

GRID_W = 64
N_MOD = 9
EPS = 1e-6
FFN_RES = 0.5
HEAD_DIM = 64
ROPE_BASE = 10000.0
LRU_WIDTH = 256
LRU_BLOCKS = 4
CONV_WIDTH = 4
CONV_LEFT = 2
LRU_C = 8.0
SWA_HEADS = 4
SWA_KV_HEADS = 2
WINDOW = 128
DIFF_HEADS = 4
NEG_INF = -1e30

LANES = 128
SUBLANES = 8
VMEM_LIMIT_BYTES = 56 * 1024 * 1024

TOKEN_TILE = 512
FFN_CHUNKS = 1
FFN_STAGE_COLS = 512
FFN_STAGE_ROWS = 128
ADA_COLS = 1024
ATT_TQ = 256
LRU_CHUNK = 256

QK_WIDTH = 1408
COL_SQ, COL_DQ, COL_DK, COL_SK = 0, 256, 768, 1280
VT_ROWS = 640
ROW_DV, ROW_SV = 0, 512
IN_LRU, IN_SQ, IN_SK, IN_SV, IN_DQ, IN_DK, IN_DV = 0, 512, 768, 896, 1024, 1536, 2048
LOG2E = math.log2(math.e)
DIFF_ROWS = 16
DIFF_CHUNK = 1024
DIFF_HEADS_STEP = 2
DIFF_PAIRS_PER_TRIP = 1
ONES_ROWS = 16
SWA_HEAD_ORDER = (0, 2, 1, 3)


def _cparams(semantics):
    return pltpu.CompilerParams(dimension_semantics=semantics, vmem_limit_bytes=VMEM_LIMIT_BYTES)


def _resident(shape):
    nd = len(shape)
    return pl.BlockSpec(shape, lambda *_: (0,) * nd, pipeline_mode=pl.Buffered(1))


def _layer_resident(stacked, layer):
    nd = stacked.ndim - 1
    return pl.BlockSpec((None,) + stacked.shape[1:], lambda *_: (layer,) + (0,) * nd, pipeline_mode=pl.Buffered(1))


def _dot(a, b):
    return jnp.dot(a, b, preferred_element_type=F32)


def _dot_nt(a, b):
    return lax.dot_general(a, b, (((1,), (1,)), ((), ())), preferred_element_type=F32)


def _rms(x, g):
    return x * lax.rsqrt(jnp.mean(x * x, axis=-1, keepdims=True) + EPS) * g


def _modnorm(h, g, mod_ref, k):
    shift = mod_ref[0, 3 * k:3 * k + 1, :]
    scale = mod_ref[0, 3 * k + 1:3 * k + 2, :]
    return _rms(h, g) * (1.0 + scale) + shift


def _neg_expm1_2x(a, x):
    return (1.0 + a * a) * jnp.tanh(-x)


def _ada_kernel(c_ref, w_ref, b_ref, o_ref):
    c = c_ref[...]
    s = (c * jax.nn.sigmoid(c)).astype(BF16)
    o_ref[0] = _dot(s, w_ref[0].astype(BF16)) + b_ref[0]


def _ada(cond, w_ada, b_ada):
    L, D, N = w_ada.shape
    R = cond.shape[0]
    return pl.pallas_call(
        _ada_kernel,
        grid=(L, N // ADA_COLS),
        in_specs=[
            pl.BlockSpec((R, D), lambda l, n: (0, 0)),
            pl.BlockSpec((1, D, ADA_COLS), lambda l, n: (l, 0, n)),
            pl.BlockSpec((1, 1, ADA_COLS), lambda l, n: (l, 0, n)),
        ],
        out_specs=pl.BlockSpec((1, R, ADA_COLS), lambda l, n: (l, 0, n)),
        out_shape=jax.ShapeDtypeStruct((L, R, N), F32),
        compiler_params=_cparams(("arbitrary", "arbitrary")),
        name="ada",
    )(cond, w_ada, b_ada.reshape(L, 1, N))


def _mod_index(n_lat_tiles, tiles_per_batch, n_batch):
    def idx(i):
        return (jnp.where(i < n_lat_tiles, i // tiles_per_batch, n_batch), 0, 0)
    return idx


def _stream_to_bf16(streams, order, ahead=2):
    tasks, seen = [], [0] * len(streams)
    for s in order:
        tasks.append((s, seen[s]))
        seen[s] += 1

    def copy(t):
        s, c = tasks[t]
        src_chunk, _, stage, sem = streams[s]
        return pltpu.make_async_copy(src_chunk(c), stage.at[c % 2], sem.at[c % 2])

    for t in range(min(ahead, len(tasks))):
        copy(t).start()
    for t, (s, c) in enumerate(tasks):
        copy(t).wait()
        _, dst_chunk, stage, _ = streams[s]
        dst_chunk(c)[...] = stage[c % 2].astype(BF16)
        if t + ahead < len(tasks):
            copy(t + ahead).start()


def _ffn_kernel(*refs, layer, k, d_ff, final_norm, n_lat_tiles, split_input, n_mix, mix_has_ctx):
    if n_mix:
        wo_bf = refs[-1]
        refs = refs[:-1]
    wgu_ref, wd_ref, st_gu, st_d, sem_gu, sem_d = refs[-6:]
    refs = refs[:-6]
    mod_ref, g_ref, wgu_hbm, wd_hbm, fg_ref, o_ref = refs[-6:]

    @pl.when(pl.program_id(0) == 0)
    def _():
        cw = st_gu.shape[2]
        rw = st_d.shape[1]
        gu = (lambda c: wgu_hbm.at[layer, :, pl.ds(c * cw, cw)], lambda c: wgu_ref.at[:, pl.ds(c * cw, cw)],
              st_gu, sem_gu)
        dn = (lambda c: wd_hbm.at[layer, pl.ds(c * rw, rw), :], lambda c: wd_ref.at[pl.ds(c * rw, rw), :],
              st_d, sem_d)
        n_gu, n_dn = 2 * d_ff // cw, d_ff // rw
        marks = sorted([((c + 0.5) / n_gu, 0) for c in range(n_gu)] + [((c + 0.5) / n_dn, 1) for c in range(n_dn)])
        _stream_to_bf16((gu, dn), [s for _, s in marks])
        if n_mix:
            wo_bf[...] = refs[-7][...].astype(BF16)

    is_lat = pl.program_id(0) < n_lat_tiles
    if split_input:
        h = jnp.where(is_lat, refs[0][...], refs[1][...])
    else:
        h = refs[0][...]
    if n_mix:
        mix_refs = refs[-7 - 2 * n_mix:-7]
        parts = []
        for lat_ref, ctx_ref in zip(mix_refs[0::2], mix_refs[1::2]):
            parts.append(jnp.where(is_lat, lat_ref[...], ctx_ref[...]) if mix_has_ctx else lat_ref[...])
        h = h + mod_ref[0, 5:6, :] * _dot(jnp.concatenate(parts, axis=1), wo_bf[...])
    xn = _modnorm(h, g_ref[...], mod_ref, k).astype(BF16)
    tf = d_ff // FFN_CHUNKS
    acc = None
    for c in range(FFN_CHUNKS):
        lo = c * tf
        g = _dot(xn, wgu_ref[:, lo:lo + tf])
        u = _dot(xn, wgu_ref[:, d_ff + lo:d_ff + lo + tf])
        a = (g * jax.nn.sigmoid(g) * u).astype(BF16)
        part = _dot(a, wd_ref[lo:lo + tf, :])
        acc = part if acc is None else acc + part
    gate = mod_ref[0, 3 * k + 2:3 * k + 3, :]
    out = h + (FFN_RES * gate) * acc
    if final_norm:
        out = _rms(out, fg_ref[...])
    o_ref[...] = out


def _ffn(h, mod, g, w_gu, w_down, final_g, *, layer, k, n_tiles, geom, final_norm=False, mix=(), w_out=None):
    split = isinstance(h, tuple)
    hs = h if split else (h,)
    D = hs[0].shape[1]
    d_ff = w_down.shape[1]
    tm = TOKEN_TILE
    n_lat_tiles = geom[0]

    def lat_idx(i):
        return (jnp.minimum(i, n_lat_tiles - 1), 0)

    def ctx_idx(i):
        return (jnp.maximum(i - n_lat_tiles, 0), 0)

    if split:
        h_specs = [pl.BlockSpec((tm, D), lat_idx), pl.BlockSpec((tm, D), ctx_idx)]
    else:
        h_specs = [pl.BlockSpec((tm, D), lambda i: (i, 0))]
    mix_args, mix_specs = [], []
    for lat, ctx in mix:
        mix_args += [lat, ctx]
        mix_specs += [pl.BlockSpec((tm, lat.shape[1]), lat_idx), pl.BlockSpec((tm, lat.shape[1]), ctx_idx)]
    if mix:
        mix_args.append(w_out)
        mix_specs.append(_layer_resident(w_out, layer))
    return pl.pallas_call(
        functools.partial(_ffn_kernel, layer=layer, k=k, d_ff=d_ff, final_norm=final_norm, n_lat_tiles=n_lat_tiles,
                          split_input=split, n_mix=len(mix), mix_has_ctx=n_tiles > n_lat_tiles),
        grid=(n_tiles,),
        in_specs=h_specs + mix_specs + [
            pl.BlockSpec((1, N_MOD, D), _mod_index(*geom)),
            _resident((1, D)),
            pl.BlockSpec(memory_space=pl.ANY),
            pl.BlockSpec(memory_space=pl.ANY),
            _resident((1, D)),
        ],
        out_specs=pl.BlockSpec((tm, D), lambda i: (i, 0)),
        out_shape=jax.ShapeDtypeStruct((n_tiles * tm, D), F32),
        scratch_shapes=[
            pltpu.VMEM((D, 2 * d_ff), BF16),
            pltpu.VMEM((d_ff, D), BF16),
            pltpu.VMEM((2, D, FFN_STAGE_COLS), F32),
            pltpu.VMEM((2, FFN_STAGE_ROWS, D), F32),
            pltpu.SemaphoreType.DMA((2,)),
            pltpu.SemaphoreType.DMA((2,)),
        ] + ([pltpu.VMEM(w_out.shape[1:], BF16)] if mix else []),
        compiler_params=_cparams(("arbitrary",)),
        name="ffn",
    )(*hs, *mix_args, mod, g, w_gu, w_down, final_g)


def _rope(x, cos, sin):
    lane = lax.broadcasted_iota(jnp.int32, x.shape, 1)
    first = (lane & (HEAD_DIM - 1)) < (HEAD_DIM // 2)
    partner = jnp.where(first, pltpu.roll(x, LANES - HEAD_DIM // 2, 1), pltpu.roll(x, HEAD_DIM // 2, 1))
    return x * cos + partner * sin


def _inproj_kernel(h_ref, mod_ref, g_ref, w_ref, cos_ref, sin_ref, lru_ref, qk_ref, vt_ref, w_bf):
    @pl.when(pl.program_id(0) == 0)
    def _():
        w_bf[...] = w_ref[...].astype(BF16)

    xn = _modnorm(h_ref[...], g_ref[...], mod_ref, 1).astype(BF16)
    y = _dot(xn, w_bf[...])
    lru_ref[...] = y[:, IN_LRU:IN_SQ]
    cos = cos_ref[...]
    sin = sin_ref[...]
    q_scale = HEAD_DIM ** -0.5 * LOG2E

    def roped(c0, scale=None):
        blk = _rope(y[:, c0:c0 + LANES], cos, sin)
        return blk if scale is None else blk * scale

    def put(c0, blk):
        qk_ref[:, c0:c0 + LANES] = blk.astype(BF16)

    qa, qb = roped(IN_SQ, q_scale), roped(IN_SQ + LANES, q_scale)
    low = lax.broadcasted_iota(jnp.int32, qa.shape, 1) < HEAD_DIM
    put(COL_SQ, jnp.where(low, qa, pltpu.roll(qb, HEAD_DIM, 1)))
    put(COL_SQ + LANES, jnp.where(low, pltpu.roll(qa, HEAD_DIM, 1), qb))
    put(COL_SK, roped(IN_SK))
    for c in range(0, DIFF_HEADS * 2 * HEAD_DIM, LANES):
        put(COL_DQ + c, roped(IN_DQ + c, q_scale))
        put(COL_DK + c, roped(IN_DK + c))
    sv_w = SWA_KV_HEADS * HEAD_DIM
    vt_ref[ROW_SV:ROW_SV + sv_w, :] = y[:, IN_SV:IN_SV + sv_w].T.astype(BF16)
    dv_w = DIFF_HEADS * 2 * HEAD_DIM
    vt_ref[ROW_DV:ROW_DV + dv_w, :] = y[:, IN_DV:IN_DV + dv_w].T.astype(BF16)


def _inproj_tile(S, n_ctx_rows):
    big = 2 * TOKEN_TILE
    return big if S % big == 0 and n_ctx_rows % big == 0 else TOKEN_TILE


def _inproj(h, mod, g, w_in, cos_t, sin_t, *, layer, B, S):
    T, D = h.shape
    tm = _inproj_tile(S, T - B * S)
    n_tiles = T // tm
    geom = (B * S // tm, S // tm, B)
    n_lat_tiles, tiles_per_batch, _ = geom

    def tab_idx(i):
        return (jnp.where(i < n_lat_tiles, i % tiles_per_batch, tiles_per_batch), 0)

    return pl.pallas_call(
        _inproj_kernel,
        grid=(n_tiles,),
        in_specs=[
            pl.BlockSpec((tm, D), lambda i: (i, 0)),
            pl.BlockSpec((1, N_MOD, D), _mod_index(*geom)),
            _resident((1, D)),
            _layer_resident(w_in, layer),
            pl.BlockSpec((tm, LANES), tab_idx),
            pl.BlockSpec((tm, LANES), tab_idx),
        ],
        out_specs=[
            pl.BlockSpec((tm, 2 * LRU_WIDTH), lambda i: (i, 0)),
            pl.BlockSpec((tm, QK_WIDTH), lambda i: (i, 0)),
            pl.BlockSpec((VT_ROWS, tm), lambda i: (0, i)),
        ],
        out_shape=[
            jax.ShapeDtypeStruct((T, 2 * LRU_WIDTH), F32),
            jax.ShapeDtypeStruct((T, QK_WIDTH), BF16),
            jax.ShapeDtypeStruct((VT_ROWS, T), BF16),
        ],
        scratch_shapes=[pltpu.VMEM(w_in.shape[1:], BF16)],
        compiler_params=_cparams(("arbitrary",)),
        name="inproj",
    )(h, mod, g, w_in, cos_t, sin_t)


def _lru_kernel(lat_ref, ctx_ref, cw_ref, cb_ref, wr_ref, br_ref, wi_ref, bi_ref, lam_ref,
                ylat_ref, yctx_ref, xpad, u_scr, hf_scr, ge_scr, *g_scr, S, Lc):
    W = LRU_WIDTH
    TC = LRU_CHUNK
    PAD = SUBLANES
    row = lax.broadcasted_iota(jnp.int32, (TC, W), 0)
    group = lax.broadcasted_iota(jnp.int32, (TC // SUBLANES, W), 0)

    def conv_chunk(start):
        a = xpad[pl.ds(start, TC + 2 * PAD), :]
        u = cb_ref[...] + jnp.zeros((TC, W), F32)
        for k in range(CONV_WIDTH):
            sh = (CONV_LEFT - k) % (TC + 2 * PAD)
            r = a if sh == 0 else pltpu.roll(a, sh, 0)
            u = u + r[PAD:PAD + TC] * cw_ref[k:k + 1, :]
        return u

    def scan_chunk(u, d, carry, reverse):
        r = jax.nn.sigmoid(_dot(u, wr_ref[d]) + br_ref[d])
        i = jax.nn.sigmoid(_dot(u, wi_ref[d]) + bi_ref[d])
        log_a = (-LRU_C) * r * jax.nn.softplus(-lam_ref[d])
        A = jnp.exp(log_a)
        Bv = jnp.sqrt(_neg_expm1_2x(A, log_a)) * i * u

        def combine(A, Bv, pos, axis):
            n = A.shape[axis]
            s = 1
            while s < n:
                if reverse:
                    keep = pos < n - s
                    sh = n - s
                else:
                    keep = pos >= s
                    sh = s
                a_sh = jnp.where(keep, pltpu.roll(A, sh, axis), 1.0)
                b_sh = jnp.where(keep, pltpu.roll(Bv, sh, axis), 0.0)
                Bv = A * b_sh + Bv
                A = A * a_sh
                s *= 2
            return A, Bv

        G = SUBLANES
        ng = TC // G
        A, Bv = combine(A.reshape(ng, G, W), Bv.reshape(ng, G, W), lax.broadcasted_iota(jnp.int32, (ng, G, W), 1), 1)
        A = A.reshape(TC, W)
        Bv = Bv.reshape(TC, W)
        edge = 0 if reverse else G - 1

        def boundary_rows(planes, x):
            out = []
            for c, scr in enumerate(planes):
                scr[...] = x[:, c * LANES:(c + 1) * LANES]
                out.append(scr[pl.ds(edge, ng, stride=G), :])
            return jnp.concatenate(out, axis=1)

        n_pl = W // LANES
        Ag, Bg = combine(boundary_rows(g_scr[:n_pl], A), boundary_rows(g_scr[n_pl:], Bv), group, 0)
        ge_scr[...] = Ag * carry + Bg
        parts = []
        for g in range(ng):
            nb_g = g + 1 if reverse else g - 1
            h_in = carry if (nb_g < 0 or nb_g >= ng) else ge_scr[nb_g:nb_g + 1, :]
            parts.append(A[g * G:(g + 1) * G] * h_in + Bv[g * G:(g + 1) * G])
        h = jnp.concatenate(parts, axis=0)
        return h, (h[0:1] if reverse else h[TC - 1:TC])

    def gelu(x):
        return jax.nn.gelu(x)

    zero_pad = jnp.zeros((PAD, W), F32)
    zero_state = jnp.zeros((1, W), F32)

    xpad[0:PAD, :] = zero_pad
    xpad[PAD:PAD + Lc, :] = ctx_ref[:, 0:W]
    xpad[PAD + Lc:2 * PAD + Lc, :] = zero_pad
    uc = conv_chunk(0)
    hcf, carry_f = scan_chunk(uc, 0, zero_state, False)
    hcb, carry_b = scan_chunk(uc, 1, zero_state, True)
    yctx_ref[...] = ((hcf + hcb) * gelu(ctx_ref[:, W:2 * W])).astype(BF16)

    xpad[PAD:PAD + S, :] = lat_ref[:, 0:W]
    xpad[PAD + S:2 * PAD + S, :] = zero_pad
    nc = S // TC

    def conv_body(c, _):
        st = pl.multiple_of(c * TC, TC)
        u_scr[pl.ds(st, TC), :] = conv_chunk(st)
        return 0

    lax.fori_loop(0, nc, conv_body, 0)

    def fwd_body(c, carry):
        st = pl.multiple_of(c * TC, TC)
        h, carry = scan_chunk(u_scr[pl.ds(st, TC), :], 0, carry, False)
        hf_scr[pl.ds(st, TC), :] = h
        return carry

    lax.fori_loop(0, nc, fwd_body, carry_f)

    def bwd_body(c, carry):
        st = pl.multiple_of((nc - 1 - c) * TC, TC)
        h, carry = scan_chunk(u_scr[pl.ds(st, TC), :], 1, carry, True)
        y = (hf_scr[pl.ds(st, TC), :] + h) * gelu(lat_ref[pl.ds(st, TC), W:2 * W])
        ylat_ref[pl.ds(st, TC), :] = y.astype(BF16)
        return carry

    lax.fori_loop(0, nc, bwd_body, carry_b)


def _lru(lru_in, conv_w, conv_b, w_r, b_r, w_i, b_i, lam, *, B, S, Lc):
    W = LRU_WIDTH
    t_lat = B * S
    ctx_blk0 = t_lat // Lc
    return pl.pallas_call(
        functools.partial(_lru_kernel, S=S, Lc=Lc),
        grid=(B,),
        in_specs=[
            pl.BlockSpec((S, 2 * W), lambda b: (b, 0)),
            pl.BlockSpec((Lc, 2 * W), lambda b: (ctx_blk0 + b, 0)),
            _resident(conv_w.shape),
            _resident(conv_b.shape),
            _resident(w_r.shape),
            _resident(b_r.shape),
            _resident(w_i.shape),
            _resident(b_i.shape),
            _resident(lam.shape),
        ],
        out_specs=[
            pl.BlockSpec((S, W), lambda b: (b, 0)),
            pl.BlockSpec((Lc, W), lambda b: (b, 0)),
        ],
        out_shape=[
            jax.ShapeDtypeStruct((t_lat, W), BF16),
            jax.ShapeDtypeStruct((B * Lc, W), BF16),
        ],
        scratch_shapes=[
            pltpu.VMEM((S + 2 * SUBLANES, W), F32),
            pltpu.VMEM((S, W), F32),
            pltpu.VMEM((S, W), F32),
            pltpu.VMEM((LRU_CHUNK // SUBLANES, W), F32),
        ] + [pltpu.VMEM((LRU_CHUNK, LANES), F32)] * (2 * (W // LANES)) + [
        ],
        compiler_params=_cparams(("arbitrary",)),
        name="lru",
    )(lru_in, lru_in, conv_w, conv_b, w_r, b_r, w_i, b_i, lam)


def _swa_kernel(sink_ref, q_ref, k_ref, vt_ref, qc_ref, kc_ref, vtc_ref, o_ref, oc_ref,
                vt3, s_a, s_b, p_a, p_b, *, S, Lc):
    TQ = ATT_TQ
    KB = TQ + 2 * WINDOW
    NKB = KB + Lc
    hd = HEAD_DIM
    nh = SWA_HEADS
    n_lat_ch = S // LANES
    n_ctx_ch = Lc // LANES
    band_ch = KB // LANES
    ones = jnp.ones((ONES_ROWS, LANES), BF16)
    for c in range(n_lat_ch + n_ctx_ch):
        src = vt_ref[:, c * LANES:(c + 1) * LANES] if c < n_lat_ch else \
            vtc_ref[:, (c - n_lat_ch) * LANES:(c - n_lat_ch + 1) * LANES]
        vt3[c, 0:2 * hd, :] = src
        vt3[c, 2 * hd:2 * hd + ONES_ROWS, :] = ones
    low = lax.broadcasted_iota(jnp.int32, (TQ, LANES), 1) < hd
    col = lax.broadcasted_iota(jnp.int32, (1, nh * TQ), 1)
    sink = jnp.full((1, nh * TQ), sink_ref[0, SWA_HEAD_ORDER[-1]] * LOG2E, F32)
    for c in range(nh - 2, -1, -1):
        sink = jnp.where(col < (c + 1) * TQ, sink_ref[0, SWA_HEAD_ORDER[c]] * LOG2E, sink)

    def band_start(t):
        return jnp.clip(2 * t - 1, 0, n_lat_ch - band_ch)

    def scores(q, s_dst, t=None):
        zero = jnp.zeros((TQ, LANES), BF16)
        rows = []
        for grp in range(2):
            qg = q[:, grp * LANES:(grp + 1) * LANES]
            rows += [jnp.where(low, qg, zero), jnp.where(low, zero, qg)]
        q4 = jnp.concatenate(rows, axis=0)
        s_c = _dot_nt(kc_ref[...], q4)
        s_dst[KB:NKB, :] = s_c
        m = jnp.max(s_c, axis=0, keepdims=True)
        if t is not None:
            st = band_start(t) * LANES
            kb = k_ref[pl.ds(pl.multiple_of(st, LANES), KB), :]
            k_abs = st + lax.broadcasted_iota(jnp.int32, (KB, TQ), 0)
            q_abs = t * TQ + lax.broadcasted_iota(jnp.int32, (KB, TQ), 1)
            valid = jnp.abs(q_abs - k_abs) <= WINDOW
            s_l = _dot_nt(kb, q4)
            s_l = jnp.concatenate(
                [jnp.where(valid, s_l[:, c * TQ:(c + 1) * TQ], NEG_INF) for c in range(nh)], axis=1)
            s_dst[0:KB, :] = s_l
            m = jnp.maximum(m, jnp.max(s_l, axis=0, keepdims=True))
        return jnp.maximum(m, sink)

    def probs(s_src, m, p_dst, lo=0):
        p_dst[lo:NKB, :] = jnp.exp2(s_src[lo:NKB, :] - m).astype(BF16)

    def attend(p_src, m, t=None):
        ctx_v = [vt3[n_lat_ch + c] for c in range(n_ctx_ch)]
        if t is None:
            lo, vt = KB, jnp.concatenate(ctx_v, axis=1)
        else:
            st = band_start(t)
            lo, vt = 0, jnp.concatenate([vt3[st + d] for d in range(band_ch)] + ctx_v, axis=1)
        acc = _dot(vt, p_src[lo:NKB, :])
        den = acc[2 * hd:2 * hd + 1] + jnp.exp2(sink - m)
        o_n = acc[0:2 * hd] * (1.0 / den)
        pieces = []
        for h in range(nh):
            c = SWA_HEAD_ORDER.index(h)
            pieces.append(o_n[(c % 2) * hd:(c % 2 + 1) * hd, c * TQ:(c + 1) * TQ])
        return jnp.concatenate(pieces, axis=0).T.astype(BF16)

    def q_block(i):
        return q_ref[pl.ds(pl.multiple_of(i * TQ, TQ), TQ), :]

    def emit(i, y):
        o_ref[pl.ds(pl.multiple_of(i * TQ, TQ), TQ), :] = y

    nb = S // TQ
    m_a = scores(q_block(0), s_a, 0)
    m_b = scores(q_block(1), s_b, 1)
    probs(s_a, m_a, p_a)

    def pair(j, carry):
        m_b, m_pa = carry
        b = 2 * j
        m_a = scores(q_block(b + 2), s_a, b + 2)
        probs(s_b, m_b, p_b)
        emit(b, attend(p_a, m_pa, b))
        m_b2 = scores(q_block(b + 3), s_b, b + 3)
        probs(s_a, m_a, p_a)
        emit(b + 1, attend(p_b, m_b, b + 1))
        return m_b2, m_a

    m_b, m_pa = lax.fori_loop(0, nb // 2 - 1, pair, (m_b, m_a))
    probs(s_b, m_b, p_b)
    emit(nb - 2, attend(p_a, m_pa, nb - 2))
    emit(nb - 1, attend(p_b, m_b, nb - 1))

    m_c = scores(qc_ref[...], s_a)
    probs(s_a, m_c, p_a, KB)
    oc_ref[...] = attend(p_a, m_c)


def _swa(qk, vt, sink, *, B, S, Lc):
    qw = SWA_HEADS * HEAD_DIM
    ctx_blk0 = B * S // Lc
    sv_blk = ROW_SV // LANES
    NKB = ATT_TQ + 2 * WINDOW + Lc
    return pl.pallas_call(
        functools.partial(_swa_kernel, S=S, Lc=Lc),
        grid=(B,),
        in_specs=[
            pl.BlockSpec(memory_space=pltpu.SMEM),
            pl.BlockSpec((S, qw), lambda b: (b, COL_SQ // qw)),
            pl.BlockSpec((S, LANES), lambda b: (b, COL_SK // LANES)),
            pl.BlockSpec((LANES, S), lambda b: (sv_blk, b)),
            pl.BlockSpec((Lc, qw), lambda b: (ctx_blk0 + b, COL_SQ // qw)),
            pl.BlockSpec((Lc, LANES), lambda b: (ctx_blk0 + b, COL_SK // LANES)),
            pl.BlockSpec((LANES, Lc), lambda b: (sv_blk, ctx_blk0 + b)),
        ],
        out_specs=[
            pl.BlockSpec((S, qw), lambda b: (b, 0)),
            pl.BlockSpec((Lc, qw), lambda b: (b, 0)),
        ],
        out_shape=[
            jax.ShapeDtypeStruct((B * S, qw), BF16),
            jax.ShapeDtypeStruct((B * Lc, qw), BF16),
        ],
        scratch_shapes=[
            pltpu.VMEM(((S + Lc) // LANES, 2 * HEAD_DIM + ONES_ROWS, LANES), BF16),
            pltpu.VMEM((NKB, SWA_HEADS * ATT_TQ), F32),
            pltpu.VMEM((NKB, SWA_HEADS * ATT_TQ), F32),
            pltpu.VMEM((NKB, SWA_HEADS * ATT_TQ), BF16),
            pltpu.VMEM((NKB, SWA_HEADS * ATT_TQ), BF16),
        ],
        compiler_params=_cparams(("arbitrary",)),
        name="swa",
    )(sink, qk, qk, vt, qk, qk, vt)


def _diff_kernel(dl_ref, g_ref, q_ref, k_ref, vt_ref, qc_ref, kc_ref, vtc_ref, o_ref, oc_ref,
                 vt_scr, s_a, s_b, p_a, p_b, *, S, Lc, lambda_init):
    TQ = ATT_TQ
    NK = S + Lc
    dv = 2 * HEAD_DIM
    nhs = DIFF_HEADS_STEP
    for hi in range(nhs):
        vt_scr[hi, 0:dv, 0:S] = vt_ref[hi * dv:(hi + 1) * dv, :]
        vt_scr[hi, 0:dv, S:NK] = vtc_ref[hi * dv:(hi + 1) * dv, :]
        vt_scr[hi, dv:dv + ONES_ROWS, :] = jnp.ones((ONES_ROWS, NK), BF16)
    dl = dl_ref[...]
    lam = (jnp.exp(jnp.sum(dl[0:1] * dl[1:2], axis=-1, keepdims=True))
           - jnp.exp(jnp.sum(dl[2:3] * dl[3:4], axis=-1, keepdims=True)) + lambda_init)
    low = lax.broadcasted_iota(jnp.int32, (TQ, LANES), 1) < HEAD_DIM
    lat_pieces = [(r, r + DIFF_CHUNK, k_ref, r) for r in range(0, S, DIFF_CHUNK)]
    ctx_piece = (S, NK, kc_ref, 0)

    def head_lanes(hi):
        return slice(hi * LANES, (hi + 1) * LANES)

    def scores(q, hi, s_dst, lo=0):
        zero = jnp.zeros_like(q)
        q2 = jnp.concatenate([jnp.where(low, q, zero), jnp.where(low, zero, q)], axis=0)
        m_acc = None
        for r0, r1, kref, off in ([] if lo else lat_pieces) + [ctx_piece]:
            blk = _dot_nt(kref[off:off + (r1 - r0), head_lanes(hi)], q2)
            s_dst[r0:r1, :] = blk
            for r in range(0, r1 - r0, DIFF_ROWS):
                part = blk[r:r + DIFF_ROWS]
                m_acc = part if m_acc is None else jnp.maximum(m_acc, part)
        return jnp.max(m_acc, axis=0, keepdims=True)

    def probs(s_src, m, p_dst, lo=0):
        p_dst[lo:NK, :] = jnp.exp2(s_src[lo:NK, :] - m).astype(BF16)

    def attend(p_src, hi, lo=0):
        acc = _dot(vt_scr[hi, :, lo:NK], p_src[lo:NK, :])
        o_n = acc[0:dv] * (1.0 / acc[dv:dv + 1])
        o_t = o_n[:, 0:TQ] - lam * o_n[:, TQ:2 * TQ]
        ms = jnp.mean(o_t * o_t, axis=0, keepdims=True)
        y_t = o_t * (lax.rsqrt(ms + EPS) * (1.0 - lambda_init))
        return (y_t.T * g_ref[...]).astype(BF16)

    def block_rows(i):
        return pl.ds(i * TQ, TQ) if isinstance(i, int) else pl.ds(pl.multiple_of(i * TQ, TQ), TQ)

    def q_block(hi, i):
        return q_ref[block_rows(i), head_lanes(hi)]

    def emit(hi, i, y):
        o_ref[block_rows(i), head_lanes(hi)] = y

    def pair(hi, b, m_b):
        m_a = scores(q_block(hi, b + 2), hi, s_a)
        probs(s_b, m_b, p_b)
        emit(hi, b, attend(p_a, hi))
        m_b = scores(q_block(hi, b + 3), hi, s_b)
        probs(s_a, m_a, p_a)
        emit(hi, b + 1, attend(p_b, hi))
        return m_b

    nb = S // TQ
    n_pairs = nb // 2 - 1
    n_trips = n_pairs // DIFF_PAIRS_PER_TRIP
    m_a = scores(q_block(0, 0), 0, s_a)
    m_b = scores(q_block(0, 1), 0, s_b)
    probs(s_a, m_a, p_a)
    for hi in range(nhs):
        def trip(j, m_b, hi=hi):
            for r in range(DIFF_PAIRS_PER_TRIP):
                m_b = pair(hi, 2 * (DIFF_PAIRS_PER_TRIP * j + r), m_b)
            return m_b

        m_b = lax.fori_loop(0, n_trips, trip, m_b)
        for r in range(n_trips * DIFF_PAIRS_PER_TRIP, n_pairs):
            m_b = pair(hi, 2 * r, m_b)
        if hi + 1 < nhs:
            m_a = scores(q_block(hi + 1, 0), hi + 1, s_a)
            probs(s_b, m_b, p_b)
            emit(hi, nb - 2, attend(p_a, hi))
            m_b = scores(q_block(hi + 1, 1), hi + 1, s_b)
            probs(s_a, m_a, p_a)
            emit(hi, nb - 1, attend(p_b, hi))
        else:
            probs(s_b, m_b, p_b)
            emit(hi, nb - 2, attend(p_a, hi))
            emit(hi, nb - 1, attend(p_b, hi))

    for hi in range(nhs):
        m_c = scores(qc_ref[:, head_lanes(hi)], hi, s_a, S)
        probs(s_a, m_c, p_a, S)
        oc_ref[:, head_lanes(hi)] = attend(p_a, hi, S)


def _diff(qk, vt, diff_lambda, subln_g, *, B, S, Lc, lambda_init):
    H = DIFF_HEADS
    dv = 2 * HEAD_DIM
    nhs = DIFF_HEADS_STEP
    NK = S + Lc
    ctx_blk0 = B * S // Lc
    w = nhs * LANES
    return pl.pallas_call(
        functools.partial(_diff_kernel, S=S, Lc=Lc, lambda_init=lambda_init),
        grid=(B, H // nhs),
        in_specs=[
            _resident(diff_lambda.shape),
            _resident(subln_g.shape),
            pl.BlockSpec((S, w), lambda b, h: (b, COL_DQ // w + h)),
            pl.BlockSpec((S, w), lambda b, h: (b, COL_DK // w + h)),
            pl.BlockSpec((w, S), lambda b, h: (ROW_DV // w + h, b)),
            pl.BlockSpec((Lc, w), lambda b, h: (ctx_blk0 + b, COL_DQ // w + h)),
            pl.BlockSpec((Lc, w), lambda b, h: (ctx_blk0 + b, COL_DK // w + h)),
            pl.BlockSpec((w, Lc), lambda b, h: (ROW_DV // w + h, ctx_blk0 + b)),
        ],
        out_specs=[
            pl.BlockSpec((S, w), lambda b, h: (b, h)),
            pl.BlockSpec((Lc, w), lambda b, h: (b, h)),
        ],
        out_shape=[
            jax.ShapeDtypeStruct((B * S, H * dv), BF16),
            jax.ShapeDtypeStruct((B * Lc, H * dv), BF16),
        ],
        scratch_shapes=[
            pltpu.VMEM((nhs, dv + ONES_ROWS, NK), BF16),
            pltpu.VMEM((NK, 2 * ATT_TQ), F32),
            pltpu.VMEM((NK, 2 * ATT_TQ), F32),
            pltpu.VMEM((NK, 2 * ATT_TQ), BF16),
            pltpu.VMEM((NK, 2 * ATT_TQ), BF16),
        ],
        compiler_params=_cparams(("arbitrary", "arbitrary")),
        name="diff",
    )(diff_lambda, subln_g, qk, qk, vt, qk, qk, vt)


def _rope_tables(S, tm):
    rows = S // GRID_W
    row = jnp.repeat(jnp.arange(rows, dtype=F32), GRID_W)
    col = jnp.tile(jnp.arange(GRID_W, dtype=F32), rows)
    n_freq = HEAD_DIM // 4
    inv_freq = ROPE_BASE ** (-jnp.arange(n_freq, dtype=F32) / n_freq)
    ang = jnp.concatenate([row[:, None] * inv_freq, col[:, None] * inv_freq], axis=-1)
    cos, sin = jnp.cos(ang), jnp.sin(ang)
    reps = LANES // (HEAD_DIM // 2)
    sign = np.tile(np.concatenate([-np.ones(HEAD_DIM // 2), np.ones(HEAD_DIM // 2)]), LANES // HEAD_DIM)
    cos_t = jnp.concatenate([jnp.tile(cos, (1, reps)), jnp.ones((tm, LANES), F32)], axis=0)
    sin_t = jnp.concatenate([jnp.tile(sin, (1, reps)) * sign.astype(np.float32), jnp.zeros((tm, LANES), F32)], axis=0)
    return cos_t, sin_t


def _block_diag(w):
    nd, K, c, _ = w.shape
    eye = jnp.eye(K, dtype=w.dtype)
    return jnp.einsum('dkij,kl->dkilj', w, eye).reshape(nd, K * c, K * c)


def kernel(x, c, ctx, c_ctx, w_ada, b_ada, norm_g, ffn1_w_gu, ffn1_w_down, ffn2_w_gu, ffn2_w_down,
           w_in, w_out, conv_w, conv_b, lru_w_r, lru_b_r, lru_w_i, lru_b_i, lru_lambda,
           swa_sink, diff_lambda, diff_subln_g, final_g):
    B, S, D = x.shape
    Lc = ctx.shape[1]
    depth = w_ada.shape[0]
    tm = TOKEN_TILE
    assert S % tm == 0 and (B * Lc) % tm == 0 and Lc == ATT_TQ and S % GRID_W == 0 and S % LRU_CHUNK == 0
    assert Lc == LRU_CHUNK and S >= ATT_TQ + 2 * WINDOW
    assert S % DIFF_CHUNK == 0 and (S // ATT_TQ) % 2 == 0
    n_lat_tiles = B * S // tm
    n_tiles = n_lat_tiles + B * Lc // tm
    geom = (n_lat_tiles, S // tm, B)

    h = (x.reshape(B * S, D), ctx.reshape(B * Lc, D))

    n_cond = B + 1
    pad = (-n_cond) % SUBLANES
    cond = jnp.concatenate([c, c_ctx[None], jnp.zeros((pad, D), F32)], axis=0)
    mod_all = _ada(cond, w_ada, b_ada).reshape(depth, n_cond + pad, N_MOD, D)

    cos_t, sin_t = _rope_tables(S, _inproj_tile(S, B * Lc))
    w1_gu, w1_down, w2_gu, w2_down = ffn1_w_gu, ffn1_w_down, ffn2_w_gu, ffn2_w_down

    for l in range(depth):
        last = l == depth - 1
        mod = mod_all[l]
        lambda_init = 0.8 - 0.6 * math.exp(-0.3 * l)
        g = norm_g[l]
        h = _ffn(h, mod, g[0:1], w1_gu, w1_down, final_g[None], layer=l, k=0, n_tiles=n_tiles, geom=geom)
        lru_in, qk, vt = _inproj(h, mod, g[1:2], w_in, cos_t, sin_t, layer=l, B=B, S=S)
        y_lru = _lru(lru_in, conv_w[l], conv_b[l][None], _block_diag(lru_w_r[l]), lru_b_r[l][:, None],
                     _block_diag(lru_w_i[l]), lru_b_i[l][:, None], lru_lambda[l][:, None], B=B, S=S, Lc=Lc)
        y_swa = _swa(qk, vt, swa_sink[l][None], B=B, S=S, Lc=Lc)
        y_diff = _diff(qk, vt, diff_lambda[l], diff_subln_g[l][None], B=B, S=S, Lc=Lc, lambda_init=lambda_init)
        n_out = n_lat_tiles if last else n_tiles
        h = _ffn(h, mod, g[2:3], w2_gu, w2_down, final_g[None], layer=l, k=2, n_tiles=n_out, geom=geom,
                 final_norm=last, mix=(y_lru, y_swa, y_diff), w_out=w_out)
    return h[:B * S].reshape(B, S, D)
```
